```python
import math
import jax
import jax.numpy as jnp
from jax import lax
import numpy as np

D_MODEL = 1024
BATCH = 2
SEQ = 8192
DEPTH = 4

GRID_W = 64
CTX_LEN = 256
N_MIXERS = 3
N_A = (DEPTH + 2) // N_MIXERS
N_B = (DEPTH + 1) // N_MIXERS
N_C = DEPTH // N_MIXERS
NORM_EPS = 1e-6

RW_HEAD = 64
RW_HEADS = D_MODEL // RW_HEAD
RW_DECAY_LORA = 64
RW_AAA_LORA = 64
RW_GATE_LORA = 160
RW_LNX_EPS = 6.4e-4

ML_HEADS = 4
ML_QK = D_MODEL // (2 * ML_HEADS)
ML_V = D_MODEL // ML_HEADS
ML_CHUNK = 128
ML_IN = 2 * ML_HEADS * ML_QK + ML_HEADS * ML_V + D_MODEL + 4 * ML_HEADS

DA_HEADS = 8
DA_HEAD = D_MODEL // (2 * DA_HEADS)
DA_QBLOCK = 128
ROPE_BASE = 10000.0

FFN_HIDDEN = 2816
CONV_W = 3

kernel_name = "hybrid_rwkv7_mlstm_diffattn_adaln"


def _prev(x):
    return jnp.pad(x, ((0, 0), (1, 0), (0, 0)))[:, :-1]


def _next(x):
    return jnp.pad(x, ((0, 0), (0, 1), (0, 0)))[:, 1:]


def rms_norm(x, g):
    xf = x.astype(jnp.float32)
    y = xf * lax.rsqrt(jnp.mean(xf * xf, axis=-1, keepdims=True) + NORM_EPS)
    return (y * g.astype(jnp.float32)).astype(x.dtype)


def conv_ffn(h, w_in, conv_w, conv_b, w_out):
    val, gate = jnp.split(h @ w_in, 2, axis=-1)
    gate = conv_w[0] * _prev(gate) + conv_w[1] * gate + conv_w[2] * _next(gate) + conv_b
    return (jax.nn.gelu(gate) * val) @ w_out


def rwkv7_scan(r, w, k, v, kk, a, state, reverse):
    def step(S, inp):
        r_t, w_t, k_t, v_t, kk_t, a_t = inp
        sa = jnp.einsum('bhvk,bhk->bhv', S, kk_t)
        S = (S * w_t[:, :, None, :] - sa[..., None] * (kk_t * a_t)[:, :, None, :]
             + v_t[..., None] * k_t[:, :, None, :])
        return S, jnp.einsum('bhvk,bhk->bhv', S, r_t)
    xs = tuple(jnp.moveaxis(t, 1, 0) for t in (r, w, k, v, kk, a))
    S, ys = lax.scan(step, state, xs, reverse=reverse)
    return jnp.moveaxis(ys, 0, 1), S


def rwkv7_mixer(h_lat, h_ctx, mix, w_rkv, w0, w1, w2, a0, a1, a2, g1, g2,
                k_k, k_a, r_k, lnx_g, lnx_b, w_out, with_ctx):
    H, N = RW_HEADS, RW_HEAD
    f32 = jnp.float32

    def lora(xin, p1, p2, act):
        return jnp.einsum('zbnl,zld->zbnd', act(jnp.einsum('bnd,zdl->zbnl', xin, p1)), p2)

    def prep(h):
        B, n, D = h.shape
        xx = 0.5 * (_prev(h) + _next(h)) - h
        xs = h[:, :, None, :] + xx[:, :, None, :] * mix
        rkv = jnp.einsum('bnjd,jde->bnje', xs[:, :, :3], w_rkv)
        r, k, v = rkv[:, :, 0], rkv[:, :, 1], rkv[:, :, 2]
        w_pre = (w0[:, None, None, :] + lora(xs[:, :, 3], w1, w2, jnp.tanh)).astype(f32)
        decay = jnp.exp(-jnp.exp(-jax.nn.softplus(-w_pre) - 0.5))
        a = jax.nn.sigmoid((a0[:, None, None, :]
                            + lora(xs[:, :, 4], a1, a2, lambda t: t)).astype(f32))
        g = jax.nn.sigmoid(xs[:, :, 5] @ g1) @ g2
        kk = (k * k_k).astype(f32).reshape(B, n, H, N)
        kk = kk / jnp.maximum(jnp.linalg.norm(kk, axis=-1, keepdims=True), 1e-12)
        kd = k.astype(f32) * (1.0 + (a - 1.0) * k_a.astype(f32))
        split = lambda t: t.reshape(t.shape[:-1] + (H, N))
        return (split(r.astype(f32)), split(decay), split(kd), split(v.astype(f32)),
                kk, split(a), g)

    def post(y, pr, h):
        r, _, kd, v, _, _, g = pr
        B, n, D = h.shape
        mu = jnp.mean(y, axis=-1, keepdims=True)
        var = jnp.mean(jnp.square(y - mu), axis=-1, keepdims=True)
        y = ((y - mu) * lax.rsqrt(var + RW_LNX_EPS)).reshape(B, n, D) * lnx_g + lnx_b
        bonus = jnp.sum(jnp.sum(r * kd * r_k, axis=-1, keepdims=True) * v, axis=0)
        return ((y + bonus.reshape(B, n, D)) * g).astype(h.dtype) @ w_out

    lat, ctx = prep(h_lat), prep(h_ctx)
    state0 = jnp.zeros((h_lat.shape[0], H, N, N), f32)
    ys_lat, ys_ctx = [], []
    for d, rev in ((0, False), (1, True)):
        yc, s_ctx = rwkv7_scan(ctx[0], ctx[1][d], ctx[2][d], ctx[3], ctx[4], ctx[5][d], state0, rev)
        yl, _ = rwkv7_scan(lat[0], lat[1][d], lat[2][d], lat[3], lat[4], lat[5][d], s_ctx, rev)
        ys_lat.append(yl)
        ys_ctx.append(yc)
    out_lat = post(ys_lat[0] + ys_lat[1], lat, h_lat)
    out_ctx = post(ys_ctx[0] + ys_ctx[1], ctx, h_ctx) if with_ctx else None
    return out_lat, out_ctx


def mlstm_chunked(q, k, v, li, lf, state):
    B, H, L, dk = q.shape
    dv = v.shape[-1]
    T = ML_CHUNK
    nc = L // T
    qc = q.reshape(B, H, nc, T, dk)
    kc = k.reshape(B, H, nc, T, dk)
    vc = v.reshape(B, H, nc, T, dv)
    lic = li.reshape(B, H, nc, T)
    b = jnp.cumsum(lf.reshape(B, H, nc, T), axis=-1)
    b_end = b[..., -1]
    g_end = b_end[..., None] - b + lic
    m_loc = jnp.max(g_end, axis=-1)
    w_end = jnp.exp(g_end - m_loc[..., None])
    C_loc = jnp.einsum('bhcsk,bhcsv->bhckv', kc * w_end[..., None], vc)
    n_loc = jnp.einsum('bhcs,bhcsk->bhck', w_end, kc)

    def step(carry, inp):
        C, n, m = carry
        Cl, nl, ml, be = inp
        m_new = jnp.maximum(be + m, ml)
        a = jnp.exp(be + m - m_new)
        e = jnp.exp(ml - m_new)
        new = (a[..., None, None] * C + e[..., None, None] * Cl,
               a[..., None] * n + e[..., None] * nl, m_new)
        return new, (C, n, m)

    xs = tuple(jnp.moveaxis(t, 2, 0) for t in (C_loc, n_loc, m_loc, b_end))
    final, starts = lax.scan(step, state, xs)
    C0, n0, m0 = (jnp.moveaxis(t, 0, 2) for t in starts)
    tri = jnp.tril(jnp.ones((T, T), dtype=bool))
    d_log = jnp.where(tri, b[..., :, None] - b[..., None, :] + lic[..., None, :], -jnp.inf)
    inter = b + m0[..., None]
    m_t = jnp.maximum(jnp.max(d_log, axis=-1), inter)
    p = jnp.exp(d_log - m_t[..., None]) * jnp.einsum('bhctk,bhcsk->bhcts', qc, kc)
    e_inter = jnp.exp(inter - m_t)
    num = (jnp.einsum('bhcts,bhcsv->bhctv', p, vc)
           + e_inter[..., None] * jnp.einsum('bhctk,bhckv->bhctv', qc, C0))
    den = jnp.sum(p, axis=-1) + e_inter * jnp.einsum('bhctk,bhck->bhct', qc, n0)
    h = num / jnp.maximum(jnp.abs(den), jnp.exp(-m_t))[..., None]
    return h.reshape(B, H, L, dv), final


def mlstm_mixer(h_lat, h_ctx, w_in, b_in, norm_g, w_out, with_ctx):
    H, dk, dv = ML_HEADS, ML_QK, ML_V
    f32 = jnp.float32
    cuts = [H * dk, 2 * H * dk, 2 * H * dk + H * dv, 2 * H * dk + H * dv + D_MODEL]

    def prep(h):
        B, n, _ = h.shape
        q, k, v, o, gt = jnp.split(h @ w_in + b_in, cuts, axis=-1)
        heads = lambda t, e: jnp.transpose(t.reshape(B, n, H, e), (0, 2, 1, 3)).astype(f32)
        gt = jnp.transpose(gt.astype(f32).reshape(B, n, 2, 2, H), (2, 3, 0, 4, 1))
        return (heads(q, dk), heads(k, dk) * (dk ** -0.5), heads(v, dv),
                jax.nn.sigmoid(o), gt[:, 0], jax.nn.log_sigmoid(gt[:, 1]))

    def post(hsum, o):
        B, _, n, _ = hsum.shape
        hn = hsum * lax.rsqrt(jnp.mean(hsum * hsum, axis=-1, keepdims=True) + NORM_EPS)
        hn = jnp.transpose(hn, (0, 2, 1, 3)).reshape(B, n, D_MODEL)
        return (hn * norm_g * o).astype(o.dtype) @ w_out

    lat, ctx = prep(h_lat), prep(h_ctx)
    B = h_lat.shape[0]
    state0 = (jnp.zeros((B, H, dk, dv), f32), jnp.zeros((B, H, dk), f32), jnp.zeros((B, H), f32))
    hs_lat, hs_ctx = [], []
    for d in range(2):
        fl = (lambda t: t) if d == 0 else (lambda t: jnp.flip(t, axis=2))
        hc, s_ctx = mlstm_chunked(fl(ctx[0]), fl(ctx[1]), fl(ctx[2]), fl(ctx[4][d]), fl(ctx[5][d]), state0)
        hl, _ = mlstm_chunked(fl(lat[0]), fl(lat[1]), fl(lat[2]), fl(lat[4][d]), fl(lat[5][d]), s_ctx)
        hs_lat.append(fl(hl))
        hs_ctx.append(fl(hc))
    out_lat = post(hs_lat[0] + hs_lat[1], lat[3])
    out_ctx = post(hs_ctx[0] + hs_ctx[1], ctx[3]) if with_ctx else None
    return out_lat, out_ctx


def axial_rope(n, dim):
    rows = n // GRID_W
    row = jnp.repeat(jnp.arange(rows), GRID_W).astype(jnp.float32)
    col = jnp.tile(jnp.arange(GRID_W), rows).astype(jnp.float32)
    nf = dim // 4
    inv = jnp.power(ROPE_BASE, -jnp.arange(nf, dtype=jnp.float32) / nf)
    ang = jnp.concatenate([row[:, None] * inv, col[:, None] * inv], axis=-1)
    return jnp.cos(ang), jnp.sin(ang)


def apply_rope(x, cos, sin):
    half = x.shape[-1] // 2
    x1, x2 = x[..., :half], x[..., half:]
    cos = cos[None, :, None, None, :].astype(x.dtype)
    sin = sin[None, :, None, None, :].astype(x.dtype)
    return jnp.concatenate([x1 * cos - x2 * sin, x1 * sin + x2 * cos], axis=-1)


def diff_attention_mixer(h_lat, h_ctx, w_qkv, lam, norm_g, w_out, lambda_init, with_ctx):
    H, d = DA_HEADS, DA_HEAD
    f32 = jnp.float32
    B, S, _ = h_lat.shape

    def proj(h):
        n = h.shape[1]
        q, k, v = jnp.split(h @ w_qkv, 3, axis=-1)
        return q.reshape(B, n, H, 2, d), k.reshape(B, n, H, 2, d), v.reshape(B, n, H, 2 * d)

    ql, kl, vl = proj(h_lat)
    qc, kc, vc = proj(h_ctx)
    cos, sin = axial_rope(S, d)
    ql, kl = apply_rope(ql, cos, sin), apply_rope(kl, cos, sin)
    lam = lam.astype(f32)
    lam_full = jnp.exp(jnp.sum(lam[0] * lam[1])) - jnp.exp(jnp.sum(lam[2] * lam[3])) + lambda_init

    def attend(q, k, v):
        s = jnp.einsum('bqhmd,bkhmd->bhmqk', q, k).astype(f32) * (d ** -0.5)
        p = jax.nn.softmax(s, axis=-1)
        p = p[:, :, 0] - lam_full * p[:, :, 1]
        return jnp.einsum('bhqk,bkhe->bqhe', p.astype(v.dtype), v)

    def post(o):
        n = o.shape[1]
        of = o.astype(f32)
        of = of * lax.rsqrt(jnp.mean(of * of, axis=-1, keepdims=True) + 1e-5) * norm_g * (1.0 - lambda_init)
        return of.astype(o.dtype).reshape(B, n, D_MODEL) @ w_out

    k_all = jnp.concatenate([kl, kc], axis=1)
    v_all = jnp.concatenate([vl, vc], axis=1)
    nb = S // DA_QBLOCK
    qb = jnp.transpose(ql.reshape(B, nb, DA_QBLOCK, H, 2, d), (1, 0, 2, 3, 4, 5))
    ol = lax.map(lambda q: attend(q, k_all, v_all), qb)
    ol = jnp.transpose(ol, (1, 0, 2, 3, 4)).reshape(B, S, H, 2 * d)
    out_lat = post(ol)
    out_ctx = post(attend(qc, kc, vc)) if with_ctx else None
    return out_lat, out_ctx


def setup_inputs(seed: int = 0) -> dict:
    key = jax.random.key(seed)
    keys = iter(jax.random.split(key, 48))
    f32 = jnp.float32
    D, F = D_MODEL, FFN_HIDDEN
    s = D ** -0.5

    def nrm(shape, scale):
        return jax.random.normal(next(keys), shape, f32) * scale

    def unif(shape, lo, hi):
        return jax.random.uniform(next(keys), shape, f32, lo, hi)

    ml_b_main = nrm((N_B, ML_IN - 4 * ML_HEADS), 0.02)
    ml_b_i = nrm((N_B, 2, ML_HEADS), 0.1)
    ml_b_f = unif((N_B, 2, ML_HEADS), 3.0, 6.0)
    ml_b_gate = jnp.stack([ml_b_i, ml_b_f], axis=2).reshape(N_B, 4 * ML_HEADS)

    return {
        "x": nrm((BATCH, SEQ, D), 1.0),
        "c": nrm((BATCH, D), 1.0),
        "ctx": nrm((BATCH, CTX_LEN, D), 1.0),
        "c_ctx": nrm((D,), 1.0),
        "ada_w": nrm((DEPTH, D, 6 * D), 0.5 * s),
        "ada_b": nrm((DEPTH, 6 * D), 0.02),
        "norm1_g": 1.0 + nrm((DEPTH, D), 0.02),
        "norm2_g": 1.0 + nrm((DEPTH, D), 0.02),
        "ffn_w_in": nrm((DEPTH, D, 2 * F), s),
        "ffn_conv_w": nrm((DEPTH, CONV_W, F), CONV_W ** -0.5),
        "ffn_conv_b": nrm((DEPTH, F), 0.02),
        "ffn_w_out": nrm((DEPTH, F, D), F ** -0.5),
        "ra_mix": unif((N_A, 6, D), 0.0, 1.0),
        "ra_w_rkv": nrm((N_A, 3, D, D), s),
        "ra_w0": unif((N_A, 2, D), -6.0, -1.0),
        "ra_w1": nrm((N_A, 2, D, RW_DECAY_LORA), 0.1 * s),
        "ra_w2": nrm((N_A, 2, RW_DECAY_LORA, D), 0.1 * RW_DECAY_LORA ** -0.5),
        "ra_a0": nrm((N_A, 2, D), 0.1),
        "ra_a1": nrm((N_A, 2, D, RW_AAA_LORA), 0.1 * s),
        "ra_a2": nrm((N_A, 2, RW_AAA_LORA, D), RW_AAA_LORA ** -0.5),
        "ra_g1": nrm((N_A, D, RW_GATE_LORA), s),
        "ra_g2": nrm((N_A, RW_GATE_LORA, D), RW_GATE_LORA ** -0.5),
        "ra_k_k": 0.85 + nrm((N_A, D), 0.02),
        "ra_k_a": 1.0 + nrm((N_A, D), 0.02),
        "ra_r_k": nrm((N_A, RW_HEADS, RW_HEAD), 0.1),
        "ra_lnx_g": 1.0 + nrm((N_A, D), 0.02),
        "ra_lnx_b": nrm((N_A, D), 0.02),
        "ra_w_out": nrm((N_A, D, D), s),
        "ml_w_in": nrm((N_B, D, ML_IN), s),
        "ml_b_in": jnp.concatenate([ml_b_main, ml_b_gate], axis=-1),
        "ml_norm_g": 1.0 + nrm((N_B, D), 0.02),
        "ml_w_out": nrm((N_B, D, D), s),
        "da_w_qkv": nrm((N_C, D, 3 * D), s),
        "da_lambda": nrm((N_C, 4, DA_HEAD), 0.1),
        "da_norm_g": 1.0 + nrm((N_C, 2 * DA_HEAD), 0.02),
        "da_w_out": nrm((N_C, D, D), s),
        "final_g": 1.0 + nrm((D,), 0.02),
    }


def reference(x, c, ctx, c_ctx, ada_w, ada_b, norm1_g, norm2_g, ffn_w_in, ffn_conv_w,
              ffn_conv_b, ffn_w_out, ra_mix, ra_w_rkv, ra_w0, ra_w1, ra_w2, ra_a0, ra_a1,
              ra_a2, ra_g1, ra_g2, ra_k_k, ra_k_a, ra_r_k, ra_lnx_g, ra_lnx_b, ra_w_out,
              ml_w_in, ml_b_in, ml_norm_g, ml_w_out, da_w_qkv, da_lambda, da_norm_g,
              da_w_out, final_g):
    for i in range(DEPTH):
        last = i == DEPTH - 1
        kind, j = i % N_MIXERS, i // N_MIXERS
        sh1, sc1, g1, sh2, sc2, g2 = [t[:, None, :] for t in
                                      jnp.split(jax.nn.silu(c) @ ada_w[i] + ada_b[i], 6, axis=-1)]
        csh1, csc1, cg1, csh2, csc2, cg2 = jnp.split(jax.nn.silu(c_ctx) @ ada_w[i] + ada_b[i], 6, axis=-1)
        h_lat = rms_norm(x, norm1_g[i]) * (1.0 + sc1) + sh1
        h_ctx = rms_norm(ctx, norm1_g[i]) * (1.0 + csc1) + csh1
        if kind == 0:
            o_lat, o_ctx = rwkv7_mixer(h_lat, h_ctx, ra_mix[j], ra_w_rkv[j], ra_w0[j], ra_w1[j],
                                       ra_w2[j], ra_a0[j], ra_a1[j], ra_a2[j], ra_g1[j], ra_g2[j],
                                       ra_k_k[j], ra_k_a[j], ra_r_k[j], ra_lnx_g[j], ra_lnx_b[j],
                                       ra_w_out[j], not last)
        elif kind == 1:
            o_lat, o_ctx = mlstm_mixer(h_lat, h_ctx, ml_w_in[j], ml_b_in[j], ml_norm_g[j],
                                       ml_w_out[j], not last)
        else:
            lambda_init = 0.8 - 0.6 * math.exp(-0.3 * i)
            o_lat, o_ctx = diff_attention_mixer(h_lat, h_ctx, da_w_qkv[j], da_lambda[j], da_norm_g[j],
                                                da_w_out[j], lambda_init, not last)
        x = x + g1 * o_lat
        x = x + g2 * conv_ffn(rms_norm(x, norm2_g[i]) * (1.0 + sc2) + sh2,
                              ffn_w_in[i], ffn_conv_w[i], ffn_conv_b[i], ffn_w_out[i])
        if not last:
            ctx = ctx + cg1 * o_ctx
            ctx = ctx + cg2 * conv_ffn(rms_norm(ctx, norm2_g[i]) * (1.0 + csc2) + csh2,
                                       ffn_w_in[i], ffn_conv_w[i], ffn_conv_b[i], ffn_w_out[i])
    return rms_norm(x, final_g)
```

```python
import functools
import math

import jax
import jax.numpy as jnp
from jax import lax
from jax.experimental import pallas as pl
from jax.experimental.pallas import tpu as pltpu

F32 = jnp.float32
BF16 = jnp.bfloat16

DEPTH = 4
N_MIXERS = 3
GRID_W = 64
NORM_EPS = 1e-6
RW_HEAD = 64
RW_LNX_EPS = 6.4e-4
ML_HEADS = 4
ML_CHUNK = 128
DA_HEADS = 8
DA_HEAD = 64
DA_EPS = 1e-5
ROPE_BASE = 10000.0

LANES = 128
SUBLANES = 8
VMEM_LIMIT_BYTES = 56 * 1024 * 1024

ROW_TILE = 256
RW_CHUNK = 64
RW_GROUP = 4
ATT_KCHUNK = 256
FFN_COLS = 256


def _params(sem, vmem=None):
    return pltpu.CompilerParams(dimension_semantics=sem, vmem_limit_bytes=vmem)


def _mm(a, b):
    return jnp.dot(a.astype(BF16), b.astype(BF16), preferred_element_type=F32)


def _mm_nt(a, b):
    return lax.dot_general(a.astype(BF16), b.astype(BF16), (((1,), (1,)), ((), ())),
                           preferred_element_type=F32)


def _mm_tn(a, b):
    return lax.dot_general(a.astype(BF16), b.astype(BF16), (((0,), (0,)), ((), ())),
                           preferred_element_type=F32)


def _split3(x):
    hi = x.astype(BF16)
    r1 = x - hi.astype(F32)
    mid = r1.astype(BF16)
    lo = (r1 - mid.astype(F32)).astype(BF16)
    return hi, mid, lo


def _exact_left(m_bf16, x):
    return sum(jnp.dot(m_bf16, p, preferred_element_type=F32) for p in _split3(x))


def _exact_right(x, m_bf16):
    return sum(jnp.dot(p, m_bf16, preferred_element_type=F32) for p in _split3(x))


def _normmod(x, g, sc, sh):
    y = x * lax.rsqrt(jnp.mean(x * x, axis=-1, keepdims=True) + NORM_EPS)
    return y * g * (1.0 + sc) + sh


def _ada_body(c_ref, w_ref, b_ref, o_ref):
    a = c_ref[...]
    a = a * jax.nn.sigmoid(a)
    o_ref[0] = jnp.dot(a, w_ref[0], precision=lax.Precision.HIGHEST,
                       preferred_element_type=F32) + b_ref[0]


def _ada(cc, ada_w, ada_b):
    depth, d, n = ada_w.shape
    nb = n // 4
    return pl.pallas_call(
        _ada_body,
        grid=(depth, n // nb),
        in_specs=[pl.BlockSpec((SUBLANES, d), lambda l, j: (0, 0)),
                  pl.BlockSpec((1, d, nb), lambda l, j: (l, 0, j)),
                  pl.BlockSpec((1, 1, nb), lambda l, j: (l, 0, j))],
        out_specs=pl.BlockSpec((1, SUBLANES, nb), lambda l, j: (l, 0, j)),
        out_shape=jax.ShapeDtypeStruct((depth, SUBLANES, n), F32),
        compiler_params=_params(("arbitrary", "arbitrary"), VMEM_LIMIT_BYTES),
        name="ada_mod",
    )(cc, ada_w, ada_b.reshape(depth, 1, n))


def _row_spec(d, tm=ROW_TILE):
    return pl.BlockSpec((1, tm, d), lambda b, t: (b, t, 0))


def _mod_spec(d, nct):
    return pl.BlockSpec((1, 1, 6, d), lambda b, t: (b, jnp.where(t >= nct, 1, 0), 0, 0))


def _full_spec(shape):
    nd = len(shape)
    return pl.BlockSpec(shape, lambda b, t: (0,) * nd)


def _halo_specs(d, tm, n_rows):
    per = tm // SUBLANES
    last = n_rows // SUBLANES - 1
    prev = pl.BlockSpec((1, SUBLANES, d), lambda b, t: (b, jnp.maximum(t * per - 1, 0), 0))
    nxt = pl.BlockSpec((1, SUBLANES, d), lambda b, t: (b, jnp.minimum((t + 1) * per, last), 0))
    return prev, nxt


def _normmod_body(x_ref, g_ref, m_ref, o_ref):
    m = m_ref[0, 0]
    o_ref[0] = _normmod(x_ref[0], g_ref[...], m[1:2], m[0:1])


def _normmod_call(xs, g, modsel, nct):
    b, t, d = xs.shape
    return pl.pallas_call(
        _normmod_body,
        grid=(b, t // ROW_TILE),
        in_specs=[_row_spec(d), _full_spec((1, d)), _mod_spec(d, nct)],
        out_specs=_row_spec(d),
        out_shape=jax.ShapeDtypeStruct((b, t, d), F32),
        compiler_params=_params(("parallel", "parallel")),
        name="normmod",
    )(xs, g.reshape(1, d), modsel)


def _final_norm_body(x_ref, g_ref, o_ref):
    x = x_ref[0]
    o_ref[0] = x * lax.rsqrt(jnp.mean(x * x, axis=-1, keepdims=True) + NORM_EPS) * g_ref[...]


def _final_norm(xs, g, nct):
    b, t, d = xs.shape
    nt = t // ROW_TILE - nct
    return pl.pallas_call(
        _final_norm_body,
        grid=(b, nt),
        in_specs=[pl.BlockSpec((1, ROW_TILE, d), lambda bb, tt: (bb, tt + nct, 0)),
                  _full_spec((1, d))],
        out_specs=_row_spec(d),
        out_shape=jax.ShapeDtypeStruct((b, nt * ROW_TILE, d), F32),
        compiler_params=_params(("parallel", "parallel")),
        name="final_norm",
    )(xs, g.reshape(1, d))


def _ffn_body(x_ref, xp_ref, xn_ref, g_ref, m_ref, win_ref, cw_ref, cb_ref, wout_ref, o_ref, act_ref,
              *, nct, nt, hidden):
    t = pl.program_id(1)
    x = x_ref[0]
    tm = x.shape[0]
    m = m_ref[0, 0]
    xa = jnp.concatenate([xp_ref[0], x, xn_ref[0]], axis=0)
    h32 = _normmod(xa, g_ref[...], m[4:5], m[3:4])
    ha = h32.astype(BF16)
    hm = h32[SUBLANES:SUBLANES + tm].astype(BF16)
    first = jnp.logical_or(t == 0, t == nct)
    last = jnp.logical_or(t == nct - 1, t == nt - 1)
    rows = lax.broadcasted_iota(jnp.int32, (tm + 2 * SUBLANES, 1), 0)
    dead = jnp.logical_or(rows == jnp.where(first, SUBLANES - 1, -1),
                          rows == jnp.where(last, tm + SUBLANES, -1))
    keep = jnp.where(dead, 0.0, 1.0)
    for j in range(hidden // FFN_COLS):
        lo, hi = j * FFN_COLS, (j + 1) * FFN_COLS
        val = jnp.dot(hm, win_ref[:, lo:hi], preferred_element_type=F32)
        gate = jnp.dot(ha, win_ref[:, hidden + lo:hidden + hi], preferred_element_type=F32) * keep
        g_prev = pltpu.roll(gate, 1, 0)[SUBLANES:SUBLANES + tm]
        g_next = pltpu.roll(gate, tm + 2 * SUBLANES - 1, 0)[SUBLANES:SUBLANES + tm]
        g_mid = gate[SUBLANES:SUBLANES + tm]
        cw = cw_ref[:, lo:hi]
        conv = cw[0:1] * g_prev + cw[1:2] * g_mid + cw[2:3] * g_next + cb_ref[:, lo:hi]
        act_ref[:, lo:hi] = (jax.nn.gelu(conv) * val).astype(BF16)
    out = jnp.dot(act_ref[...], wout_ref[...], preferred_element_type=F32)
    o_ref[0] = x + m[5:6] * out


def _ffn_call(xs, g, modsel, w_in, conv_w, conv_b, w_out, nct):
    b, t, d = xs.shape
    hidden = w_out.shape[0]
    nt = t // ROW_TILE
    prev, nxt = _halo_specs(d, ROW_TILE, t)
    body = functools.partial(_ffn_body, nct=nct, nt=nt, hidden=hidden)
    return pl.pallas_call(
        body,
        grid=(b, nt),
        in_specs=[_row_spec(d), prev, nxt, _full_spec((1, d)), _mod_spec(d, nct),
                  _full_spec((d, 2 * hidden)), _full_spec((3, hidden)), _full_spec((1, hidden)),
                  _full_spec((hidden, d))],
        out_specs=_row_spec(d),
        out_shape=jax.ShapeDtypeStruct((b, t, d), F32),
        scratch_shapes=[pltpu.VMEM((ROW_TILE, hidden), BF16)],
        compiler_params=_params(("parallel", "parallel"), VMEM_LIMIT_BYTES),
        name="conv_ffn",
    )(xs, xs, xs, g.reshape(1, d), modsel, w_in.astype(BF16), conv_w, conv_b.reshape(1, hidden),
      w_out.astype(BF16))


def _seg_sum(x, ones_bd):
    return _exact_right(x, ones_bd)


def _rwkv_prep_body(h_ref, hp_ref, hn_ref, mix_ref, wrkv_ref, w1_ref, a1_ref, g1_ref, w2_ref, a2_ref,
                    g2_ref, vec_ref, ones_ref,
                    r_ref, v_ref, kk_ref, g_ref, bonus_ref, lw_ref, kd_ref, ka_ref, *, nct, nt):
    t = pl.program_id(1)
    h = h_ref[0]
    tm = h.shape[0]
    first = jnp.logical_or(t == 0, t == nct)
    last = jnp.logical_or(t == nct - 1, t == nt - 1)
    p_row = hp_ref[0][SUBLANES - 1:SUBLANES] * jnp.where(first, 0.0, 1.0)
    n_row = hn_ref[0][0:1] * jnp.where(last, 0.0, 1.0)
    rows = lax.broadcasted_iota(jnp.int32, (tm, 1), 0)
    h_prev = jnp.where(rows == 0, p_row, pltpu.roll(h, 1, 0))
    h_next = jnp.where(rows == tm - 1, n_row, pltpu.roll(h, tm - 1, 0))
    xx = 0.5 * (h_prev + h_next) - h
    mix = mix_ref[...]
    vec = vec_ref[...]
    ones_bd = ones_ref[...]

    def shifted(j):
        return h + xx * mix[j:j + 1]

    r = _mm(shifted(0), wrkv_ref[0])
    k = _mm(shifted(1), wrkv_ref[1])
    v = _mm(shifted(2), wrkv_ref[2])
    w_in = jnp.tanh(_mm(shifted(3), w1_ref[...]))
    a_in = _mm(shifted(4), a1_ref[...])
    g = _mm(jax.nn.sigmoid(_mm(shifted(5), g1_ref[...])), g2_ref[...])

    kk = k * vec[4:5]
    kk = kk / jnp.maximum(jnp.sqrt(_seg_sum(kk * kk, ones_bd)), 1e-12)
    kd_sum = None
    for d in range(2):
        w_pre = vec[d:d + 1] + _mm(w_in, w2_ref[d])
        lw_ref[d, 0] = -jax.nn.sigmoid(w_pre) * math.exp(-0.5)
        a = jax.nn.sigmoid(vec[2 + d:3 + d] + _mm(a_in, a2_ref[d]))
        kd = k * (1.0 + (a - 1.0) * vec[5:6])
        kd_ref[d, 0] = kd
        ka_ref[d, 0] = kk * a
        kd_sum = kd if kd_sum is None else kd_sum + kd
    r_ref[0] = r
    v_ref[0] = v
    kk_ref[0] = kk
    g_ref[0] = g
    bonus_ref[0] = _seg_sum(r * kd_sum * vec[6:7], ones_bd) * v


def _rwkv_prep(h, mix, w_rkv, w0, w1, w2, a0, a1, a2, g1, g2, k_k, k_a, r_k, ones_bd, nct):
    b, t, d = h.shape
    nt = t // ROW_TILE
    lora = w1.shape[-1]
    prev, nxt = _halo_specs(d, ROW_TILE, t)
    zeros = jnp.zeros((lora, d), F32)
    w2p = jnp.stack([jnp.concatenate([w2[0], zeros], 0), jnp.concatenate([zeros, w2[1]], 0)]).astype(BF16)
    a2p = jnp.stack([jnp.concatenate([a2[0], zeros], 0), jnp.concatenate([zeros, a2[1]], 0)]).astype(BF16)
    w1c = jnp.concatenate([w1[0], w1[1]], axis=1).astype(BF16)
    a1c = jnp.concatenate([a1[0], a1[1]], axis=1).astype(BF16)
    vec = jnp.stack([w0[0], w0[1], a0[0], a0[1], k_k, k_a, r_k.reshape(d), jnp.zeros((d,), F32)])
    shared = jax.ShapeDtypeStruct((b, t, d), F32)
    directional = jax.ShapeDtypeStruct((2, b, t, d), F32)
    dir_spec = pl.BlockSpec((2, 1, ROW_TILE, d), lambda bb, tt: (0, bb, tt, 0))
    body = functools.partial(_rwkv_prep_body, nct=nct, nt=nt)
    return pl.pallas_call(
        body,
        grid=(b, nt),
        in_specs=[_row_spec(d), prev, nxt, _full_spec((6, d)), _full_spec((3, d, d)),
                  _full_spec(w1c.shape), _full_spec(a1c.shape), _full_spec(g1.shape),
                  _full_spec(w2p.shape), _full_spec(a2p.shape), _full_spec(g2.shape),
                  _full_spec((SUBLANES, d)), _full_spec((d, d))],
        out_specs=[_row_spec(d)] * 5 + [dir_spec] * 3,
        out_shape=[shared] * 5 + [directional] * 3,
        compiler_params=_params(("parallel", "parallel"), VMEM_LIMIT_BYTES),
        name="rwkv_prep",
    )(h, h, h, mix, w_rkv.astype(BF16), w1c, a1c, g1.astype(BF16), w2p, a2p, g2.astype(BF16), vec,
      ones_bd)


def _rwkv_scan_body(r_ref, v_ref, kk_ref, lw_ref, kd_ref, ka_ref, y_ref, s_ref, *, chunk, group):
    d = pl.program_id(1)

    @pl.when(pl.program_id(3) == 0)
    def _():
        s_ref[...] = jnp.zeros_like(s_ref)

    c, gc, width = chunk, group * chunk, group * RW_HEAD
    sgn = 1 - 2 * d
    rr = lax.broadcasted_iota(jnp.int32, (gc, gc), 0)
    cc = lax.broadcasted_iota(jnp.int32, (gc, gc), 1)
    ahead = (rr - cc) * sgn
    strict = ahead > 0
    incl = ahead >= 0
    eye = jnp.where(rr == cc, 1.0, 0.0)
    tr = lax.broadcasted_iota(jnp.int32, (c, c), 0)
    tc = lax.broadcasted_iota(jnp.int32, (c, c), 1)
    tri = jnp.where((tr - tc) * sgn >= 0, 1.0, 0.0).astype(BF16)
    head_of_row = lax.broadcasted_iota(jnp.int32, (gc, width), 0) // c
    head_of_lane = lax.broadcasted_iota(jnp.int32, (gc, width), 1) // RW_HEAD
    own = head_of_row == head_of_lane

    def stack(x):
        return jnp.where(own, jnp.concatenate([x] * group, axis=0), 0.0).astype(BF16)

    lw = lw_ref[0, 0]
    cum = _exact_left(tri, lw)
    tot = jnp.sum(lw, axis=0, keepdims=True)
    e_neg = jnp.exp(-cum)
    e_out = jnp.exp(tot - cum)
    kk, ka, kd = kk_ref[0], ka_ref[0, 0], kd_ref[0, 0]
    alpha = stack(kk * jnp.exp(cum - lw))
    rbar = stack(r_ref[0] * jnp.exp(cum))
    beta_t = stack(ka * e_neg)
    k_t = stack(kd * e_neg)
    beta_h = stack(ka * e_out)
    k_h = stack(kd * e_out)
    vm = stack(v_ref[0])

    a_ab = jnp.where(strict, _mm_nt(alpha, beta_t), 0.0)
    a_ak = jnp.where(strict, _mm_nt(alpha, k_t), 0.0)
    r_ab = jnp.where(incl, _mm_nt(rbar, beta_t), 0.0)
    r_ak = jnp.where(incl, _mm_nt(rbar, k_t), 0.0)

    xp = -a_ab
    inv = eye + xp
    for _ in range(int(math.log2(c)) - 1):
        xp = _mm(xp, xp)
        inv = inv + _mm(inv, xp)

    s = s_ref[...]
    um = -_mm(inv, _mm_nt(alpha, s) + _mm(a_ak, vm))
    ym = _mm_nt(rbar, s) + _mm(r_ab, um) + _mm(r_ak, vm)
    y = ym[0:c]
    for j in range(1, group):
        y = y + ym[j * c:(j + 1) * c]
    y_ref[0, 0] = y
    s_ref[...] = s * jnp.exp(tot) + _mm_tn(um, beta_h) + _mm_tn(vm, k_h)


def _rwkv_scan(r, v, kk, lw, kd, ka, ctx_len):
    b, t, d = r.shape
    c = RW_CHUNK
    width = RW_GROUP * RW_HEAD
    nc, nck = t // c, ctx_len // c

    def chunk_of(dd, i):
        back = jnp.where(i < nck, nck - 1 - i, nc - 1 - (i - nck))
        return jnp.where(dd == 0, i, back)

    shared = pl.BlockSpec((1, c, width), lambda bb, dd, hg, i: (bb, chunk_of(dd, i), hg))
    directional = pl.BlockSpec((1, 1, c, width), lambda bb, dd, hg, i: (dd, bb, chunk_of(dd, i), hg))
    body = functools.partial(_rwkv_scan_body, chunk=c, group=RW_GROUP)
    return pl.pallas_call(
        body,
        grid=(b, 2, d // width, nc),
        in_specs=[shared, shared, shared, directional, directional, directional],
        out_specs=directional,
        out_shape=jax.ShapeDtypeStruct((2, b, t, d), F32),
        scratch_shapes=[pltpu.VMEM((width, width), F32)],
        compiler_params=_params(("parallel", "parallel", "parallel", "arbitrary")),
        name="rwkv_scan",
    )(r, v, kk, lw, kd, ka)


def _rwkv_post_body(y_ref, bonus_ref, g_ref, x_ref, m_ref, lng_ref, lnb_ref, ones_ref, wout_ref, o_ref):
    ones_bd = ones_ref[...]
    y = y_ref[0, 0] + y_ref[1, 0]
    mu = _seg_sum(y, ones_bd) * (1.0 / RW_HEAD)
    yc = y - mu
    var = _seg_sum(yc * yc, ones_bd) * (1.0 / RW_HEAD)
    yn = yc * lax.rsqrt(var + RW_LNX_EPS) * lng_ref[...] + lnb_ref[...]
    out = _mm((yn + bonus_ref[0]) * g_ref[0], wout_ref[...])
    o_ref[0] = x_ref[0] + m_ref[0, 0][2:3] * out


def _rwkv_post(y, bonus, g, xs, modsel, lnx_g, lnx_b, ones_bd, w_out, nct):
    b, t, d = xs.shape
    dir_spec = pl.BlockSpec((2, 1, ROW_TILE, d), lambda bb, tt: (0, bb, tt, 0))
    return pl.pallas_call(
        _rwkv_post_body,
        grid=(b, t // ROW_TILE),
        in_specs=[dir_spec, _row_spec(d), _row_spec(d), _row_spec(d), _mod_spec(d, nct),
                  _full_spec((1, d)), _full_spec((1, d)), _full_spec((d, d)), _full_spec((d, d))],
        out_specs=_row_spec(d),
        out_shape=jax.ShapeDtypeStruct((b, t, d), F32),
        compiler_params=_params(("parallel", "parallel"), VMEM_LIMIT_BYTES),
        name="rwkv_post",
    )(y, bonus, g, xs, modsel, lnx_g.reshape(1, d), lnx_b.reshape(1, d), ones_bd, w_out.astype(BF16))


def _rwkv_layer(xs, n1g, modsel, p, ones_bd, nct, ctx_len):
    h = _normmod_call(xs, n1g, modsel, nct)
    r, v, kk, g, bonus, lw, kd, ka = _rwkv_prep(
        h, p["mix"], p["w_rkv"], p["w0"], p["w1"], p["w2"], p["a0"], p["a1"], p["a2"], p["g1"], p["g2"],
        p["k_k"], p["k_a"], p["r_k"], ones_bd, nct)
    y = _rwkv_scan(r, v, kk, lw, kd, ka, ctx_len)
    return _rwkv_post(y, bonus, g, xs, modsel, p["lnx_g"], p["lnx_b"], ones_bd, p["w_out"], nct)


def _mlstm_in_body(x_ref, g_ref, m_ref, w_ref, b_ref, q_ref, k_ref, v_ref, o_ref, gf_ref, gb_ref,
                   *, heads, dk, dv):
    m = m_ref[0, 0]
    h = _normmod(x_ref[0], g_ref[...], m[1:2], m[0:1])
    z = _mm(h, w_ref[...]) + b_ref[...]
    nqk = heads * dk
    nv = heads * dv
    q_ref[0] = z[:, 0:nqk]
    k_ref[0] = z[:, nqk:2 * nqk] * (dk ** -0.5)
    v_ref[0] = z[:, 2 * nqk:2 * nqk + nv]
    d_model = o_ref.shape[-1]
    o_ref[0] = jax.nn.sigmoid(z[:, 2 * nqk + nv:2 * nqk + nv + d_model])
    gates = z[:, 2 * nqk + nv + d_model:]
    lane = lax.broadcasted_iota(jnp.int32, (1, LANES), 1)
    is_forget = jnp.logical_and(lane >= heads, lane < 2 * heads)
    for ref, off in ((gf_ref, 0), (gb_ref, LANES)):
        gd = gates[:, off:off + LANES]
        ref[0] = jnp.where(is_forget, jax.nn.log_sigmoid(gd), gd)


def _mlstm_in(xs, g, modsel, w_in, b_in, nct):
    b, t, d = xs.shape
    heads, dk, dv = ML_HEADS, d // (2 * ML_HEADS), d // ML_HEADS
    main = 2 * heads * dk + heads * dv + d
    pad = jnp.zeros((d, LANES - 2 * heads), F32)
    w_cat = jnp.concatenate([w_in[:, :main], w_in[:, main:main + 2 * heads], pad,
                             w_in[:, main + 2 * heads:], pad], axis=1)
    padb = jnp.zeros((LANES - 2 * heads,), F32)
    b_cat = jnp.concatenate([b_in[:main], b_in[main:main + 2 * heads], padb, b_in[main + 2 * heads:], padb])
    n = w_cat.shape[1]
    body = functools.partial(_mlstm_in_body, heads=heads, dk=dk, dv=dv)
    shapes = [(heads * dk), (heads * dk), (heads * dv), d, LANES, LANES]
    return pl.pallas_call(
        body,
        grid=(b, t // ROW_TILE),
        in_specs=[_row_spec(d), _full_spec((1, d)), _mod_spec(d, nct), _full_spec((d, n)), _full_spec((1, n))],
        out_specs=[_row_spec(s) for s in shapes],
        out_shape=[jax.ShapeDtypeStruct((b, t, s), F32) for s in shapes],
        compiler_params=_params(("parallel", "parallel"), VMEM_LIMIT_BYTES),
        name="mlstm_in",
    )(xs, g.reshape(1, d), modsel, w_cat.astype(BF16), b_cat.reshape(1, n))


def _mlstm_chunk_body(q_ref, k_ref, v_ref, gc_ref, gr_ref, h_ref, c_ref, n_ref, m_ref, *, heads, dk, dv):
    d = pl.program_id(1)

    @pl.when(pl.program_id(2) == 0)
    def _():
        c_ref[...] = jnp.zeros_like(c_ref)
        n_ref[...] = jnp.zeros_like(n_ref)
        m_ref[...] = jnp.zeros_like(m_ref)

    t = q_ref.shape[1]
    sgn = 1 - 2 * d
    rr = lax.broadcasted_iota(jnp.int32, (t, t), 0)
    cc = lax.broadcasted_iota(jnp.int32, (t, t), 1)
    seen = (rr - cc) * sgn >= 0
    tri = jnp.where(seen, 1.0, 0.0).astype(BF16)
    gcol = gc_ref[0, 0]
    grow = gr_ref[0, 0]
    bcol = _exact_left(tri, gcol)
    brow = sum(_mm_nt(p, tri) for p in _split3(grow))
    for h in range(heads):
        f = heads + h
        li_col, li_row = gcol[:, h:h + 1], grow[h:h + 1, :]
        b_col, b_row = bcol[:, f:f + 1], brow[f:f + 1, :]
        b_end = jnp.sum(gcol[:, f:f + 1], axis=0, keepdims=True)
        q = q_ref[0, :, h * dk:(h + 1) * dk]
        k = k_ref[0, :, h * dk:(h + 1) * dk]
        v = v_ref[0, :, h * dv:(h + 1) * dv]
        c0 = c_ref[h]
        n0 = n_ref[h]
        m0 = m_ref[h][:, 0:1]
        g_end = b_end - b_col + li_col
        m_loc = jnp.max(g_end, axis=0, keepdims=True)
        kw = k * jnp.exp(g_end - m_loc)
        c_loc = _mm_tn(kw, v)
        n_loc = jnp.sum(kw, axis=0, keepdims=True)
        d_log = jnp.where(seen, b_col - b_row + li_row, -jnp.inf)
        inter = b_col + m0
        m_t = jnp.maximum(jnp.max(d_log, axis=-1, keepdims=True), inter)
        p = jnp.exp(d_log - m_t) * _mm_nt(q, k)
        e_inter = jnp.exp(inter - m_t)
        num = _mm(p, v) + e_inter * _mm(q, c0)
        den = jnp.sum(p, axis=-1, keepdims=True) + e_inter * jnp.sum(q * n0, axis=-1, keepdims=True)
        h_ref[0, 0, :, h * dv:(h + 1) * dv] = num / jnp.maximum(jnp.abs(den), jnp.exp(-m_t))
        m_new = jnp.maximum(b_end + m0, m_loc)
        a = jnp.exp(b_end + m0 - m_new)
        e = jnp.exp(m_loc - m_new)
        c_ref[h] = a * c0 + e * c_loc
        n_ref[h] = a * n0 + e * n_loc
        m_ref[h] = jnp.broadcast_to(m_new, (1, LANES))


def _mlstm_chunks(q, k, v, gcol, grow, ctx_len):
    b, t, _ = q.shape
    heads = ML_HEADS
    dk, dv = q.shape[-1] // heads, v.shape[-1] // heads
    c = ML_CHUNK
    nc, nck = t // c, ctx_len // c

    def chunk_of(dd, i):
        back = jnp.where(i < nck, nck - 1 - i, nc - 1 - (i - nck))
        return jnp.where(dd == 0, i, back)

    def seq(width):
        return pl.BlockSpec((1, c, width), lambda bb, dd, i: (bb, chunk_of(dd, i), 0))

    body = functools.partial(_mlstm_chunk_body, heads=heads, dk=dk, dv=dv)
    return pl.pallas_call(
        body,
        grid=(b, 2, nc),
        in_specs=[seq(heads * dk), seq(heads * dk), seq(heads * dv),
                  pl.BlockSpec((1, 1, c, LANES), lambda bb, dd, i: (dd, bb, chunk_of(dd, i), 0)),
                  pl.BlockSpec((1, 1, SUBLANES, c), lambda bb, dd, i: (dd, bb, 0, chunk_of(dd, i)))],
        out_specs=pl.BlockSpec((1, 1, c, heads * dv), lambda bb, dd, i: (dd, bb, chunk_of(dd, i), 0)),
        out_shape=jax.ShapeDtypeStruct((2, b, t, heads * dv), F32),
        scratch_shapes=[pltpu.VMEM((heads, dk, dv), F32), pltpu.VMEM((heads, 1, dk), F32),
                        pltpu.VMEM((heads, 1, LANES), F32)],
        compiler_params=_params(("parallel", "parallel", "arbitrary")),
        name="mlstm_chunks",
    )(q, k, v, gcol, grow)


def _mlstm_post_body(h_ref, o_ref, x_ref, m_ref, ng_ref, wout_ref, out_ref, *, heads):
    hs = h_ref[0, 0] + h_ref[1, 0]
    dv = hs.shape[-1] // heads
    parts = []
    for h in range(heads):
        hh = hs[:, h * dv:(h + 1) * dv]
        parts.append(hh * lax.rsqrt(jnp.mean(hh * hh, axis=-1, keepdims=True) + NORM_EPS))
    hn = jnp.concatenate(parts, axis=-1)
    out = _mm(hn * ng_ref[...] * o_ref[0], wout_ref[...])
    out_ref[0] = x_ref[0] + m_ref[0, 0][2:3] * out


def _mlstm_post(hdir, o, xs, modsel, norm_g, w_out, nct):
    b, t, d = xs.shape
    dir_spec = pl.BlockSpec((2, 1, ROW_TILE, d), lambda bb, tt: (0, bb, tt, 0))
    body = functools.partial(_mlstm_post_body, heads=ML_HEADS)
    return pl.pallas_call(
        body,
        grid=(b, t // ROW_TILE),
        in_specs=[dir_spec, _row_spec(d), _row_spec(d), _mod_spec(d, nct), _full_spec((1, d)),
                  _full_spec((d, d))],
        out_specs=_row_spec(d),
        out_shape=jax.ShapeDtypeStruct((b, t, d), F32),
        compiler_params=_params(("parallel", "parallel"), VMEM_LIMIT_BYTES),
        name="mlstm_post",
    )(hdir, o, xs, modsel, norm_g.reshape(1, d), w_out.astype(BF16))


def _mlstm_layer(xs, n1g, modsel, p, nct, ctx_len):
    q, k, v, o, gf, gb = _mlstm_in(xs, n1g, modsel, p["w_in"], p["b_in"], nct)
    gcol = jnp.stack([gf, gb])
    grow = jnp.swapaxes(gcol[..., :SUBLANES], -1, -2)
    hdir = _mlstm_chunks(q, k, v, gcol, grow, ctx_len)
    return _mlstm_post(hdir, o, xs, modsel, p["norm_g"], p["w_out"], nct)


def _attn_in_body(x_ref, g_ref, m_ref, w_ref, cos_ref, sin_ref, q_ref, k_ref, v_ref, *, d_model):
    m = m_ref[0, 0]
    h = _normmod(x_ref[0], g_ref[...], m[1:2], m[0:1])
    z = _mm(h, w_ref[...])
    reps = d_model // LANES
    cos = jnp.concatenate([cos_ref[...]] * reps, axis=1)
    sin = jnp.concatenate([sin_ref[...]] * reps, axis=1)
    lane = lax.broadcasted_iota(jnp.int32, (1, d_model), 1)
    low = (lane % DA_HEAD) < (DA_HEAD // 2)
    half = DA_HEAD // 2

    def rope(u):
        partner = jnp.where(low, pltpu.roll(u, d_model - half, 1), pltpu.roll(u, half, 1))
        return u * cos + partner * sin

    q_ref[0] = (rope(z[:, 0:d_model]) * (DA_HEAD ** -0.5)).astype(BF16)
    k_ref[0] = rope(z[:, d_model:2 * d_model]).astype(BF16)
    v_ref[0] = z[:, 2 * d_model:].astype(BF16)


def _attn_in(xs, g, modsel, w_qkv, cos, sin, nct):
    b, t, d = xs.shape
    body = functools.partial(_attn_in_body, d_model=d)
    tab = pl.BlockSpec((ROW_TILE, LANES), lambda bb, tt: (tt, 0))
    return pl.pallas_call(
        body,
        grid=(b, t // ROW_TILE),
        in_specs=[_row_spec(d), _full_spec((1, d)), _mod_spec(d, nct), _full_spec((d, 3 * d)), tab, tab],
        out_specs=[_row_spec(d)] * 3,
        out_shape=[jax.ShapeDtypeStruct((b, t, d), BF16)] * 3,
        compiler_params=_params(("parallel", "parallel"), VMEM_LIMIT_BYTES),
        name="attn_in",
    )(xs, g.reshape(1, d), modsel, w_qkv.astype(BF16), cos, sin)


def _attn_body(lam_ref, q_ref, k_ref, v_ref, o_ref, *, nct, lambda_init, kchunk):
    qi = pl.program_id(2)
    q = q_ref[0]
    tq, dh2 = q.shape
    lane = lax.broadcasted_iota(jnp.int32, (1, dh2), 1)
    zero = jnp.zeros_like(q)
    qs = jnp.concatenate([jnp.where(lane < DA_HEAD, q, zero), jnp.where(lane >= DA_HEAD, q, zero)], axis=0)
    n_keys = k_ref.shape[1]
    n_steps = jnp.where(qi < nct, (nct * tq) // kchunk, n_keys // kchunk)

    def step(j, carry):
        m_run, l_run, acc = carry
        start = pl.multiple_of(j * kchunk, kchunk)
        kc = k_ref[0, pl.ds(start, kchunk), :]
        vc = v_ref[0, pl.ds(start, kchunk), :]
        s = lax.dot_general(qs, kc, (((1,), (1,)), ((), ())), preferred_element_type=F32)
        m_new = jnp.maximum(m_run, jnp.max(s, axis=-1, keepdims=True))
        scale = jnp.exp(m_run - m_new)
        p = jnp.exp(s - m_new)
        l_new = l_run * scale + jnp.sum(p, axis=-1, keepdims=True)
        acc_new = acc * scale + jnp.dot(p.astype(BF16), vc, preferred_element_type=F32)
        return m_new, l_new, acc_new

    init = (jnp.full((2 * tq, 1), -jnp.inf, F32), jnp.zeros((2 * tq, 1), F32), jnp.zeros((2 * tq, dh2), F32))
    _, l_run, acc = lax.fori_loop(0, n_steps, step, init)
    lam = lam_ref[...]
    lam_full = (jnp.exp(jnp.sum(lam[0:1] * lam[1:2], axis=-1, keepdims=True))
                - jnp.exp(jnp.sum(lam[2:3] * lam[3:4], axis=-1, keepdims=True)) + lambda_init)
    o = acc / l_run
    o_ref[0] = o[0:tq] - lam_full * o[tq:2 * tq]


def _attn(q, k, v, lam, lambda_init, nct):
    b, t, d = q.shape
    dh2 = 2 * DA_HEAD
    body = functools.partial(_attn_body, nct=nct, lambda_init=lambda_init, kchunk=ATT_KCHUNK)
    kv = pl.BlockSpec((1, t, dh2), lambda bb, hh, qq: (bb, 0, hh))
    qo = pl.BlockSpec((1, ROW_TILE, dh2), lambda bb, hh, qq: (bb, qq, hh))
    return pl.pallas_call(
        body,
        grid=(b, d // dh2, t // ROW_TILE),
        in_specs=[pl.BlockSpec(lam.shape, lambda bb, hh, qq: (0, 0)), qo, kv, kv],
        out_specs=qo,
        out_shape=jax.ShapeDtypeStruct((b, t, d), F32),
        compiler_params=_params(("parallel", "parallel", "parallel"), VMEM_LIMIT_BYTES),
        name="diff_attn",
    )(lam, q, k, v)


def _attn_post_body(a_ref, x_ref, m_ref, ng_ref, wout_ref, o_ref, *, heads, scale):
    a = a_ref[0]
    dh2 = a.shape[-1] // heads
    parts = []
    for h in range(heads):
        ah = a[:, h * dh2:(h + 1) * dh2]
        parts.append(ah * lax.rsqrt(jnp.mean(ah * ah, axis=-1, keepdims=True) + DA_EPS))
    an = jnp.concatenate(parts, axis=-1) * ng_ref[...] * scale
    o_ref[0] = x_ref[0] + m_ref[0, 0][2:3] * _mm(an, wout_ref[...])


def _attn_post(a, xs, modsel, norm_g, w_out, lambda_init, nct):
    b, t, d = xs.shape
    body = functools.partial(_attn_post_body, heads=DA_HEADS, scale=1.0 - lambda_init)
    return pl.pallas_call(
        body,
        grid=(b, t // ROW_TILE),
        in_specs=[_row_spec(d), _row_spec(d), _mod_spec(d, nct), _full_spec((1, d)), _full_spec((d, d))],
        out_specs=_row_spec(d),
        out_shape=jax.ShapeDtypeStruct((b, t, d), F32),
        compiler_params=_params(("parallel", "parallel"), VMEM_LIMIT_BYTES),
        name="attn_post",
    )(a, xs, modsel, jnp.tile(norm_g, DA_HEADS).reshape(1, d), w_out.astype(BF16))


def _rope_tables(seq, ctx_len):
    rows = seq // GRID_W
    row = jnp.repeat(jnp.arange(rows), GRID_W).astype(F32)
    col = jnp.tile(jnp.arange(GRID_W), rows).astype(F32)
    nf = DA_HEAD // 4
    inv = jnp.power(ROPE_BASE, -jnp.arange(nf, dtype=F32) / nf)
    ang = jnp.concatenate([row[:, None] * inv, col[:, None] * inv], axis=-1)
    cos, sin = jnp.cos(ang), jnp.sin(ang)
    cos = jnp.concatenate([jnp.ones((ctx_len, DA_HEAD // 2), F32), cos], axis=0)
    sin = jnp.concatenate([jnp.zeros((ctx_len, DA_HEAD // 2), F32), sin], axis=0)
    reps = LANES // DA_HEAD
    return (jnp.tile(jnp.concatenate([cos, cos], axis=-1), (1, reps)),
            jnp.tile(jnp.concatenate([-sin, sin], axis=-1), (1, reps)))


def _attn_layer(xs, n1g, modsel, p, lambda_init, nct, ctx_len):
    cos, sin = _rope_tables(xs.shape[1] - ctx_len, ctx_len)
    q, k, v = _attn_in(xs, n1g, modsel, p["w_qkv"], cos, sin, nct)
    a = _attn(q, k, v, p["lam"].astype(F32), lambda_init, nct)
    return _attn_post(a, xs, modsel, p["norm_g"], p["w_out"], lambda_init, nct)


def kernel(x, c, ctx, c_ctx, ada_w, ada_b, norm1_g, norm2_g, ffn_w_in, ffn_conv_w, ffn_conv_b, ffn_w_out,
           ra_mix, ra_w_rkv, ra_w0, ra_w1, ra_w2, ra_a0, ra_a1, ra_a2, ra_g1, ra_g2, ra_k_k, ra_k_a, ra_r_k,
           ra_lnx_g, ra_lnx_b, ra_w_out, ml_w_in, ml_b_in, ml_norm_g, ml_w_out, da_w_qkv, da_lambda,
           da_norm_g, da_w_out, final_g):
    batch, seq, d = x.shape
    ctx_len = ctx.shape[1]
    assert ctx_len % ROW_TILE == 0 and seq % ROW_TILE == 0 and batch + 1 <= SUBLANES
    nct = ctx_len // ROW_TILE
    depth = ada_w.shape[0]

    xs = jnp.concatenate([ctx, x], axis=1)
    cc = jnp.concatenate([c_ctx[None], c, jnp.zeros((SUBLANES - 1 - batch, d), F32)], axis=0)
    mod = _ada(cc, ada_w, ada_b).reshape(depth, SUBLANES, 6, d)
    modsel = jnp.stack([jnp.broadcast_to(mod[:, 0:1], (depth, batch, 6, d)), mod[:, 1:1 + batch]], axis=2)

    head_id = jnp.arange(d) // RW_HEAD
    ones_bd = (head_id[:, None] == head_id[None, :]).astype(BF16)

    for i in range(depth):
        kind, j = i % N_MIXERS, i // N_MIXERS
        if kind == 0:
            p = dict(mix=ra_mix[j], w_rkv=ra_w_rkv[j], w0=ra_w0[j], w1=ra_w1[j], w2=ra_w2[j], a0=ra_a0[j],
                     a1=ra_a1[j], a2=ra_a2[j], g1=ra_g1[j], g2=ra_g2[j], k_k=ra_k_k[j], k_a=ra_k_a[j],
                     r_k=ra_r_k[j], lnx_g=ra_lnx_g[j], lnx_b=ra_lnx_b[j], w_out=ra_w_out[j])
            xs = _rwkv_layer(xs, norm1_g[i], modsel[i], p, ones_bd, nct, ctx_len)
        elif kind == 1:
            p = dict(w_in=ml_w_in[j], b_in=ml_b_in[j], norm_g=ml_norm_g[j], w_out=ml_w_out[j])
            xs = _mlstm_layer(xs, norm1_g[i], modsel[i], p, nct, ctx_len)
        else:
            lambda_init = 0.8 - 0.6 * math.exp(-0.3 * i)
            p = dict(w_qkv=da_w_qkv[j], lam=da_lambda[j], norm_g=da_norm_g[j], w_out=da_w_out[j])
            xs = _attn_layer(xs, norm1_g[i], modsel[i], p, lambda_init, nct, ctx_len)
        xs = _ffn_call(xs, norm2_g[i], modsel[i], ffn_w_in[i], ffn_conv_w[i], ffn_conv_b[i], ffn_w_out[i], nct)
    return _final_norm(xs, final_g, nct)
```

```python
import functools
import math

import jax
import jax.numpy as jnp
from jax import lax
from jax.experimental import pallas as pl
from jax.experimental.pallas import tpu as pltpu

F32 = jnp.float32
BF16 = jnp.bfloat16

DEPTH = 4
N_MIXERS = 3
GRID_W = 64
NORM_EPS = 1e-6
RW_HEAD = 64
RW_LNX_EPS = 6.4e-4
ML_HEADS = 4
ML_CHUNK = 128
DA_HEADS = 8
DA_HEAD = 64
DA_EPS = 1e-5
ROPE_BASE = 10000.0

LANES = 128
SUBLANES = 8
VMEM_LIMIT_BYTES = 56 * 1024 * 1024

ROW_TILE = 256
RW_CHUNK = 64
RW_GROUP = 4
FFN_COLS = 256
ATT_UNROLL = 8


def _params(sem, vmem=None):
    return pltpu.CompilerParams(dimension_semantics=sem, vmem_limit_bytes=vmem)


def _mm(a, b):
    return jnp.dot(a.astype(BF16), b.astype(BF16), preferred_element_type=F32)


def _mm_nt(a, b):
    return lax.dot_general(a.astype(BF16), b.astype(BF16), (((1,), (1,)), ((), ())),
                           preferred_element_type=F32)


def _mm_tn(a, b):
    return lax.dot_general(a.astype(BF16), b.astype(BF16), (((0,), (0,)), ((), ())),
                           preferred_element_type=F32)


def _split3(x):
    hi = x.astype(BF16)
    r1 = x - hi.astype(F32)
    mid = r1.astype(BF16)
    lo = (r1 - mid.astype(F32)).astype(BF16)
    return hi, mid, lo


def _exact_left(m_bf16, x):
    return sum(jnp.dot(m_bf16, p, preferred_element_type=F32) for p in _split3(x))


def _normmod(x, g, sc, sh):
    y = x * lax.rsqrt(jnp.mean(x * x, axis=-1, keepdims=True) + NORM_EPS)
    return y * g * (1.0 + sc) + sh


def _ada_body(c_ref, w_ref, b_ref, o_ref):
    a = c_ref[...]
    a = a * jax.nn.sigmoid(a)
    o_ref[0] = jnp.dot(a, w_ref[0], precision=lax.Precision.HIGHEST,
                       preferred_element_type=F32) + b_ref[0]


def _ada(cc, ada_w, ada_b):
    depth, d, n = ada_w.shape
    nb = n // 4
    return pl.pallas_call(
        _ada_body,
        grid=(depth, n // nb),
        in_specs=[pl.BlockSpec((SUBLANES, d), lambda l, j: (0, 0)),
                  pl.BlockSpec((1, d, nb), lambda l, j: (l, 0, j)),
                  pl.BlockSpec((1, 1, nb), lambda l, j: (l, 0, j))],
        out_specs=pl.BlockSpec((1, SUBLANES, nb), lambda l, j: (l, 0, j)),
        out_shape=jax.ShapeDtypeStruct((depth, SUBLANES, n), F32),
        compiler_params=_params(("arbitrary", "arbitrary"), VMEM_LIMIT_BYTES),
        name="ada_mod",
    )(cc, ada_w, ada_b.reshape(depth, 1, n))


def _row_spec(d, tm=ROW_TILE):
    return pl.BlockSpec((1, tm, d), lambda b, t: (b, t, 0))


def _mod_spec(d, nct):
    return pl.BlockSpec((1, 1, 6, d), lambda b, t: (b, jnp.where(t >= nct, 1, 0), 0, 0))


def _full_spec(shape):
    nd = len(shape)
    return pl.BlockSpec(shape, lambda b, t: (0,) * nd)


def _halo_specs(d, tm, n_rows):
    per = tm // SUBLANES
    last = n_rows // SUBLANES - 1
    prev = pl.BlockSpec((1, SUBLANES, d), lambda b, t: (b, jnp.maximum(t * per - 1, 0), 0))
    nxt = pl.BlockSpec((1, SUBLANES, d), lambda b, t: (b, jnp.minimum((t + 1) * per, last), 0))
    return prev, nxt


def _final_norm_body(x_ref, g_ref, o_ref):
    x = x_ref[0]
    o_ref[0] = x * lax.rsqrt(jnp.mean(x * x, axis=-1, keepdims=True) + NORM_EPS) * g_ref[...]


def _final_norm(xs, g, nct):
    b, t, d = xs.shape
    nt = t // ROW_TILE - nct
    return pl.pallas_call(
        _final_norm_body,
        grid=(b, nt),
        in_specs=[pl.BlockSpec((1, ROW_TILE, d), lambda bb, tt: (bb, tt + nct, 0)),
                  _full_spec((1, d))],
        out_specs=_row_spec(d),
        out_shape=jax.ShapeDtypeStruct((b, nt * ROW_TILE, d), F32),
        compiler_params=_params(("parallel", "parallel")),
        name="final_norm",
    )(xs, g.reshape(1, d))


def _ffn_body(x_ref, xp_ref, xn_ref, g_ref, m_ref, win_ref, cw_ref, cb_ref, wout_ref, o_ref, act_ref,
              *, nct, nt, hidden):
    t = pl.program_id(1)
    x = x_ref[0]
    tm = x.shape[0]
    m = m_ref[0, 0]
    xa = jnp.concatenate([xp_ref[0], x, xn_ref[0]], axis=0)
    h32 = _normmod(xa, g_ref[...], m[4:5], m[3:4])
    ha = h32.astype(BF16)
    hm = h32[SUBLANES:SUBLANES + tm].astype(BF16)
    first = jnp.logical_or(t == 0, t == nct)
    last = jnp.logical_or(t == nct - 1, t == nt - 1)
    rows = lax.broadcasted_iota(jnp.int32, (tm + 2 * SUBLANES, 1), 0)
    dead = jnp.logical_or(rows == jnp.where(first, SUBLANES - 1, -1),
                          rows == jnp.where(last, tm + SUBLANES, -1))
    keep = jnp.where(dead, 0.0, 1.0)
    for j in range(hidden // FFN_COLS):
        lo, hi = j * FFN_COLS, (j + 1) * FFN_COLS
        val = jnp.dot(hm, win_ref[:, lo:hi], preferred_element_type=F32)
        gate = jnp.dot(ha, win_ref[:, hidden + lo:hidden + hi], preferred_element_type=F32) * keep
        g_prev = pltpu.roll(gate, 1, 0)[SUBLANES:SUBLANES + tm]
        g_next = pltpu.roll(gate, tm + 2 * SUBLANES - 1, 0)[SUBLANES:SUBLANES + tm]
        g_mid = gate[SUBLANES:SUBLANES + tm]
        cw = cw_ref[:, lo:hi]
        conv = cw[0:1] * g_prev + cw[1:2] * g_mid + cw[2:3] * g_next + cb_ref[:, lo:hi]
        act_ref[:, lo:hi] = (jax.nn.gelu(conv) * val).astype(BF16)
    out = jnp.dot(act_ref[...], wout_ref[...], preferred_element_type=F32)
    o_ref[0] = x + m[5:6] * out


def _ffn_call(xs, g, modsel, w_in, conv_w, conv_b, w_out, nct):
    b, t, d = xs.shape
    hidden = w_out.shape[0]
    nt = t // ROW_TILE
    prev, nxt = _halo_specs(d, ROW_TILE, t)
    body = functools.partial(_ffn_body, nct=nct, nt=nt, hidden=hidden)
    return pl.pallas_call(
        body,
        grid=(b, nt),
        in_specs=[_row_spec(d), prev, nxt, _full_spec((1, d)), _mod_spec(d, nct),
                  _full_spec((d, 2 * hidden)), _full_spec((3, hidden)), _full_spec((1, hidden)),
                  _full_spec((hidden, d))],
        out_specs=_row_spec(d),
        out_shape=jax.ShapeDtypeStruct((b, t, d), F32),
        scratch_shapes=[pltpu.VMEM((ROW_TILE, hidden), BF16)],
        compiler_params=_params(("parallel", "parallel"), VMEM_LIMIT_BYTES),
        name="conv_ffn",
    )(xs, xs, xs, g.reshape(1, d), modsel, w_in.astype(BF16), conv_w, conv_b.reshape(1, hidden),
      w_out.astype(BF16))


def _seg_sum(x, ones_bd):
    hi = x.astype(BF16)
    lo = (x - hi.astype(F32)).astype(BF16)
    return (jnp.dot(hi, ones_bd, preferred_element_type=F32) + jnp.dot(lo, ones_bd, preferred_element_type=F32))


def _rwkv_prep_body(x_ref, xp_ref, xn_ref, ng_ref, m_ref, mix_ref, wrkv_ref, w1_ref, a1_ref, g1_ref, w2_ref, a2_ref,
                    g2_ref, vec_ref, ones_ref,
                    r_ref, v_ref, kk_ref, g_ref, bonus_ref, lw_ref, kd_ref, ka_ref, *, nct, nt):
    t = pl.program_id(1)
    m = m_ref[0, 0]
    h = _normmod(x_ref[0], ng_ref[...], m[1:2], m[0:1])
    tm = h.shape[0]
    first = jnp.logical_or(t == 0, t == nct)
    last = jnp.logical_or(t == nct - 1, t == nt - 1)
    p_row = _normmod(xp_ref[0], ng_ref[...], m[1:2], m[0:1])[SUBLANES - 1:SUBLANES] * jnp.where(first, 0.0, 1.0)
    n_row = _normmod(xn_ref[0], ng_ref[...], m[1:2], m[0:1])[0:1] * jnp.where(last, 0.0, 1.0)
    rows = lax.broadcasted_iota(jnp.int32, (tm, 1), 0)
    h_prev = jnp.where(rows == 0, p_row, pltpu.roll(h, 1, 0))
    h_next = jnp.where(rows == tm - 1, n_row, pltpu.roll(h, tm - 1, 0))
    xx = 0.5 * (h_prev + h_next) - h
    mix = mix_ref[...]
    vec = vec_ref[...]
    ones_bd = ones_ref[...]

    def shifted(j):
        return h + xx * mix[j:j + 1]

    r = _mm(shifted(0), wrkv_ref[0])
    k = _mm(shifted(1), wrkv_ref[1])
    v = _mm(shifted(2), wrkv_ref[2])
    w_in = jnp.tanh(_mm(shifted(3), w1_ref[...]))
    a_in = _mm(shifted(4), a1_ref[...])
    g = _mm(jax.nn.sigmoid(_mm(shifted(5), g1_ref[...])), g2_ref[...])

    kk = k * vec[4:5]
    kk = kk / jnp.maximum(jnp.sqrt(_seg_sum(kk * kk, ones_bd)), 1e-12)
    kd_sum = None
    for d in range(2):
        w_pre = vec[d:d + 1] + _mm(w_in, w2_ref[d])
        lw_ref[d, 0] = -jax.nn.sigmoid(w_pre) * math.exp(-0.5)
        a = jax.nn.sigmoid(vec[2 + d:3 + d] + _mm(a_in, a2_ref[d]))
        kd = k * (1.0 + (a - 1.0) * vec[5:6])
        kd_ref[d, 0] = kd
        ka_ref[d, 0] = kk * a
        kd_sum = kd if kd_sum is None else kd_sum + kd
    r_ref[0] = r
    v_ref[0] = v
    kk_ref[0] = kk
    g_ref[0] = g
    bonus_ref[0] = _seg_sum(r * kd_sum * vec[6:7], ones_bd) * v


def _rwkv_prep(xs, ng, modsel, mix, w_rkv, w0, w1, w2, a0, a1, a2, g1, g2, k_k, k_a, r_k, ones_bd, nct):
    b, t, d = xs.shape
    nt = t // ROW_TILE
    lora = w1.shape[-1]
    prev, nxt = _halo_specs(d, ROW_TILE, t)
    zeros = jnp.zeros((lora, d), F32)
    w2p = jnp.stack([jnp.concatenate([w2[0], zeros], 0), jnp.concatenate([zeros, w2[1]], 0)]).astype(BF16)
    a2p = jnp.stack([jnp.concatenate([a2[0], zeros], 0), jnp.concatenate([zeros, a2[1]], 0)]).astype(BF16)
    w1c = jnp.concatenate([w1[0], w1[1]], axis=1).astype(BF16)
    a1c = jnp.concatenate([a1[0], a1[1]], axis=1).astype(BF16)
    vec = jnp.stack([w0[0], w0[1], a0[0], a0[1], k_k, k_a, r_k.reshape(d), jnp.zeros((d,), F32)])
    shared = jax.ShapeDtypeStruct((b, t, d), F32)
    directional = jax.ShapeDtypeStruct((2, b, t, d), F32)
    dir_spec = pl.BlockSpec((2, 1, ROW_TILE, d), lambda bb, tt: (0, bb, tt, 0))
    body = functools.partial(_rwkv_prep_body, nct=nct, nt=nt)
    return pl.pallas_call(
        body,
        grid=(b, nt),
        in_specs=[_row_spec(d), prev, nxt, _full_spec((1, d)), _mod_spec(d, nct), _full_spec((6, d)),
                  _full_spec((3, d, d)), _full_spec(w1c.shape), _full_spec(a1c.shape), _full_spec(g1.shape),
                  _full_spec(w2p.shape), _full_spec(a2p.shape), _full_spec(g2.shape),
                  _full_spec((SUBLANES, d)), _full_spec((d, d))],
        out_specs=[_row_spec(d)] * 5 + [dir_spec] * 3,
        out_shape=[shared] * 5 + [directional] * 3,
        compiler_params=_params(("parallel", "parallel"), VMEM_LIMIT_BYTES),
        name="rwkv_prep",
    )(xs, xs, xs, ng.reshape(1, d), modsel, mix, w_rkv.astype(BF16), w1c, a1c, g1.astype(BF16), w2p, a2p, g2.astype(BF16), vec,
      ones_bd)


def _rwkv_scan_body(r_ref, v_ref, kk_ref, lw_ref, kd_ref, ka_ref, y_ref, s_ref, *, chunk, group):
    d = pl.program_id(1)

    @pl.when(pl.program_id(2) == 0)
    def _():
        s_ref[...] = jnp.zeros_like(s_ref)

    c, gc, width = chunk, group * chunk, group * RW_HEAD
    sgn = 1 - 2 * d
    rr = lax.broadcasted_iota(jnp.int32, (2 * gc, 2 * gc), 0)
    cc = lax.broadcasted_iota(jnp.int32, (2 * gc, 2 * gc), 1)
    ahead = ((rr % gc) - (cc % gc)) * sgn
    visible = ahead > jnp.where(rr < gc, 0, -1)
    er = lax.broadcasted_iota(jnp.int32, (gc, gc), 0)
    ec = lax.broadcasted_iota(jnp.int32, (gc, gc), 1)
    eye = jnp.where(er == ec, 1.0, 0.0)
    tr = lax.broadcasted_iota(jnp.int32, (c, c), 0)
    tc = lax.broadcasted_iota(jnp.int32, (c, c), 1)
    tri = jnp.where((tr - tc) * sgn >= 0, 1.0, 0.0).astype(BF16)
    head_of_row = lax.broadcasted_iota(jnp.int32, (gc, width), 0) // c
    head_of_lane = lax.broadcasted_iota(jnp.int32, (gc, width), 1) // RW_HEAD
    own = head_of_row == head_of_lane

    def stack(x):
        return jnp.where(own, jnp.concatenate([x] * group, axis=0), 0.0).astype(BF16)

    lw_all = lw_ref[0, 0]
    cum_all = _exact_left(tri, lw_all)
    tot_all = jnp.sum(lw_all, axis=0, keepdims=True)
    groups = range(lw_all.shape[1] // width)
    lanes = [slice(hg * width, (hg + 1) * width) for hg in groups]
    left, right, carry_in, vm, decay = [], [], [], [], []
    for ln in lanes:
        lw, cum, tot = lw_all[:, ln], cum_all[:, ln], tot_all[:, ln]
        e_neg = jnp.exp(-cum)
        e_out = jnp.exp(tot - cum)
        kk, ka, kd = kk_ref[0, :, ln], ka_ref[0, 0, :, ln], kd_ref[0, 0, :, ln]
        left.append(jnp.concatenate([stack(kk * jnp.exp(cum - lw)), stack(r_ref[0, :, ln] * jnp.exp(cum))], axis=0))
        right.append(jnp.concatenate([stack(ka * e_neg), stack(kd * e_neg)], axis=0))
        carry_in.append(jnp.concatenate([stack(ka * e_out), stack(kd * e_out)], axis=0))
        vm.append(stack(v_ref[0, :, ln]))
        decay.append(jnp.exp(tot))
    pair = [jnp.where(visible, _mm_nt(left[g], right[g]), 0.0) for g in groups]
    xp = [-pair[g][0:gc, 0:gc] for g in groups]
    inv = [eye + xp[g] for g in groups]
    xp = [_mm(xp[g], xp[g]) for g in groups]
    for _ in range(int(math.log2(c)) - 2):
        both = [_mm(jnp.concatenate([inv[g], xp[g]], axis=0), xp[g]) for g in groups]
        inv = [inv[g] + both[g][0:gc] for g in groups]
        xp = [both[g][gc:2 * gc] for g in groups]
    inv = [inv[g] + _mm(inv[g], xp[g]) for g in groups]

    s = [s_ref[g] for g in groups]
    from_state = [_mm_nt(left[g], s[g]) for g in groups]
    from_v = [_mm(pair[g][:, gc:2 * gc], vm[g]) for g in groups]
    um = [-_mm(inv[g], from_state[g][0:gc] + from_v[g][0:gc]) for g in groups]
    ym = [from_state[g][gc:2 * gc] + from_v[g][gc:2 * gc] + _mm(pair[g][gc:2 * gc, 0:gc], um[g]) for g in groups]
    for g in groups:
        y = ym[g][0:c]
        for j in range(1, group):
            y = y + ym[g][j * c:(j + 1) * c]
        y_ref[0, 0, :, lanes[g]] = y
        s_ref[g] = s[g] * decay[g] + _mm_tn(jnp.concatenate([um[g].astype(BF16), vm[g]], axis=0), carry_in[g])


def _rwkv_scan(r, v, kk, lw, kd, ka, ctx_len):
    b, t, d = r.shape
    c = RW_CHUNK
    width = RW_GROUP * RW_HEAD
    nc, nck = t // c, ctx_len // c

    def chunk_of(dd, i):
        back = jnp.where(i < nck, nck - 1 - i, nc - 1 - (i - nck))
        return jnp.where(dd == 0, i, back)

    shared = pl.BlockSpec((1, c, d), lambda bb, dd, i: (bb, chunk_of(dd, i), 0))
    directional = pl.BlockSpec((1, 1, c, d), lambda bb, dd, i: (dd, bb, chunk_of(dd, i), 0))
    body = functools.partial(_rwkv_scan_body, chunk=c, group=RW_GROUP)
    return pl.pallas_call(
        body,
        grid=(b, 2, nc),
        in_specs=[shared, shared, shared, directional, directional, directional],
        out_specs=directional,
        out_shape=jax.ShapeDtypeStruct((2, b, t, d), F32),
        scratch_shapes=[pltpu.VMEM((d // width, width, width), F32)],
        compiler_params=_params(("parallel", "parallel", "arbitrary"), VMEM_LIMIT_BYTES),
        name="rwkv_scan",
    )(r, v, kk, lw, kd, ka)


def _rwkv_post_body(y_ref, bonus_ref, g_ref, x_ref, m_ref, lng_ref, lnb_ref, ones_ref, wout_ref, o_ref):
    ones_bd = ones_ref[...]
    y = y_ref[0, 0] + y_ref[1, 0]
    mu = _seg_sum(y, ones_bd) * (1.0 / RW_HEAD)
    yc = y - mu
    var = _seg_sum(yc * yc, ones_bd) * (1.0 / RW_HEAD)
    yn = yc * lax.rsqrt(var + RW_LNX_EPS) * lng_ref[...] + lnb_ref[...]
    out = _mm((yn + bonus_ref[0]) * g_ref[0], wout_ref[...])
    o_ref[0] = x_ref[0] + m_ref[0, 0][2:3] * out


def _rwkv_post(y, bonus, g, xs, modsel, lnx_g, lnx_b, ones_bd, w_out, nct):
    b, t, d = xs.shape
    dir_spec = pl.BlockSpec((2, 1, ROW_TILE, d), lambda bb, tt: (0, bb, tt, 0))
    return pl.pallas_call(
        _rwkv_post_body,
        grid=(b, t // ROW_TILE),
        in_specs=[dir_spec, _row_spec(d), _row_spec(d), _row_spec(d), _mod_spec(d, nct),
                  _full_spec((1, d)), _full_spec((1, d)), _full_spec((d, d)), _full_spec((d, d))],
        out_specs=_row_spec(d),
        out_shape=jax.ShapeDtypeStruct((b, t, d), F32),
        compiler_params=_params(("parallel", "parallel"), VMEM_LIMIT_BYTES),
        name="rwkv_post",
    )(y, bonus, g, xs, modsel, lnx_g.reshape(1, d), lnx_b.reshape(1, d), ones_bd, w_out.astype(BF16))


def _rwkv_layer(xs, n1g, modsel, p, ones_bd, nct, ctx_len):
    r, v, kk, g, bonus, lw, kd, ka = _rwkv_prep(
        xs, n1g, modsel, p["mix"], p["w_rkv"], p["w0"], p["w1"], p["w2"], p["a0"], p["a1"], p["a2"], p["g1"], p["g2"],
        p["k_k"], p["k_a"], p["r_k"], ones_bd, nct)
    y = _rwkv_scan(r, v, kk, lw, kd, ka, ctx_len)
    return _rwkv_post(y, bonus, g, xs, modsel, p["lnx_g"], p["lnx_b"], ones_bd, p["w_out"], nct)


def _mlstm_in_body(x_ref, g_ref, m_ref, w_ref, b_ref, q_ref, k_ref, v_ref, o_ref, gf_ref, gb_ref,
                   *, heads, dk, dv):
    m = m_ref[0, 0]
    h = _normmod(x_ref[0], g_ref[...], m[1:2], m[0:1])
    z = _mm(h, w_ref[...]) + b_ref[...]
    nqk = heads * dk
    nv = heads * dv
    q_ref[0] = z[:, 0:nqk]
    k_ref[0] = z[:, nqk:2 * nqk] * (dk ** -0.5)
    v_ref[0] = z[:, 2 * nqk:2 * nqk + nv]
    d_model = o_ref.shape[-1]
    o_ref[0] = jax.nn.sigmoid(z[:, 2 * nqk + nv:2 * nqk + nv + d_model])
    gates = z[:, 2 * nqk + nv + d_model:]
    lane = lax.broadcasted_iota(jnp.int32, (1, LANES), 1)
    is_forget = jnp.logical_and(lane >= heads, lane < 2 * heads)
    for ref, off in ((gf_ref, 0), (gb_ref, LANES)):
        gd = gates[:, off:off + LANES]
        ref[0] = jnp.where(is_forget, jax.nn.log_sigmoid(gd), gd)


def _mlstm_in(xs, g, modsel, w_in, b_in, nct):
    b, t, d = xs.shape
    heads, dk, dv = ML_HEADS, d // (2 * ML_HEADS), d // ML_HEADS
    main = 2 * heads * dk + heads * dv + d
    pad = jnp.zeros((d, LANES - 2 * heads), F32)
    w_cat = jnp.concatenate([w_in[:, :main], w_in[:, main:main + 2 * heads], pad,
                             w_in[:, main + 2 * heads:], pad], axis=1)
    padb = jnp.zeros((LANES - 2 * heads,), F32)
    b_cat = jnp.concatenate([b_in[:main], b_in[main:main + 2 * heads], padb, b_in[main + 2 * heads:], padb])
    n = w_cat.shape[1]
    body = functools.partial(_mlstm_in_body, heads=heads, dk=dk, dv=dv)
    shapes = [(heads * dk), (heads * dk), (heads * dv), d, LANES, LANES]
    return pl.pallas_call(
        body,
        grid=(b, t // ROW_TILE),
        in_specs=[_row_spec(d), _full_spec((1, d)), _mod_spec(d, nct), _full_spec((d, n)), _full_spec((1, n))],
        out_specs=[_row_spec(s) for s in shapes],
        out_shape=[jax.ShapeDtypeStruct((b, t, s), F32) for s in shapes],
        compiler_params=_params(("parallel", "parallel"), VMEM_LIMIT_BYTES),
        name="mlstm_in",
    )(xs, g.reshape(1, d), modsel, w_cat.astype(BF16), b_cat.reshape(1, n))


def _mlstm_chunk_body(q_ref, k_ref, v_ref, gc_ref, gr_ref, h_ref, c_ref, n_ref, m_ref, *, heads, dk, dv):
    d = pl.program_id(1)

    @pl.when(pl.program_id(2) == 0)
    def _():
        c_ref[...] = jnp.zeros_like(c_ref)
        n_ref[...] = jnp.zeros_like(n_ref)
        m_ref[...] = jnp.zeros_like(m_ref)

    t = q_ref.shape[1]
    sgn = 1 - 2 * d
    rr = lax.broadcasted_iota(jnp.int32, (t, t), 0)
    cc = lax.broadcasted_iota(jnp.int32, (t, t), 1)
    seen = (rr - cc) * sgn >= 0
    tri = jnp.where(seen, 1.0, 0.0).astype(BF16)
    gcol = gc_ref[0, 0]
    grow = gr_ref[0, 0]
    bcol = _exact_left(tri, gcol)
    brow = sum(_mm_nt(p, tri) for p in _split3(grow))
    for h in range(heads):
        f = heads + h
        li_col, li_row = gcol[:, h:h + 1], grow[h:h + 1, :]
        b_col, b_row = bcol[:, f:f + 1], brow[f:f + 1, :]
        b_end = jnp.sum(gcol[:, f:f + 1], axis=0, keepdims=True)
        q = q_ref[0, :, h * dk:(h + 1) * dk]
        k = k_ref[0, :, h * dk:(h + 1) * dk]
        v = v_ref[0, :, h * dv:(h + 1) * dv]
        c0 = c_ref[h]
        n0 = n_ref[h]
        m0 = m_ref[h][:, 0:1]
        g_end = b_end - b_col + li_col
        m_loc = jnp.max(g_end, axis=0, keepdims=True)
        kw = k * jnp.exp(g_end - m_loc)
        c_loc = _mm_tn(kw, v)
        n_loc = jnp.sum(kw, axis=0, keepdims=True)
        d_log = jnp.where(seen, b_col - b_row + li_row, -jnp.inf)
        inter = b_col + m0
        m_t = jnp.maximum(jnp.max(d_log, axis=-1, keepdims=True), inter)
        p = jnp.exp(d_log - m_t) * _mm_nt(q, k)
        e_inter = jnp.exp(inter - m_t)
        num = _mm(p, v) + e_inter * _mm(q, c0)
        den = jnp.sum(p, axis=-1, keepdims=True) + e_inter * jnp.sum(q * n0, axis=-1, keepdims=True)
        h_ref[0, 0, :, h * dv:(h + 1) * dv] = num / jnp.maximum(jnp.abs(den), jnp.exp(-m_t))
        m_new = jnp.maximum(b_end + m0, m_loc)
        a = jnp.exp(b_end + m0 - m_new)
        e = jnp.exp(m_loc - m_new)
        c_ref[h] = a * c0 + e * c_loc
        n_ref[h] = a * n0 + e * n_loc
        m_ref[h] = jnp.broadcast_to(m_new, (1, LANES))


def _mlstm_chunks(q, k, v, gcol, grow, ctx_len):
    b, t, _ = q.shape
    heads = ML_HEADS
    dk, dv = q.shape[-1] // heads, v.shape[-1] // heads
    c = ML_CHUNK
    nc, nck = t // c, ctx_len // c

    def chunk_of(dd, i):
        back = jnp.where(i < nck, nck - 1 - i, nc - 1 - (i - nck))
        return jnp.where(dd == 0, i, back)

    def seq(width):
        return pl.BlockSpec((1, c, width), lambda bb, dd, i: (bb, chunk_of(dd, i), 0))

    body = functools.partial(_mlstm_chunk_body, heads=heads, dk=dk, dv=dv)
    return pl.pallas_call(
        body,
        grid=(b, 2, nc),
        in_specs=[seq(heads * dk), seq(heads * dk), seq(heads * dv),
                  pl.BlockSpec((1, 1, c, LANES), lambda bb, dd, i: (dd, bb, chunk_of(dd, i), 0)),
                  pl.BlockSpec((1, 1, SUBLANES, c), lambda bb, dd, i: (dd, bb, 0, chunk_of(dd, i)))],
        out_specs=pl.BlockSpec((1, 1, c, heads * dv), lambda bb, dd, i: (dd, bb, chunk_of(dd, i), 0)),
        out_shape=jax.ShapeDtypeStruct((2, b, t, heads * dv), F32),
        scratch_shapes=[pltpu.VMEM((heads, dk, dv), F32), pltpu.VMEM((heads, 1, dk), F32),
                        pltpu.VMEM((heads, 1, LANES), F32)],
        compiler_params=_params(("parallel", "parallel", "arbitrary")),
        name="mlstm_chunks",
    )(q, k, v, gcol, grow)


def _mlstm_post_body(h_ref, o_ref, x_ref, m_ref, ng_ref, wout_ref, out_ref, *, heads):
    hs = h_ref[0, 0] + h_ref[1, 0]
    dv = hs.shape[-1] // heads
    parts = []
    for h in range(heads):
        hh = hs[:, h * dv:(h + 1) * dv]
        parts.append(hh * lax.rsqrt(jnp.mean(hh * hh, axis=-1, keepdims=True) + NORM_EPS))
    hn = jnp.concatenate(parts, axis=-1)
    out = _mm(hn * ng_ref[...] * o_ref[0], wout_ref[...])
    out_ref[0] = x_ref[0] + m_ref[0, 0][2:3] * out


def _mlstm_post(hdir, o, xs, modsel, norm_g, w_out, nct):
    b, t, d = xs.shape
    dir_spec = pl.BlockSpec((2, 1, ROW_TILE, d), lambda bb, tt: (0, bb, tt, 0))
    body = functools.partial(_mlstm_post_body, heads=ML_HEADS)
    return pl.pallas_call(
        body,
        grid=(b, t // ROW_TILE),
        in_specs=[dir_spec, _row_spec(d), _row_spec(d), _mod_spec(d, nct), _full_spec((1, d)),
                  _full_spec((d, d))],
        out_specs=_row_spec(d),
        out_shape=jax.ShapeDtypeStruct((b, t, d), F32),
        compiler_params=_params(("parallel", "parallel"), VMEM_LIMIT_BYTES),
        name="mlstm_post",
    )(hdir, o, xs, modsel, norm_g.reshape(1, d), w_out.astype(BF16))


def _mlstm_layer(xs, n1g, modsel, p, nct, ctx_len):
    q, k, v, o, gf, gb = _mlstm_in(xs, n1g, modsel, p["w_in"], p["b_in"], nct)
    gcol = jnp.stack([gf, gb])
    grow = jnp.swapaxes(gcol[..., :SUBLANES], -1, -2)
    hdir = _mlstm_chunks(q, k, v, gcol, grow, ctx_len)
    return _mlstm_post(hdir, o, xs, modsel, p["norm_g"], p["w_out"], nct)


def _attn_in_body(x_ref, g_ref, m_ref, w_ref, cos_ref, sin_ref, qt_ref, k_ref, vt_ref, *, d_model):
    m = m_ref[0, 0]
    h = _normmod(x_ref[0], g_ref[...], m[1:2], m[0:1])
    z = _mm(h, w_ref[...])
    reps = d_model // LANES
    cos = jnp.concatenate([cos_ref[...]] * reps, axis=1)
    sin = jnp.concatenate([sin_ref[...]] * reps, axis=1)
    lane = lax.broadcasted_iota(jnp.int32, (1, d_model), 1)
    low = (lane % DA_HEAD) < (DA_HEAD // 2)
    half = DA_HEAD // 2

    def rope(u):
        partner = jnp.where(low, pltpu.roll(u, d_model - half, 1), pltpu.roll(u, half, 1))
        return u * cos + partner * sin

    q = rope(z[:, 0:d_model]) * (DA_HEAD ** -0.5 * math.log2(math.e))
    qt_ref[0] = q.T.astype(BF16)
    k_ref[0] = rope(z[:, d_model:2 * d_model]).astype(BF16)
    vt_ref[0, 0] = z[:, 2 * d_model:].T.astype(BF16)


def _attn_in(xs, g, modsel, w_qkv, cos, sin, nct):
    b, t, d = xs.shape
    nt = t // ROW_TILE
    body = functools.partial(_attn_in_body, d_model=d)
    tab = pl.BlockSpec((ROW_TILE, LANES), lambda bb, tt: (tt, 0))
    return pl.pallas_call(
        body,
        grid=(b, nt),
        in_specs=[_row_spec(d), _full_spec((1, d)), _mod_spec(d, nct), _full_spec((d, 3 * d)), tab, tab],
        out_specs=[pl.BlockSpec((1, d, ROW_TILE), lambda bb, tt: (bb, 0, tt)), _row_spec(d),
                   pl.BlockSpec((1, 1, d, ROW_TILE), lambda bb, tt: (bb, tt, 0, 0))],
        out_shape=[jax.ShapeDtypeStruct((b, d, t), BF16), jax.ShapeDtypeStruct((b, t, d), BF16),
                   jax.ShapeDtypeStruct((b, nt, d, ROW_TILE), BF16)],
        compiler_params=_params(("parallel", "parallel"), VMEM_LIMIT_BYTES),
        name="attn_in",
    )(xs, g.reshape(1, d), modsel, w_qkv.astype(BF16), cos, sin)


def _attn_body(lam_ref, qt_ref, k_ref, vt_ref, o_ref, m_ref, l_ref, acc_ref, *, nct, lambda_init):
    qi = pl.program_id(2)
    qt = qt_ref[0]
    dh2, tq = qt.shape
    tk = vt_ref.shape[-1]
    zero = jnp.zeros((DA_HEAD, tq), qt.dtype)
    qs = jnp.concatenate([jnp.concatenate([qt[0:DA_HEAD], zero], axis=1),
                          jnp.concatenate([zero, qt[DA_HEAD:dh2]], axis=1)], axis=0)
    m_ref[...] = jnp.full(m_ref.shape, -jnp.inf, F32)
    l_ref[...] = jnp.zeros(l_ref.shape, F32)
    acc_ref[...] = jnp.zeros(acc_ref.shape, F32)

    def attend(chunks):
        s = [jnp.dot(k_ref[0, pl.ds(pl.multiple_of(j * tk, tk), tk), :], qs, preferred_element_type=F32)
             for j in chunks]
        m_old = m_ref[...]
        m_new = m_old
        for sj in s:
            m_new = jnp.maximum(m_new, jnp.max(sj, axis=0, keepdims=True))
        scale = jnp.exp2(m_old - m_new)
        l_new = l_ref[...] * scale
        acc = acc_ref[...] * scale
        for j, sj in zip(chunks, s):
            p = jnp.exp2(sj - m_new)
            l_new = l_new + jnp.sum(p, axis=0, keepdims=True)
            acc = acc + jnp.dot(vt_ref[0, j], p.astype(BF16), preferred_element_type=F32)
        l_ref[...] = l_new
        acc_ref[...] = acc
        m_ref[...] = m_new

    attend(list(range(nct)))
    n_lat = vt_ref.shape[1] - nct

    @pl.when(qi >= nct)
    def _():
        unroll = math.gcd(n_lat, ATT_UNROLL)

        def block(jj, carry):
            attend([nct + unroll * jj + u for u in range(unroll)])
            return carry
        lax.fori_loop(0, n_lat // unroll, block, 0)

    lam = lam_ref[...]
    lam_full = (jnp.exp(jnp.sum(lam[0:1] * lam[1:2], axis=-1, keepdims=True))
                - jnp.exp(jnp.sum(lam[2:3] * lam[3:4], axis=-1, keepdims=True)) + lambda_init)
    o = acc_ref[...] / l_ref[...]
    o_ref[0] = (o[:, 0:tq] - lam_full * o[:, tq:2 * tq]).T


def _attn(qt, k, vt, lam, lambda_init, nct):
    b, t, d = k.shape
    dh2 = 2 * DA_HEAD
    nt = vt.shape[1]
    body = functools.partial(_attn_body, nct=nct, lambda_init=lambda_init)
    return pl.pallas_call(
        body,
        grid=(b, d // dh2, t // ROW_TILE),
        in_specs=[pl.BlockSpec(lam.shape, lambda bb, hh, qq: (0, 0)),
                  pl.BlockSpec((1, dh2, ROW_TILE), lambda bb, hh, qq: (bb, hh, qq)),
                  pl.BlockSpec((1, t, dh2), lambda bb, hh, qq: (bb, 0, hh)),
                  pl.BlockSpec((1, nt, dh2, ROW_TILE), lambda bb, hh, qq: (bb, 0, hh, 0))],
        out_specs=pl.BlockSpec((1, ROW_TILE, dh2), lambda bb, hh, qq: (bb, qq, hh)),
        out_shape=jax.ShapeDtypeStruct((b, t, d), F32),
        scratch_shapes=[pltpu.VMEM((1, 2 * ROW_TILE), F32), pltpu.VMEM((1, 2 * ROW_TILE), F32),
                        pltpu.VMEM((dh2, 2 * ROW_TILE), F32)],
        compiler_params=_params(("parallel", "parallel", "parallel"), VMEM_LIMIT_BYTES),
        name="diff_attn",
    )(lam, qt, k, vt)


def _attn_post_body(a_ref, x_ref, m_ref, ng_ref, wout_ref, o_ref, *, heads, scale):
    a = a_ref[0]
    dh2 = a.shape[-1] // heads
    parts = []
    for h in range(heads):
        ah = a[:, h * dh2:(h + 1) * dh2]
        parts.append(ah * lax.rsqrt(jnp.mean(ah * ah, axis=-1, keepdims=True) + DA_EPS))
    an = jnp.concatenate(parts, axis=-1) * ng_ref[...] * scale
    o_ref[0] = x_ref[0] + m_ref[0, 0][2:3] * _mm(an, wout_ref[...])


def _attn_post(a, xs, modsel, norm_g, w_out, lambda_init, nct):
    b, t, d = xs.shape
    body = functools.partial(_attn_post_body, heads=DA_HEADS, scale=1.0 - lambda_init)
    return pl.pallas_call(
        body,
        grid=(b, t // ROW_TILE),
        in_specs=[_row_spec(d), _row_spec(d), _mod_spec(d, nct), _full_spec((1, d)), _full_spec((d, d))],
        out_specs=_row_spec(d),
        out_shape=jax.ShapeDtypeStruct((b, t, d), F32),
        compiler_params=_params(("parallel", "parallel"), VMEM_LIMIT_BYTES),
        name="attn_post",
    )(a, xs, modsel, jnp.tile(norm_g, DA_HEADS).reshape(1, d), w_out.astype(BF16))


def _rope_tables(seq, ctx_len):
    rows = seq // GRID_W
    row = jnp.repeat(jnp.arange(rows), GRID_W).astype(F32)
    col = jnp.tile(jnp.arange(GRID_W), rows).astype(F32)
    nf = DA_HEAD // 4
    inv = jnp.power(ROPE_BASE, -jnp.arange(nf, dtype=F32) / nf)
    ang = jnp.concatenate([row[:, None] * inv, col[:, None] * inv], axis=-1)
    cos, sin = jnp.cos(ang), jnp.sin(ang)
    cos = jnp.concatenate([jnp.ones((ctx_len, DA_HEAD // 2), F32), cos], axis=0)
    sin = jnp.concatenate([jnp.zeros((ctx_len, DA_HEAD // 2), F32), sin], axis=0)
    reps = LANES // DA_HEAD
    return (jnp.tile(jnp.concatenate([cos, cos], axis=-1), (1, reps)),
            jnp.tile(jnp.concatenate([-sin, sin], axis=-1), (1, reps)))


def _attn_layer(xs, n1g, modsel, p, lambda_init, nct, ctx_len):
    cos, sin = _rope_tables(xs.shape[1] - ctx_len, ctx_len)
    qt, k, vt = _attn_in(xs, n1g, modsel, p["w_qkv"], cos, sin, nct)
    a = _attn(qt, k, vt, p["lam"].astype(F32), lambda_init, nct)
    return _attn_post(a, xs, modsel, p["norm_g"], p["w_out"], lambda_init, nct)


def kernel(x, c, ctx, c_ctx, ada_w, ada_b, norm1_g, norm2_g, ffn_w_in, ffn_conv_w, ffn_conv_b, ffn_w_out,
           ra_mix, ra_w_rkv, ra_w0, ra_w1, ra_w2, ra_a0, ra_a1, ra_a2, ra_g1, ra_g2, ra_k_k, ra_k_a, ra_r_k,
           ra_lnx_g, ra_lnx_b, ra_w_out, ml_w_in, ml_b_in, ml_norm_g, ml_w_out, da_w_qkv, da_lambda,
           da_norm_g, da_w_out, final_g):
    batch, seq, d = x.shape
    ctx_len = ctx.shape[1]
    assert ctx_len % ROW_TILE == 0 and seq % ROW_TILE == 0 and batch + 1 <= SUBLANES
    nct = ctx_len // ROW_TILE
    depth = ada_w.shape[0]

    xs = jnp.concatenate([ctx, x], axis=1)
    cc = jnp.concatenate([c_ctx[None], c, jnp.zeros((SUBLANES - 1 - batch, d), F32)], axis=0)
    mod = _ada(cc, ada_w, ada_b).reshape(depth, SUBLANES, 6, d)
    modsel = jnp.stack([jnp.broadcast_to(mod[:, 0:1], (depth, batch, 6, d)), mod[:, 1:1 + batch]], axis=2)

    head_id = jnp.arange(d) // RW_HEAD
    ones_bd = (head_id[:, None] == head_id[None, :]).astype(BF16)

    for i in range(depth):
        kind, j = i % N_MIXERS, i // N_MIXERS
        if kind == 0:
            p = dict(mix=ra_mix[j], w_rkv=ra_w_rkv[j], w0=ra_w0[j], w1=ra_w1[j], w2=ra_w2[j], a0=ra_a0[j],
                     a1=ra_a1[j], a2=ra_a2[j], g1=ra_g1[j], g2=ra_g2[j], k_k=ra_k_k[j], k_a=ra_k_a[j],
                     r_k=ra_r_k[j], lnx_g=ra_lnx_g[j], lnx_b=ra_lnx_b[j], w_out=ra_w_out[j])
            xs = _rwkv_layer(xs, norm1_g[i], modsel[i], p, ones_bd, nct, ctx_len)
        elif kind == 1:
            p = dict(w_in=ml_w_in[j], b_in=ml_b_in[j], norm_g=ml_norm_g[j], w_out=ml_w_out[j])
            xs = _mlstm_layer(xs, norm1_g[i], modsel[i], p, nct, ctx_len)
        else:
            lambda_init = 0.8 - 0.6 * math.exp(-0.3 * i)
            p = dict(w_qkv=da_w_qkv[j], lam=da_lambda[j], norm_g=da_norm_g[j], w_out=da_w_out[j])
            xs = _attn_layer(xs, norm1_g[i], modsel[i], p, lambda_init, nct, ctx_len)
        xs = _ffn_call(xs, norm2_g[i], modsel[i], ffn_w_in[i], ffn_conv_w[i], ffn_conv_b[i], ffn_w_out[i], nct)
    return _final_norm(xs, final_g, nct)
```

```python
import functools
import math

import jax
import jax.numpy as jnp
from jax import lax
from jax.experimental import pallas as pl
from jax.experimental.pallas import tpu as pltpu

F32 = jnp.float32
BF16 = jnp.bfloat16

DEPTH = 4
N_MIXERS = 3
GRID_W = 64
NORM_EPS = 1e-6
RW_HEAD = 64
RW_LNX_EPS = 6.4e-4
ML_HEADS = 4
ML_CHUNK = 128
DA_HEADS = 8
DA_HEAD = 64
DA_EPS = 1e-5
ROPE_BASE = 10000.0

LANES = 128
SUBLANES = 8
VMEM_LIMIT_BYTES = 56 * 1024 * 1024

ROW_TILE = 256
RW_CHUNK = 64
RW_GROUP = 4
FFN_COLS = 256
ATT_UNROLL = 8
ATT_ONES = 16


def _params(sem, vmem=None):
    return pltpu.CompilerParams(dimension_semantics=sem, vmem_limit_bytes=vmem)


def _mm(a, b):
    return jnp.dot(a.astype(BF16), b.astype(BF16), preferred_element_type=F32)


def _mm_nt(a, b):
    return lax.dot_general(a.astype(BF16), b.astype(BF16), (((1,), (1,)), ((), ())),
                           preferred_element_type=F32)


def _mm_tn(a, b):
    return lax.dot_general(a.astype(BF16), b.astype(BF16), (((0,), (0,)), ((), ())),
                           preferred_element_type=F32)


def _split3(x):
    hi = x.astype(BF16)
    r1 = x - hi.astype(F32)
    mid = r1.astype(BF16)
    lo = (r1 - mid.astype(F32)).astype(BF16)
    return hi, mid, lo


def _exact_left(m_bf16, x):
    return sum(jnp.dot(m_bf16, p, preferred_element_type=F32) for p in _split3(x))


def _normmod(x, g, sc, sh):
    y = x * lax.rsqrt(jnp.mean(x * x, axis=-1, keepdims=True) + NORM_EPS)
    return y * g * (1.0 + sc) + sh


def _ada_body(c_ref, w_ref, b_ref, o_ref):
    a = c_ref[...]
    a = a * jax.nn.sigmoid(a)
    o_ref[0] = jnp.dot(a, w_ref[0], precision=lax.Precision.HIGHEST,
                       preferred_element_type=F32) + b_ref[0]


def _ada(cc, ada_w, ada_b):
    depth, d, n = ada_w.shape
    nb = n // 4
    return pl.pallas_call(
        _ada_body,
        grid=(depth, n // nb),
        in_specs=[pl.BlockSpec((SUBLANES, d), lambda l, j: (0, 0)),
                  pl.BlockSpec((1, d, nb), lambda l, j: (l, 0, j)),
                  pl.BlockSpec((1, 1, nb), lambda l, j: (l, 0, j))],
        out_specs=pl.BlockSpec((1, SUBLANES, nb), lambda l, j: (l, 0, j)),
        out_shape=jax.ShapeDtypeStruct((depth, SUBLANES, n), F32),
        compiler_params=_params(("arbitrary", "arbitrary"), VMEM_LIMIT_BYTES),
        name="ada_mod",
    )(cc, ada_w, ada_b.reshape(depth, 1, n))


def _row_spec(d, tm=ROW_TILE):
    return pl.BlockSpec((1, tm, d), lambda b, t: (b, t, 0))


def _mod_spec(d, nct):
    return pl.BlockSpec((1, 1, 6, d), lambda b, t: (b, jnp.where(t >= nct, 1, 0), 0, 0))


def _full_spec(shape):
    nd = len(shape)
    return pl.BlockSpec(shape, lambda b, t: (0,) * nd)


def _halo_specs(d, tm, n_rows):
    per = tm // SUBLANES
    last = n_rows // SUBLANES - 1
    prev = pl.BlockSpec((1, SUBLANES, d), lambda b, t: (b, jnp.maximum(t * per - 1, 0), 0))
    nxt = pl.BlockSpec((1, SUBLANES, d), lambda b, t: (b, jnp.minimum((t + 1) * per, last), 0))
    return prev, nxt


def _final_norm_body(x_ref, g_ref, o_ref):
    x = x_ref[0]
    o_ref[0] = x * lax.rsqrt(jnp.mean(x * x, axis=-1, keepdims=True) + NORM_EPS) * g_ref[...]


def _final_norm(xs, g, nct):
    b, t, d = xs.shape
    nt = t // ROW_TILE - nct
    return pl.pallas_call(
        _final_norm_body,
        grid=(b, nt),
        in_specs=[pl.BlockSpec((1, ROW_TILE, d), lambda bb, tt: (bb, tt + nct, 0)),
                  _full_spec((1, d))],
        out_specs=_row_spec(d),
        out_shape=jax.ShapeDtypeStruct((b, nt * ROW_TILE, d), F32),
        compiler_params=_params(("parallel", "parallel")),
        name="final_norm",
    )(xs, g.reshape(1, d))


def _ffn_body(x_ref, xp_ref, xn_ref, g_ref, m_ref, win_ref, cw_ref, cb_ref, wout_ref, o_ref, act_ref,
              *, nct, nt, hidden):
    t = pl.program_id(1)
    x = x_ref[0]
    tm = x.shape[0]
    m = m_ref[0, 0]
    xa = jnp.concatenate([xp_ref[0], x, xn_ref[0]], axis=0)
    h32 = _normmod(xa, g_ref[...], m[4:5], m[3:4])
    ha = h32.astype(BF16)
    hm = h32[SUBLANES:SUBLANES + tm].astype(BF16)
    first = jnp.logical_or(t == 0, t == nct)
    last = jnp.logical_or(t == nct - 1, t == nt - 1)
    rows = lax.broadcasted_iota(jnp.int32, (tm + 2 * SUBLANES, 1), 0)
    dead = jnp.logical_or(rows == jnp.where(first, SUBLANES - 1, -1),
                          rows == jnp.where(last, tm + SUBLANES, -1))
    keep = jnp.where(dead, 0.0, 1.0)
    for j in range(hidden // FFN_COLS):
        lo, hi = j * FFN_COLS, (j + 1) * FFN_COLS
        val = jnp.dot(hm, win_ref[:, lo:hi], preferred_element_type=F32)
        gate = jnp.dot(ha, win_ref[:, hidden + lo:hidden + hi], preferred_element_type=F32) * keep
        g_prev = pltpu.roll(gate, 1, 0)[SUBLANES:SUBLANES + tm]
        g_next = pltpu.roll(gate, tm + 2 * SUBLANES - 1, 0)[SUBLANES:SUBLANES + tm]
        g_mid = gate[SUBLANES:SUBLANES + tm]
        cw = cw_ref[:, lo:hi]
        conv = cw[0:1] * g_prev + cw[1:2] * g_mid + cw[2:3] * g_next + cb_ref[:, lo:hi]
        act_ref[:, lo:hi] = (jax.nn.gelu(conv) * val).astype(BF16)
    out = jnp.dot(act_ref[...], wout_ref[...], preferred_element_type=F32)
    o_ref[0] = x + m[5:6] * out


def _ffn_call(xs, g, modsel, w_in, conv_w, conv_b, w_out, nct):
    b, t, d = xs.shape
    hidden = w_out.shape[0]
    nt = t // ROW_TILE
    prev, nxt = _halo_specs(d, ROW_TILE, t)
    body = functools.partial(_ffn_body, nct=nct, nt=nt, hidden=hidden)
    return pl.pallas_call(
        body,
        grid=(b, nt),
        in_specs=[_row_spec(d), prev, nxt, _full_spec((1, d)), _mod_spec(d, nct),
                  _full_spec((d, 2 * hidden)), _full_spec((3, hidden)), _full_spec((1, hidden)),
                  _full_spec((hidden, d))],
        out_specs=_row_spec(d),
        out_shape=jax.ShapeDtypeStruct((b, t, d), F32),
        scratch_shapes=[pltpu.VMEM((ROW_TILE, hidden), BF16)],
        compiler_params=_params(("parallel", "parallel"), VMEM_LIMIT_BYTES),
        name="conv_ffn",
    )(xs, xs, xs, g.reshape(1, d), modsel, w_in.astype(BF16), conv_w, conv_b.reshape(1, hidden),
      w_out.astype(BF16))


def _seg_sum(x, ones_bd):
    hi = x.astype(BF16)
    lo = (x - hi.astype(F32)).astype(BF16)
    return (jnp.dot(hi, ones_bd, preferred_element_type=F32) + jnp.dot(lo, ones_bd, preferred_element_type=F32))


def _rwkv_prep_body(x_ref, xp_ref, xn_ref, ng_ref, m_ref, mix_ref, wrkv_ref, w1_ref, a1_ref, g1_ref, w2_ref, a2_ref,
                    g2_ref, vec_ref, ones_ref,
                    r_ref, v_ref, kk_ref, g_ref, bonus_ref, lw_ref, kd_ref, ka_ref, *, nct, nt):
    t = pl.program_id(1)
    m = m_ref[0, 0]
    h = _normmod(x_ref[0], ng_ref[...], m[1:2], m[0:1])
    tm = h.shape[0]
    first = jnp.logical_or(t == 0, t == nct)
    last = jnp.logical_or(t == nct - 1, t == nt - 1)
    p_row = _normmod(xp_ref[0], ng_ref[...], m[1:2], m[0:1])[SUBLANES - 1:SUBLANES] * jnp.where(first, 0.0, 1.0)
    n_row = _normmod(xn_ref[0], ng_ref[...], m[1:2], m[0:1])[0:1] * jnp.where(last, 0.0, 1.0)
    rows = lax.broadcasted_iota(jnp.int32, (tm, 1), 0)
    h_prev = jnp.where(rows == 0, p_row, pltpu.roll(h, 1, 0))
    h_next = jnp.where(rows == tm - 1, n_row, pltpu.roll(h, tm - 1, 0))
    xx = 0.5 * (h_prev + h_next) - h
    mix = mix_ref[...]
    vec = vec_ref[...]
    ones_bd = ones_ref[...]

    def shifted(j):
        return h + xx * mix[j:j + 1]

    r = _mm(shifted(0), wrkv_ref[0])
    k = _mm(shifted(1), wrkv_ref[1])
    v = _mm(shifted(2), wrkv_ref[2])
    w_in = jnp.tanh(_mm(shifted(3), w1_ref[...]))
    a_in = _mm(shifted(4), a1_ref[...])
    g = _mm(jax.nn.sigmoid(_mm(shifted(5), g1_ref[...])), g2_ref[...])

    kk = k * vec[4:5]
    kk = kk / jnp.maximum(jnp.sqrt(_seg_sum(kk * kk, ones_bd)), 1e-12)
    kd_sum = None
    for d in range(2):
        w_pre = vec[d:d + 1] + _mm(w_in, w2_ref[d])
        lw_ref[d, 0] = -jax.nn.sigmoid(w_pre) * math.exp(-0.5)
        a = jax.nn.sigmoid(vec[2 + d:3 + d] + _mm(a_in, a2_ref[d]))
        kd = k * (1.0 + (a - 1.0) * vec[5:6])
        kd_ref[d, 0] = kd
        ka_ref[d, 0] = kk * a
        kd_sum = kd if kd_sum is None else kd_sum + kd
    r_ref[0] = r
    v_ref[0] = v
    kk_ref[0] = kk
    g_ref[0] = g
    bonus_ref[0] = _seg_sum(r * kd_sum * vec[6:7], ones_bd) * v


def _rwkv_prep(xs, ng, modsel, mix, w_rkv, w0, w1, w2, a0, a1, a2, g1, g2, k_k, k_a, r_k, ones_bd, nct):
    b, t, d = xs.shape
    nt = t // ROW_TILE
    lora = w1.shape[-1]
    prev, nxt = _halo_specs(d, ROW_TILE, t)
    zeros = jnp.zeros((lora, d), F32)
    w2p = jnp.stack([jnp.concatenate([w2[0], zeros], 0), jnp.concatenate([zeros, w2[1]], 0)]).astype(BF16)
    a2p = jnp.stack([jnp.concatenate([a2[0], zeros], 0), jnp.concatenate([zeros, a2[1]], 0)]).astype(BF16)
    w1c = jnp.concatenate([w1[0], w1[1]], axis=1).astype(BF16)
    a1c = jnp.concatenate([a1[0], a1[1]], axis=1).astype(BF16)
    vec = jnp.stack([w0[0], w0[1], a0[0], a0[1], k_k, k_a, r_k.reshape(d), jnp.zeros((d,), F32)])
    shared = jax.ShapeDtypeStruct((b, t, d), F32)
    directional = jax.ShapeDtypeStruct((2, b, t, d), F32)
    dir_spec = pl.BlockSpec((2, 1, ROW_TILE, d), lambda bb, tt: (0, bb, tt, 0))
    body = functools.partial(_rwkv_prep_body, nct=nct, nt=nt)
    return pl.pallas_call(
        body,
        grid=(b, nt),
        in_specs=[_row_spec(d), prev, nxt, _full_spec((1, d)), _mod_spec(d, nct), _full_spec((6, d)),
                  _full_spec((3, d, d)), _full_spec(w1c.shape), _full_spec(a1c.shape), _full_spec(g1.shape),
                  _full_spec(w2p.shape), _full_spec(a2p.shape), _full_spec(g2.shape),
                  _full_spec((SUBLANES, d)), _full_spec((d, d))],
        out_specs=[_row_spec(d)] * 5 + [dir_spec] * 3,
        out_shape=[shared] * 5 + [directional] * 3,
        compiler_params=_params(("parallel", "parallel"), VMEM_LIMIT_BYTES),
        name="rwkv_prep",
    )(xs, xs, xs, ng.reshape(1, d), modsel, mix, w_rkv.astype(BF16), w1c, a1c, g1.astype(BF16), w2p, a2p, g2.astype(BF16), vec,
      ones_bd)


def _rwkv_scan_body(rf_ref, vf_ref, kkf_ref, lwf_ref, kdf_ref, kaf_ref, rb_ref, vb_ref, kkb_ref, lwb_ref, kdb_ref,
                    kab_ref, yf_ref, yb_ref, s_ref, *, chunk, group):
    @pl.when(pl.program_id(1) == 0)
    def _():
        s_ref[...] = jnp.zeros_like(s_ref)

    c, gc, width = chunk, group * chunk, group * RW_HEAD
    pr = lax.broadcasted_iota(jnp.int32, (2 * c, 2 * gc), 0)
    pc = lax.broadcasted_iota(jnp.int32, (2 * c, 2 * gc), 1)
    wr = lax.broadcasted_iota(jnp.int32, (c, gc), 0)
    wc = lax.broadcasted_iota(jnp.int32, (c, gc), 1)
    eye_w = jnp.where(wr == wc % c, 1.0, 0.0)
    tr = lax.broadcasted_iota(jnp.int32, (c, c), 0)
    tc = lax.broadcasted_iota(jnp.int32, (c, c), 1)
    own = (lax.broadcasted_iota(jnp.int32, (gc, width), 0) // c
           == lax.broadcasted_iota(jnp.int32, (gc, width), 1) // RW_HEAD)
    same_head = (lax.broadcasted_iota(jnp.int32, (width, width), 0) // RW_HEAD
                 == lax.broadcasted_iota(jnp.int32, (width, width), 1) // RW_HEAD)

    def stack(x):
        return jnp.where(own, jnp.concatenate([x] * group, axis=0), 0.0).astype(BF16)

    left, right, carry_in, v_nat, vm, decay, visible, where = [], [], [], [], [], [], [], []
    for di, (r_ref, v_ref, kk_ref, lw_ref, kd_ref, ka_ref, y_ref) in enumerate(
            ((rf_ref, vf_ref, kkf_ref, lwf_ref, kdf_ref, kaf_ref, yf_ref),
             (rb_ref, vb_ref, kkb_ref, lwb_ref, kdb_ref, kab_ref, yb_ref))):
        sgn = 1 - 2 * di
        vis = ((pr % c) - (pc % c)) * sgn > jnp.where(pr < c, 0, -1)
        tri = jnp.where((tr - tc) * sgn >= 0, 1.0, 0.0).astype(BF16)
        lw_all = lw_ref[0, 0]
        cum_all = _exact_left(tri, lw_all)
        tot_all = jnp.sum(lw_all, axis=0, keepdims=True)
        for hg in range(lw_all.shape[1] // width):
            ln = slice(hg * width, (hg + 1) * width)
            lw, cum, tot = lw_all[:, ln], cum_all[:, ln], tot_all[:, ln]
            e_neg = jnp.exp(-cum)
            e_out = jnp.exp(tot - cum)
            kk, ka, kd = kk_ref[0, :, ln], ka_ref[0, 0, :, ln], kd_ref[0, 0, :, ln]
            left.append(jnp.concatenate([kk * jnp.exp(cum - lw), r_ref[0, :, ln] * jnp.exp(cum)],
                                        axis=0).astype(BF16))
            right.append(jnp.concatenate([stack(ka * e_neg), stack(kd * e_neg)], axis=0))
            carry_in.append(jnp.concatenate([ka * e_out, kd * e_out], axis=0).astype(BF16))
            v_nat.append(v_ref[0, :, ln].astype(BF16))
            vm.append(stack(v_ref[0, :, ln]))
            decay.append(jnp.exp(tot))
            visible.append(vis)
            where.append((di, hg, y_ref, ln))
    chains = range(len(where))
    pair = [jnp.where(visible[g], _mm_nt(left[g], right[g]), 0.0) for g in chains]
    xp = [-pair[g][0:c, 0:gc] for g in chains]
    inv = [eye_w + xp[g] for g in chains]
    xp = [_mm(xp[g], stack(xp[g])) for g in chains]
    for _ in range(int(math.log2(c)) - 2):
        both = [_mm(jnp.concatenate([inv[g], xp[g]], axis=0), stack(xp[g])) for g in chains]
        inv = [inv[g] + both[g][0:c] for g in chains]
        xp = [both[g][c:2 * c] for g in chains]
    inv = [inv[g] + _mm(inv[g], stack(xp[g])) for g in chains]
    a_hi = [pair[g][0:c, 0:gc].astype(BF16).astype(F32) for g in chains]
    a_lo = [pair[g][0:c, 0:gc] - a_hi[g] for g in chains]
    t_hi = [inv[g].astype(BF16).astype(F32) for g in chains]
    t_lo = [inv[g] - t_hi[g] for g in chains]
    major = [_mm(jnp.concatenate([a_hi[g], a_lo[g]], axis=0), stack(t_hi[g])) for g in chains]
    minor = [_mm(a_hi[g], stack(t_lo[g])) for g in chains]
    resid = [(eye_w - inv[g]) - (major[g][0:c] + major[g][c:2 * c] + minor[g]) for g in chains]
    inv = [inv[g] + _mm(inv[g], stack(resid[g])) for g in chains]

    s = [s_ref[where[g][0], where[g][1]] for g in chains]
    from_state = [_mm_nt(left[g], s[g]) for g in chains]
    from_v = [_mm(pair[g][:, gc:2 * gc], vm[g]) for g in chains]
    u = [-_mm(inv[g], stack(from_state[g][0:c] + from_v[g][0:c])) for g in chains]
    y = [from_state[g][c:2 * c] + from_v[g][c:2 * c] + _mm(pair[g][c:2 * c, 0:gc], stack(u[g])) for g in chains]
    for g in chains:
        di, hg, y_ref, ln = where[g]
        y_ref[0, :, ln] = y[g]
        grown = _mm_tn(jnp.concatenate([u[g].astype(BF16), v_nat[g]], axis=0), carry_in[g])
        s_ref[di, hg] = s[g] * decay[g] + jnp.where(same_head, grown, 0.0)


def _rwkv_scan(r, v, kk, lw, kd, ka, ctx_len):
    b, t, d = r.shape
    c = RW_CHUNK
    width = RW_GROUP * RW_HEAD
    nc, nck = t // c, ctx_len // c

    def back(i):
        return jnp.where(i < nck, nck - 1 - i, nc - 1 - (i - nck))

    fwd = pl.BlockSpec((1, c, d), lambda bb, i: (bb, i, 0))
    bwd = pl.BlockSpec((1, c, d), lambda bb, i: (bb, back(i), 0))
    fwd_dir = pl.BlockSpec((1, 1, c, d), lambda bb, i: (0, bb, i, 0))
    bwd_dir = pl.BlockSpec((1, 1, c, d), lambda bb, i: (1, bb, back(i), 0))
    body = functools.partial(_rwkv_scan_body, chunk=c, group=RW_GROUP)
    return pl.pallas_call(
        body,
        grid=(b, nc),
        in_specs=[fwd, fwd, fwd, fwd_dir, fwd_dir, fwd_dir, bwd, bwd, bwd, bwd_dir, bwd_dir, bwd_dir],
        out_specs=[fwd, bwd],
        out_shape=[jax.ShapeDtypeStruct((b, t, d), F32)] * 2,
        scratch_shapes=[pltpu.VMEM((2, d // width, width, width), F32)],
        compiler_params=_params(("parallel", "arbitrary"), VMEM_LIMIT_BYTES),
        name="rwkv_scan",
    )(r, v, kk, lw, kd, ka, r, v, kk, lw, kd, ka)


def _rwkv_post_body(yf_ref, yb_ref, bonus_ref, g_ref, x_ref, m_ref, lng_ref, lnb_ref, ones_ref, wout_ref, o_ref):
    ones_bd = ones_ref[...]
    y = yf_ref[0] + yb_ref[0]
    mu = _seg_sum(y, ones_bd) * (1.0 / RW_HEAD)
    yc = y - mu
    var = _seg_sum(yc * yc, ones_bd) * (1.0 / RW_HEAD)
    yn = yc * lax.rsqrt(var + RW_LNX_EPS) * lng_ref[...] + lnb_ref[...]
    out = _mm((yn + bonus_ref[0]) * g_ref[0], wout_ref[...])
    o_ref[0] = x_ref[0] + m_ref[0, 0][2:3] * out


def _rwkv_post(yf, yb, bonus, g, xs, modsel, lnx_g, lnx_b, ones_bd, w_out, nct):
    b, t, d = xs.shape
    return pl.pallas_call(
        _rwkv_post_body,
        grid=(b, t // ROW_TILE),
        in_specs=[_row_spec(d)] * 5 + [_mod_spec(d, nct), _full_spec((1, d)), _full_spec((1, d)),
                                      _full_spec((d, d)), _full_spec((d, d))],
        out_specs=_row_spec(d),
        out_shape=jax.ShapeDtypeStruct((b, t, d), F32),
        compiler_params=_params(("parallel", "parallel"), VMEM_LIMIT_BYTES),
        name="rwkv_post",
    )(yf, yb, bonus, g, xs, modsel, lnx_g.reshape(1, d), lnx_b.reshape(1, d), ones_bd, w_out.astype(BF16))


def _rwkv_layer(xs, n1g, modsel, p, ones_bd, nct, ctx_len):
    r, v, kk, g, bonus, lw, kd, ka = _rwkv_prep(
        xs, n1g, modsel, p["mix"], p["w_rkv"], p["w0"], p["w1"], p["w2"], p["a0"], p["a1"], p["a2"], p["g1"], p["g2"],
        p["k_k"], p["k_a"], p["r_k"], ones_bd, nct)
    yf, yb = _rwkv_scan(r, v, kk, lw, kd, ka, ctx_len)
    return _rwkv_post(yf, yb, bonus, g, xs, modsel, p["lnx_g"], p["lnx_b"], ones_bd, p["w_out"], nct)


def _mlstm_in_body(x_ref, g_ref, m_ref, w_ref, b_ref, q_ref, k_ref, v_ref, o_ref, gf_ref, gb_ref,
                   *, heads, dk, dv):
    m = m_ref[0, 0]
    h = _normmod(x_ref[0], g_ref[...], m[1:2], m[0:1])
    z = _mm(h, w_ref[...]) + b_ref[...]
    nqk = heads * dk
    nv = heads * dv
    q_ref[0] = z[:, 0:nqk]
    k_ref[0] = z[:, nqk:2 * nqk] * (dk ** -0.5)
    v_ref[0] = z[:, 2 * nqk:2 * nqk + nv]
    d_model = o_ref.shape[-1]
    o_ref[0] = jax.nn.sigmoid(z[:, 2 * nqk + nv:2 * nqk + nv + d_model])
    gates = z[:, 2 * nqk + nv + d_model:]
    lane = lax.broadcasted_iota(jnp.int32, (1, LANES), 1)
    is_forget = jnp.logical_and(lane >= heads, lane < 2 * heads)
    for ref, off in ((gf_ref, 0), (gb_ref, LANES)):
        gd = gates[:, off:off + LANES]
        ref[0] = jnp.where(is_forget, jax.nn.log_sigmoid(gd), gd)


def _mlstm_in(xs, g, modsel, w_in, b_in, nct):
    b, t, d = xs.shape
    heads, dk, dv = ML_HEADS, d // (2 * ML_HEADS), d // ML_HEADS
    main = 2 * heads * dk + heads * dv + d
    pad = jnp.zeros((d, LANES - 2 * heads), F32)
    w_cat = jnp.concatenate([w_in[:, :main], w_in[:, main:main + 2 * heads], pad,
                             w_in[:, main + 2 * heads:], pad], axis=1)
    padb = jnp.zeros((LANES - 2 * heads,), F32)
    b_cat = jnp.concatenate([b_in[:main], b_in[main:main + 2 * heads], padb, b_in[main + 2 * heads:], padb])
    n = w_cat.shape[1]
    body = functools.partial(_mlstm_in_body, heads=heads, dk=dk, dv=dv)
    shapes = [(heads * dk), (heads * dk), (heads * dv), d, LANES, LANES]
    return pl.pallas_call(
        body,
        grid=(b, t // ROW_TILE),
        in_specs=[_row_spec(d), _full_spec((1, d)), _mod_spec(d, nct), _full_spec((d, n)), _full_spec((1, n))],
        out_specs=[_row_spec(s) for s in shapes],
        out_shape=[jax.ShapeDtypeStruct((b, t, s), F32) for s in shapes],
        compiler_params=_params(("parallel", "parallel"), VMEM_LIMIT_BYTES),
        name="mlstm_in",
    )(xs, g.reshape(1, d), modsel, w_cat.astype(BF16), b_cat.reshape(1, n))


def _mlstm_chunk_body(q_ref, k_ref, v_ref, gc_ref, gr_ref, h_ref, c_ref, n_ref, m_ref, *, heads, dk, dv):
    d = pl.program_id(1)

    @pl.when(pl.program_id(2) == 0)
    def _():
        c_ref[...] = jnp.zeros_like(c_ref)
        n_ref[...] = jnp.zeros_like(n_ref)
        m_ref[...] = jnp.zeros_like(m_ref)

    t = q_ref.shape[1]
    sgn = 1 - 2 * d
    rr = lax.broadcasted_iota(jnp.int32, (t, t), 0)
    cc = lax.broadcasted_iota(jnp.int32, (t, t), 1)
    seen = (rr - cc) * sgn >= 0
    tri = jnp.where(seen, 1.0, 0.0).astype(BF16)
    gcol = gc_ref[0, 0]
    grow = gr_ref[0, 0]
    bcol = _exact_left(tri, gcol)
    brow = sum(_mm_nt(p, tri) for p in _split3(grow))
    hs = range(heads)
    li_col = [gcol[:, h:h + 1] for h in hs]
    li_row = [grow[h:h + 1, :] for h in hs]
    b_col = [bcol[:, heads + h:heads + h + 1] for h in hs]
    b_row = [brow[heads + h:heads + h + 1, :] for h in hs]
    b_end = [jnp.sum(gcol[:, heads + h:heads + h + 1], axis=0, keepdims=True) for h in hs]
    q = [q_ref[0, :, h * dk:(h + 1) * dk].astype(BF16) for h in hs]
    k = [k_ref[0, :, h * dk:(h + 1) * dk] for h in hs]
    v = [v_ref[0, :, h * dv:(h + 1) * dv].astype(BF16) for h in hs]
    c0 = [c_ref[h] for h in hs]
    n0 = [n_ref[h] for h in hs]
    m0 = [m_ref[h][:, 0:1] for h in hs]
    g_end = [b_end[h] - b_col[h] + li_col[h] for h in hs]
    m_loc = [jnp.max(g_end[h], axis=0, keepdims=True) for h in hs]
    kw = [k[h] * jnp.exp(g_end[h] - m_loc[h]) for h in hs]
    c_loc = [_mm_tn(kw[h], v[h]) for h in hs]
    n_loc = [jnp.sum(kw[h], axis=0, keepdims=True) for h in hs]
    qk = [_mm_nt(q[h], k[h]) for h in hs]
    qc = [_mm(q[h], c0[h]) for h in hs]
    d_log = [jnp.where(seen, b_col[h] - b_row[h] + li_row[h], -jnp.inf) for h in hs]
    inter = [b_col[h] + m0[h] for h in hs]
    m_t = [jnp.maximum(jnp.max(d_log[h], axis=-1, keepdims=True), inter[h]) for h in hs]
    p = [jnp.exp(d_log[h] - m_t[h]) * qk[h] for h in hs]
    e_inter = [jnp.exp(inter[h] - m_t[h]) for h in hs]
    num = [_mm(p[h], v[h]) + e_inter[h] * qc[h] for h in hs]
    for h in hs:
        den = (jnp.sum(p[h], axis=-1, keepdims=True)
               + e_inter[h] * jnp.sum(q_ref[0, :, h * dk:(h + 1) * dk] * n0[h], axis=-1, keepdims=True))
        h_ref[0, 0, :, h * dv:(h + 1) * dv] = num[h] / jnp.maximum(jnp.abs(den), jnp.exp(-m_t[h]))
        m_new = jnp.maximum(b_end[h] + m0[h], m_loc[h])
        a = jnp.exp(b_end[h] + m0[h] - m_new)
        e = jnp.exp(m_loc[h] - m_new)
        c_ref[h] = a * c0[h] + e * c_loc[h]
        n_ref[h] = a * n0[h] + e * n_loc[h]
        m_ref[h] = jnp.broadcast_to(m_new, (1, LANES))


def _mlstm_chunks(q, k, v, gcol, grow, ctx_len):
    b, t, _ = q.shape
    heads = ML_HEADS
    dk, dv = q.shape[-1] // heads, v.shape[-1] // heads
    c = ML_CHUNK
    nc, nck = t // c, ctx_len // c

    def chunk_of(dd, i):
        back = jnp.where(i < nck, nck - 1 - i, nc - 1 - (i - nck))
        return jnp.where(dd == 0, i, back)

    def seq(width):
        return pl.BlockSpec((1, c, width), lambda bb, dd, i: (bb, chunk_of(dd, i), 0))

    body = functools.partial(_mlstm_chunk_body, heads=heads, dk=dk, dv=dv)
    return pl.pallas_call(
        body,
        grid=(b, 2, nc),
        in_specs=[seq(heads * dk), seq(heads * dk), seq(heads * dv),
                  pl.BlockSpec((1, 1, c, LANES), lambda bb, dd, i: (dd, bb, chunk_of(dd, i), 0)),
                  pl.BlockSpec((1, 1, SUBLANES, c), lambda bb, dd, i: (dd, bb, 0, chunk_of(dd, i)))],
        out_specs=pl.BlockSpec((1, 1, c, heads * dv), lambda bb, dd, i: (dd, bb, chunk_of(dd, i), 0)),
        out_shape=jax.ShapeDtypeStruct((2, b, t, heads * dv), F32),
        scratch_shapes=[pltpu.VMEM((heads, dk, dv), F32), pltpu.VMEM((heads, 1, dk), F32),
                        pltpu.VMEM((heads, 1, LANES), F32)],
        compiler_params=_params(("parallel", "parallel", "arbitrary")),
        name="mlstm_chunks",
    )(q, k, v, gcol, grow)


def _mlstm_post_body(h_ref, o_ref, x_ref, m_ref, ng_ref, wout_ref, out_ref, *, heads):
    hs = h_ref[0, 0] + h_ref[1, 0]
    dv = hs.shape[-1] // heads
    parts = []
    for h in range(heads):
        hh = hs[:, h * dv:(h + 1) * dv]
        parts.append(hh * lax.rsqrt(jnp.mean(hh * hh, axis=-1, keepdims=True) + NORM_EPS))
    hn = jnp.concatenate(parts, axis=-1)
    out = _mm(hn * ng_ref[...] * o_ref[0], wout_ref[...])
    out_ref[0] = x_ref[0] + m_ref[0, 0][2:3] * out


def _mlstm_post(hdir, o, xs, modsel, norm_g, w_out, nct):
    b, t, d = xs.shape
    dir_spec = pl.BlockSpec((2, 1, ROW_TILE, d), lambda bb, tt: (0, bb, tt, 0))
    body = functools.partial(_mlstm_post_body, heads=ML_HEADS)
    return pl.pallas_call(
        body,
        grid=(b, t // ROW_TILE),
        in_specs=[dir_spec, _row_spec(d), _row_spec(d), _mod_spec(d, nct), _full_spec((1, d)),
                  _full_spec((d, d))],
        out_specs=_row_spec(d),
        out_shape=jax.ShapeDtypeStruct((b, t, d), F32),
        compiler_params=_params(("parallel", "parallel"), VMEM_LIMIT_BYTES),
        name="mlstm_post",
    )(hdir, o, xs, modsel, norm_g.reshape(1, d), w_out.astype(BF16))


def _mlstm_layer(xs, n1g, modsel, p, nct, ctx_len):
    q, k, v, o, gf, gb = _mlstm_in(xs, n1g, modsel, p["w_in"], p["b_in"], nct)
    gcol = jnp.stack([gf, gb])
    grow = jnp.swapaxes(gcol[..., :SUBLANES], -1, -2)
    hdir = _mlstm_chunks(q, k, v, gcol, grow, ctx_len)
    return _mlstm_post(hdir, o, xs, modsel, p["norm_g"], p["w_out"], nct)


def _attn_in_body(x_ref, g_ref, m_ref, w_ref, cos_ref, sin_ref, qt_ref, k_ref, vt_ref, *, d_model):
    m = m_ref[0, 0]
    h = _normmod(x_ref[0], g_ref[...], m[1:2], m[0:1])
    z = _mm(h, w_ref[...])
    reps = d_model // LANES
    cos = jnp.concatenate([cos_ref[...]] * reps, axis=1)
    sin = jnp.concatenate([sin_ref[...]] * reps, axis=1)
    lane = lax.broadcasted_iota(jnp.int32, (1, d_model), 1)
    low = (lane % DA_HEAD) < (DA_HEAD // 2)
    half = DA_HEAD // 2

    def rope(u):
        partner = jnp.where(low, pltpu.roll(u, d_model - half, 1), pltpu.roll(u, half, 1))
        return u * cos + partner * sin

    q = rope(z[:, 0:d_model]) * (DA_HEAD ** -0.5 * math.log2(math.e))
    qt_ref[0] = q.T.astype(BF16)
    k_ref[0] = rope(z[:, d_model:2 * d_model]).astype(BF16)
    vt = z[:, 2 * d_model:].T.astype(BF16)
    dh2 = 2 * DA_HEAD
    ones = jnp.ones((ATT_ONES, vt.shape[1]), BF16)
    for hh in range(d_model // dh2):
        vt_ref[0, 0, hh] = jnp.concatenate([vt[hh * dh2:(hh + 1) * dh2], ones], axis=0)


def _attn_in(xs, g, modsel, w_qkv, cos, sin, nct):
    b, t, d = xs.shape
    nt = t // ROW_TILE
    heads, vrows = d // (2 * DA_HEAD), 2 * DA_HEAD + ATT_ONES
    body = functools.partial(_attn_in_body, d_model=d)
    tab = pl.BlockSpec((ROW_TILE, LANES), lambda bb, tt: (tt, 0))
    return pl.pallas_call(
        body,
        grid=(b, nt),
        in_specs=[_row_spec(d), _full_spec((1, d)), _mod_spec(d, nct), _full_spec((d, 3 * d)), tab, tab],
        out_specs=[pl.BlockSpec((1, d, ROW_TILE), lambda bb, tt: (bb, 0, tt)), _row_spec(d),
                   pl.BlockSpec((1, 1, heads, vrows, ROW_TILE), lambda bb, tt: (bb, tt, 0, 0, 0))],
        out_shape=[jax.ShapeDtypeStruct((b, d, t), BF16), jax.ShapeDtypeStruct((b, t, d), BF16),
                   jax.ShapeDtypeStruct((b, nt, heads, vrows, ROW_TILE), BF16)],
        compiler_params=_params(("parallel", "parallel"), VMEM_LIMIT_BYTES),
        name="attn_in",
    )(xs, g.reshape(1, d), modsel, w_qkv.astype(BF16), cos, sin)


def _attn_body(lam_ref, qt_ref, k_ref, vt_ref, o_ref, m_ref, acc_ref, sa_ref, sb_ref, *, nct, lambda_init,
               unroll, n_blocks):
    qi = pl.program_id(2)
    qt = qt_ref[0]
    dh2, tq = qt.shape
    tk = vt_ref.shape[-1]
    zero = jnp.zeros((DA_HEAD, tq), qt.dtype)
    qs = jnp.concatenate([jnp.concatenate([qt[0:DA_HEAD], zero], axis=1),
                          jnp.concatenate([zero, qt[DA_HEAD:dh2]], axis=1)], axis=0)
    m_ref[...] = jnp.full(m_ref.shape, -jnp.inf, F32)
    acc_ref[...] = jnp.zeros(acc_ref.shape, F32)

    def scores(j):
        start = pl.multiple_of(j * tk, tk)
        return jnp.dot(k_ref[0, pl.ds(start, tk), :], qs, preferred_element_type=F32)

    def new_max(s_list):
        m_new = m_ref[...]
        for sj in s_list:
            m_new = jnp.maximum(m_new, jnp.max(sj, axis=0, keepdims=True))
        return m_new

    def weigh(j, sj, m_new):
        return jnp.dot(vt_ref[0, j, 0], jnp.exp2(sj - m_new).astype(BF16), preferred_element_type=F32)

    s_ctx = [scores(j) for j in range(nct)]
    m_new = new_max(s_ctx)
    acc = acc_ref[...]
    for j in range(nct):
        acc = acc + weigh(j, s_ctx[j], m_new)
    acc_ref[...] = acc
    m_ref[...] = m_new

    @pl.when(qi >= nct)
    def _():
        def fill(blk, buf):
            for u in range(unroll):
                buf[u] = scores(nct + unroll * blk + u)

        def step(blk, buf, nxt_blk, nxt_buf):
            m_old = m_ref[...]
            m_new = new_max([buf[u] for u in range(unroll)])
            acc = acc_ref[...] * jnp.exp2(m_old - m_new)
            for u in range(unroll):
                if nxt_buf is not None:
                    nxt_buf[u] = scores(nct + unroll * nxt_blk + u)
                acc = acc + weigh(nct + unroll * blk + u, buf[u], m_new)
            acc_ref[...] = acc
            m_ref[...] = m_new

        fill(0, sa_ref)

        def two_blocks(ii, carry):
            step(2 * ii, sa_ref, 2 * ii + 1, sb_ref)
            step(2 * ii + 1, sb_ref, 2 * ii + 2, sa_ref)
            return carry
        lax.fori_loop(0, n_blocks // 2 - 1, two_blocks, 0)
        step(n_blocks - 2, sa_ref, n_blocks - 1, sb_ref)
        step(n_blocks - 1, sb_ref, None, None)

    lam = lam_ref[...]
    lam_full = (jnp.exp(jnp.sum(lam[0:1] * lam[1:2], axis=-1, keepdims=True))
                - jnp.exp(jnp.sum(lam[2:3] * lam[3:4], axis=-1, keepdims=True)) + lambda_init)
    acc = acc_ref[...]
    o = acc[0:dh2] / acc[dh2:dh2 + 1]
    o_ref[0] = (o[:, 0:tq] - lam_full * o[:, tq:2 * tq]).T


def _attn(qt, k, vt, lam, lambda_init, nct):
    b, t, d = k.shape
    dh2 = 2 * DA_HEAD
    nt, vrows = vt.shape[1], vt.shape[3]
    n_lat = nt - nct
    unroll = max(u for u in range(1, ATT_UNROLL + 1) if n_lat % (2 * u) == 0)
    body = functools.partial(_attn_body, nct=nct, lambda_init=lambda_init, unroll=unroll,
                             n_blocks=n_lat // unroll)
    score_buf = pltpu.VMEM((unroll, ROW_TILE, 2 * ROW_TILE), F32)
    return pl.pallas_call(
        body,
        grid=(b, d // dh2, t // ROW_TILE),
        in_specs=[pl.BlockSpec(lam.shape, lambda bb, hh, qq: (0, 0)),
                  pl.BlockSpec((1, dh2, ROW_TILE), lambda bb, hh, qq: (bb, hh, qq)),
                  pl.BlockSpec((1, t, dh2), lambda bb, hh, qq: (bb, 0, hh)),
                  pl.BlockSpec((1, nt, 1, vrows, ROW_TILE), lambda bb, hh, qq: (bb, 0, hh, 0, 0))],
        out_specs=pl.BlockSpec((1, ROW_TILE, dh2), lambda bb, hh, qq: (bb, qq, hh)),
        out_shape=jax.ShapeDtypeStruct((b, t, d), F32),
        scratch_shapes=[pltpu.VMEM((1, 2 * ROW_TILE), F32), pltpu.VMEM((vrows, 2 * ROW_TILE), F32),
                        score_buf, score_buf],
        compiler_params=_params(("parallel", "parallel", "parallel"), VMEM_LIMIT_BYTES),
        name="diff_attn",
    )(lam, qt, k, vt)


def _attn_post_body(a_ref, x_ref, m_ref, ng_ref, wout_ref, o_ref, *, heads, scale):
    a = a_ref[0]
    dh2 = a.shape[-1] // heads
    parts = []
    for h in range(heads):
        ah = a[:, h * dh2:(h + 1) * dh2]
        parts.append(ah * lax.rsqrt(jnp.mean(ah * ah, axis=-1, keepdims=True) + DA_EPS))
    an = jnp.concatenate(parts, axis=-1) * ng_ref[...] * scale
    o_ref[0] = x_ref[0] + m_ref[0, 0][2:3] * _mm(an, wout_ref[...])


def _attn_post(a, xs, modsel, norm_g, w_out, lambda_init, nct):
    b, t, d = xs.shape
    body = functools.partial(_attn_post_body, heads=DA_HEADS, scale=1.0 - lambda_init)
    return pl.pallas_call(
        body,
        grid=(b, t // ROW_TILE),
        in_specs=[_row_spec(d), _row_spec(d), _mod_spec(d, nct), _full_spec((1, d)), _full_spec((d, d))],
        out_specs=_row_spec(d),
        out_shape=jax.ShapeDtypeStruct((b, t, d), F32),
        compiler_params=_params(("parallel", "parallel"), VMEM_LIMIT_BYTES),
        name="attn_post",
    )(a, xs, modsel, jnp.tile(norm_g, DA_HEADS).reshape(1, d), w_out.astype(BF16))


def _rope_tables(seq, ctx_len):
    rows = seq // GRID_W
    row = jnp.repeat(jnp.arange(rows), GRID_W).astype(F32)
    col = jnp.tile(jnp.arange(GRID_W), rows).astype(F32)
    nf = DA_HEAD // 4
    inv = jnp.power(ROPE_BASE, -jnp.arange(nf, dtype=F32) / nf)
    ang = jnp.concatenate([row[:, None] * inv, col[:, None] * inv], axis=-1)
    cos, sin = jnp.cos(ang), jnp.sin(ang)
    cos = jnp.concatenate([jnp.ones((ctx_len, DA_HEAD // 2), F32), cos], axis=0)
    sin = jnp.concatenate([jnp.zeros((ctx_len, DA_HEAD // 2), F32), sin], axis=0)
    reps = LANES // DA_HEAD
    return (jnp.tile(jnp.concatenate([cos, cos], axis=-1), (1, reps)),
            jnp.tile(jnp.concatenate([-sin, sin], axis=-1), (1, reps)))


def _attn_layer(xs, n1g, modsel, p, lambda_init, nct, ctx_len):
    cos, sin = _rope_tables(xs.shape[1] - ctx_len, ctx_len)
    qt, k, vt = _attn_in(xs, n1g, modsel, p["w_qkv"], cos, sin, nct)
    a = _attn(qt, k, vt, p["lam"].astype(F32), lambda_init, nct)
    return _attn_post(a, xs, modsel, p["norm_g"], p["w_out"], lambda_init, nct)


def kernel(x, c, ctx, c_ctx, ada_w, ada_b, norm1_g, norm2_g, ffn_w_in, ffn_conv_w, ffn_conv_b, ffn_w_out,
           ra_mix, ra_w_rkv, ra_w0, ra_w1, ra_w2, ra_a0, ra_a1, ra_a2, ra_g1, ra_g2, ra_k_k, ra_k_a, ra_r_k,
           ra_lnx_g, ra_lnx_b, ra_w_out, ml_w_in, ml_b_in, ml_norm_g, ml_w_out, da_w_qkv, da_lambda,
           da_norm_g, da_w_out, final_g):
    batch, seq, d = x.shape
    ctx_len = ctx.shape[1]
    assert ctx_len % ROW_TILE == 0 and seq % ROW_TILE == 0 and batch + 1 <= SUBLANES
    nct = ctx_len // ROW_TILE
    depth = ada_w.shape[0]

    xs = jnp.concatenate([ctx, x], axis=1)
    cc = jnp.concatenate([c_ctx[None], c, jnp.zeros((SUBLANES - 1 - batch, d), F32)], axis=0)
    mod = _ada(cc, ada_w, ada_b).reshape(depth, SUBLANES, 6, d)
    modsel = jnp.stack([jnp.broadcast_to(mod[:, 0:1], (depth, batch, 6, d)), mod[:, 1:1 + batch]], axis=2)

    head_id = jnp.arange(d) // RW_HEAD
    ones_bd = (head_id[:, None] == head_id[None, :]).astype(BF16)

    for i in range(depth):
        kind, j = i % N_MIXERS, i // N_MIXERS
        if kind == 0:
            p = dict(mix=ra_mix[j], w_rkv=ra_w_rkv[j], w0=ra_w0[j], w1=ra_w1[j], w2=ra_w2[j], a0=ra_a0[j],
                     a1=ra_a1[j], a2=ra_a2[j], g1=ra_g1[j], g2=ra_g2[j], k_k=ra_k_k[j], k_a=ra_k_a[j],
                     r_k=ra_r_k[j], lnx_g=ra_lnx_g[j], lnx_b=ra_lnx_b[j], w_out=ra_w_out[j])
            xs = _rwkv_layer(xs, norm1_g[i], modsel[i], p, ones_bd, nct, ctx_len)
        elif kind == 1:
            p = dict(w_in=ml_w_in[j], b_in=ml_b_in[j], norm_g=ml_norm_g[j], w_out=ml_w_out[j])
            xs = _mlstm_layer(xs, norm1_g[i], modsel[i], p, nct, ctx_len)
        else:
            lambda_init = 0.8 - 0.6 * math.exp(-0.3 * i)
            p = dict(w_qkv=da_w_qkv[j], lam=da_lambda[j], norm_g=da_norm_g[j], w_out=da_w_out[j])
            xs = _attn_layer(xs, norm1_g[i], modsel[i], p, lambda_init, nct, ctx_len)
        xs = _ffn_call(xs, norm2_g[i], modsel[i], ffn_w_in[i], ffn_conv_w[i], ffn_conv_b[i], ffn_w_out[i], nct)
    return _final_norm(xs, final_g, nct)
```

```python
import functools
import math

import jax
import jax.numpy as jnp
from jax import lax
from jax.experimental import pallas as pl
from jax.experimental.pallas import tpu as pltpu

F32 = jnp.float32
BF16 = jnp.bfloat16

DEPTH = 4
N_MIXERS = 3
GRID_W = 64
NORM_EPS = 1e-6
RW_HEAD = 64
RW_LNX_EPS = 6.4e-4
ML_HEADS = 4
ML_CHUNK = 128
DA_HEADS = 8
DA_HEAD = 64
DA_EPS = 1e-5
ROPE_BASE = 10000.0

LANES = 128
SUBLANES = 8
VMEM_LIMIT_BYTES = 56 * 1024 * 1024

ROW_TILE = 256
RW_CHUNK = 64
RW_GROUP = 4
FFN_COLS = 256
ATT_UNROLL = 8
ATT_ONES = 16


def _params(sem, vmem=None):
    return pltpu.CompilerParams(dimension_semantics=sem, vmem_limit_bytes=vmem)


def _mm(a, b):
    return jnp.dot(a.astype(BF16), b.astype(BF16), preferred_element_type=F32)


def _mm_nt(a, b):
    return lax.dot_general(a.astype(BF16), b.astype(BF16), (((1,), (1,)), ((), ())),
                           preferred_element_type=F32)


def _mm_tn(a, b):
    return lax.dot_general(a.astype(BF16), b.astype(BF16), (((0,), (0,)), ((), ())),
                           preferred_element_type=F32)


def _split3(x):
    hi = x.astype(BF16)
    r1 = x - hi.astype(F32)
    mid = r1.astype(BF16)
    lo = (r1 - mid.astype(F32)).astype(BF16)
    return hi, mid, lo


def _exact_left(m_bf16, x, terms=3):
    return sum(jnp.dot(m_bf16, p, preferred_element_type=F32) for p in _split3(x)[:terms])


def _normmod(x, g, sc, sh):
    y = x * lax.rsqrt(jnp.mean(x * x, axis=-1, keepdims=True) + NORM_EPS)
    return y * g * (1.0 + sc) + sh


def _ada_body(c_ref, w_ref, b_ref, o_ref):
    a = c_ref[...]
    a = a * jax.nn.sigmoid(a)
    o_ref[0] = jnp.dot(a, w_ref[0], precision=lax.Precision.HIGHEST,
                       preferred_element_type=F32) + b_ref[0]


def _ada(cc, ada_w, ada_b):
    depth, d, n = ada_w.shape
    nb = n // 4
    return pl.pallas_call(
        _ada_body,
        grid=(depth, n // nb),
        in_specs=[pl.BlockSpec((SUBLANES, d), lambda l, j: (0, 0)),
                  pl.BlockSpec((1, d, nb), lambda l, j: (l, 0, j)),
                  pl.BlockSpec((1, 1, nb), lambda l, j: (l, 0, j))],
        out_specs=pl.BlockSpec((1, SUBLANES, nb), lambda l, j: (l, 0, j)),
        out_shape=jax.ShapeDtypeStruct((depth, SUBLANES, n), F32),
        compiler_params=_params(("arbitrary", "arbitrary"), VMEM_LIMIT_BYTES),
        name="ada_mod",
    )(cc, ada_w, ada_b.reshape(depth, 1, n))


def _row_spec(d, tm=ROW_TILE):
    return pl.BlockSpec((1, tm, d), lambda b, t: (b, t, 0))


def _mod_spec(d, nct):
    return pl.BlockSpec((1, 1, 6, d), lambda b, t: (b, jnp.where(t >= nct, 1, 0), 0, 0))


def _full_spec(shape):
    nd = len(shape)
    return pl.BlockSpec(shape, lambda b, t: (0,) * nd)


def _halo_specs(d, tm, n_rows):
    per = tm // SUBLANES
    last = n_rows // SUBLANES - 1
    prev = pl.BlockSpec((1, SUBLANES, d), lambda b, t: (b, jnp.maximum(t * per - 1, 0), 0))
    nxt = pl.BlockSpec((1, SUBLANES, d), lambda b, t: (b, jnp.minimum((t + 1) * per, last), 0))
    return prev, nxt


def _final_norm_body(x_ref, g_ref, o_ref):
    x = x_ref[0]
    o_ref[0] = x * lax.rsqrt(jnp.mean(x * x, axis=-1, keepdims=True) + NORM_EPS) * g_ref[...]


def _final_norm(xs, g, nct):
    b, t, d = xs.shape
    nt = t // ROW_TILE - nct
    return pl.pallas_call(
        _final_norm_body,
        grid=(b, nt),
        in_specs=[pl.BlockSpec((1, ROW_TILE, d), lambda bb, tt: (bb, tt + nct, 0)),
                  _full_spec((1, d))],
        out_specs=_row_spec(d),
        out_shape=jax.ShapeDtypeStruct((b, nt * ROW_TILE, d), F32),
        compiler_params=_params(("parallel", "parallel")),
        name="final_norm",
    )(xs, g.reshape(1, d))


def _ffn_body(x_ref, xp_ref, xn_ref, g_ref, m_ref, win_ref, cw_ref, cb_ref, wout_ref, o_ref, act_ref,
              *, nct, nt, hidden):
    t = pl.program_id(1)
    x = x_ref[0]
    tm = x.shape[0]
    m = m_ref[0, 0]
    xa = jnp.concatenate([xp_ref[0], x, xn_ref[0]], axis=0)
    h32 = _normmod(xa, g_ref[...], m[4:5], m[3:4])
    ha = h32.astype(BF16)
    hm = h32[SUBLANES:SUBLANES + tm].astype(BF16)
    first = jnp.logical_or(t == 0, t == nct)
    last = jnp.logical_or(t == nct - 1, t == nt - 1)
    rows = lax.broadcasted_iota(jnp.int32, (tm + 2 * SUBLANES, 1), 0)
    dead = jnp.logical_or(rows == jnp.where(first, SUBLANES - 1, -1),
                          rows == jnp.where(last, tm + SUBLANES, -1))
    keep = jnp.where(dead, 0.0, 1.0)
    for j in range(hidden // FFN_COLS):
        lo, hi = j * FFN_COLS, (j + 1) * FFN_COLS
        val = jnp.dot(hm, win_ref[:, lo:hi], preferred_element_type=F32)
        gate = jnp.dot(ha, win_ref[:, hidden + lo:hidden + hi], preferred_element_type=F32) * keep
        g_prev = pltpu.roll(gate, 1, 0)[SUBLANES:SUBLANES + tm]
        g_next = pltpu.roll(gate, tm + 2 * SUBLANES - 1, 0)[SUBLANES:SUBLANES + tm]
        g_mid = gate[SUBLANES:SUBLANES + tm]
        cw = cw_ref[:, lo:hi]
        conv = cw[0:1] * g_prev + cw[1:2] * g_mid + cw[2:3] * g_next + cb_ref[:, lo:hi]
        act_ref[:, lo:hi] = (jax.nn.gelu(conv) * val).astype(BF16)
    out = jnp.dot(act_ref[...], wout_ref[...], preferred_element_type=F32)
    o_ref[0] = x + m[5:6] * out


def _ffn_call(xs, g, modsel, w_in, conv_w, conv_b, w_out, nct):
    b, t, d = xs.shape
    hidden = w_out.shape[0]
    nt = t // ROW_TILE
    prev, nxt = _halo_specs(d, ROW_TILE, t)
    body = functools.partial(_ffn_body, nct=nct, nt=nt, hidden=hidden)
    return pl.pallas_call(
        body,
        grid=(b, nt),
        in_specs=[_row_spec(d), prev, nxt, _full_spec((1, d)), _mod_spec(d, nct),
                  _full_spec((d, 2 * hidden)), _full_spec((3, hidden)), _full_spec((1, hidden)),
                  _full_spec((hidden, d))],
        out_specs=_row_spec(d),
        out_shape=jax.ShapeDtypeStruct((b, t, d), F32),
        scratch_shapes=[pltpu.VMEM((ROW_TILE, hidden), BF16)],
        compiler_params=_params(("parallel", "parallel"), VMEM_LIMIT_BYTES),
        name="conv_ffn",
    )(xs, xs, xs, g.reshape(1, d), modsel, w_in.astype(BF16), conv_w, conv_b.reshape(1, hidden),
      w_out.astype(BF16))


def _seg_sum(x, ones_bd):
    return jnp.dot(x.astype(BF16), ones_bd, preferred_element_type=F32)


def _rwkv_prep_body(x_ref, xp_ref, xn_ref, ng_ref, m_ref, mix_ref, wrkv_ref, w1_ref, a1_ref, g1_ref, w2_ref, a2_ref,
                    g2_ref, vec_ref, ones_ref,
                    r_ref, v_ref, kk_ref, g_ref, bonus_ref, lw_ref, kd_ref, ka_ref, *, nct, nt):
    t = pl.program_id(1)
    m = m_ref[0, 0]
    h = _normmod(x_ref[0], ng_ref[...], m[1:2], m[0:1])
    tm = h.shape[0]
    first = jnp.logical_or(t == 0, t == nct)
    last = jnp.logical_or(t == nct - 1, t == nt - 1)
    p_row = _normmod(xp_ref[0], ng_ref[...], m[1:2], m[0:1])[SUBLANES - 1:SUBLANES] * jnp.where(first, 0.0, 1.0)
    n_row = _normmod(xn_ref[0], ng_ref[...], m[1:2], m[0:1])[0:1] * jnp.where(last, 0.0, 1.0)
    rows = lax.broadcasted_iota(jnp.int32, (tm, 1), 0)
    h_prev = jnp.where(rows == 0, p_row, pltpu.roll(h, 1, 0))
    h_next = jnp.where(rows == tm - 1, n_row, pltpu.roll(h, tm - 1, 0))
    xx = 0.5 * (h_prev + h_next) - h
    mix = mix_ref[...]
    vec = vec_ref[...]
    ones_bd = ones_ref[...]

    def shifted(j):
        return h + xx * mix[j:j + 1]

    r = _mm(shifted(0), wrkv_ref[0])
    k = _mm(shifted(1), wrkv_ref[1])
    v = _mm(shifted(2), wrkv_ref[2])
    w_in = jnp.tanh(_mm(shifted(3), w1_ref[...]))
    a_in = _mm(shifted(4), a1_ref[...])
    g = _mm(jax.nn.sigmoid(_mm(shifted(5), g1_ref[...])), g2_ref[...])

    kk = k * vec[4:5]
    kk = kk / jnp.maximum(jnp.sqrt(_seg_sum(kk * kk, ones_bd)), 1e-12)
    kd_sum = None
    for d in range(2):
        w_pre = vec[d:d + 1] + _mm(w_in, w2_ref[d])
        lw_ref[d, 0] = -jax.nn.sigmoid(w_pre) * math.exp(-0.5)
        a = jax.nn.sigmoid(vec[2 + d:3 + d] + _mm(a_in, a2_ref[d]))
        kd = k * (1.0 + (a - 1.0) * vec[5:6])
        kd_ref[d, 0] = kd
        ka_ref[d, 0] = kk * a
        kd_sum = kd if kd_sum is None else kd_sum + kd
    r_ref[0] = r
    v_ref[0] = v
    kk_ref[0] = kk
    g_ref[0] = g
    bonus_ref[0] = _seg_sum(r * kd_sum * vec[6:7], ones_bd) * v


def _rwkv_prep(xs, ng, modsel, mix, w_rkv, w0, w1, w2, a0, a1, a2, g1, g2, k_k, k_a, r_k, ones_bd, nct):
    b, t, d = xs.shape
    nt = t // ROW_TILE
    lora = w1.shape[-1]
    prev, nxt = _halo_specs(d, ROW_TILE, t)
    zeros = jnp.zeros((lora, d), F32)
    w2p = jnp.stack([jnp.concatenate([w2[0], zeros], 0), jnp.concatenate([zeros, w2[1]], 0)]).astype(BF16)
    a2p = jnp.stack([jnp.concatenate([a2[0], zeros], 0), jnp.concatenate([zeros, a2[1]], 0)]).astype(BF16)
    w1c = jnp.concatenate([w1[0], w1[1]], axis=1).astype(BF16)
    a1c = jnp.concatenate([a1[0], a1[1]], axis=1).astype(BF16)
    vec = jnp.stack([w0[0], w0[1], a0[0], a0[1], k_k, k_a, r_k.reshape(d), jnp.zeros((d,), F32)])
    shared = jax.ShapeDtypeStruct((b, t, d), F32)
    directional = jax.ShapeDtypeStruct((2, b, t, d), F32)
    dir_spec = pl.BlockSpec((2, 1, ROW_TILE, d), lambda bb, tt: (0, bb, tt, 0))
    body = functools.partial(_rwkv_prep_body, nct=nct, nt=nt)
    return pl.pallas_call(
        body,
        grid=(b, nt),
        in_specs=[_row_spec(d), prev, nxt, _full_spec((1, d)), _mod_spec(d, nct), _full_spec((6, d)),
                  _full_spec((3, d, d)), _full_spec(w1c.shape), _full_spec(a1c.shape), _full_spec(g1.shape),
                  _full_spec(w2p.shape), _full_spec(a2p.shape), _full_spec(g2.shape),
                  _full_spec((SUBLANES, d)), _full_spec((d, d))],
        out_specs=[_row_spec(d)] * 5 + [dir_spec] * 3,
        out_shape=[shared] * 5 + [directional] * 3,
        compiler_params=_params(("parallel", "parallel"), VMEM_LIMIT_BYTES),
        name="rwkv_prep",
    )(xs, xs, xs, ng.reshape(1, d), modsel, mix, w_rkv.astype(BF16), w1c, a1c, g1.astype(BF16), w2p, a2p, g2.astype(BF16), vec,
      ones_bd)


def _rwkv_scan_body(rf_ref, vf_ref, kkf_ref, lwf_ref, kdf_ref, kaf_ref, rb_ref, vb_ref, kkb_ref, lwb_ref, kdb_ref,
                    kab_ref, yf_ref, yb_ref, s_ref, *, chunk, group):
    @pl.when(pl.program_id(1) == 0)
    def _():
        s_ref[...] = jnp.zeros_like(s_ref)

    c, gc, width = chunk, group * chunk, group * RW_HEAD
    pr = lax.broadcasted_iota(jnp.int32, (2 * c, 2 * gc), 0)
    pc = lax.broadcasted_iota(jnp.int32, (2 * c, 2 * gc), 1)
    wr = lax.broadcasted_iota(jnp.int32, (c, gc), 0)
    wc = lax.broadcasted_iota(jnp.int32, (c, gc), 1)
    eye_w = jnp.where(wr == wc % c, 1.0, 0.0)
    tr = lax.broadcasted_iota(jnp.int32, (c, c), 0)
    tc = lax.broadcasted_iota(jnp.int32, (c, c), 1)
    own = (lax.broadcasted_iota(jnp.int32, (gc, width), 0) // c
           == lax.broadcasted_iota(jnp.int32, (gc, width), 1) // RW_HEAD)
    same_head = (lax.broadcasted_iota(jnp.int32, (width, width), 0) // RW_HEAD
                 == lax.broadcasted_iota(jnp.int32, (width, width), 1) // RW_HEAD)

    def stack(x):
        return jnp.where(own, jnp.concatenate([x] * group, axis=0), 0.0).astype(BF16)

    def direction(di, r_ref, v_ref, kk_ref, lw_ref, kd_ref, ka_ref, y_ref):
        sgn = 1 - 2 * di
        gs = range(lw_ref.shape[-1] // width)
        lns = [slice(g * width, (g + 1) * width) for g in gs]
        st = {}

        def prepare():
            tri = jnp.where((tr - tc) * sgn >= 0, 1.0, 0.0).astype(BF16)
            lw_all = lw_ref[0, 0]
            cum_all = _exact_left(tri, lw_all, terms=2)
            tot_all = jnp.sum(lw_all, axis=0, keepdims=True)
            st["left"], st["right"], st["carry_in"], st["v_nat"], st["vm"], st["decay"] = [], [], [], [], [], []
            for ln in lns:
                lw, cum, tot = lw_all[:, ln], cum_all[:, ln], tot_all[:, ln]
                e_neg = jnp.exp(-cum)
                e_out = jnp.exp(tot - cum)
                kk, ka, kd = kk_ref[0, :, ln], ka_ref[0, 0, :, ln], kd_ref[0, 0, :, ln]
                st["left"].append(jnp.concatenate([kk * jnp.exp(cum - lw), r_ref[0, :, ln] * jnp.exp(cum)],
                                                  axis=0).astype(BF16))
                st["right"].append(jnp.concatenate([stack(ka * e_neg), stack(kd * e_neg)], axis=0))
                st["carry_in"].append(jnp.concatenate([ka * e_out, kd * e_out], axis=0).astype(BF16))
                st["v_nat"].append(v_ref[0, :, ln].astype(BF16))
                st["vm"].append(stack(v_ref[0, :, ln]))
                st["decay"].append(jnp.exp(tot))

        def pair_products():
            visible = ((pr % c) - (pc % c)) * sgn > jnp.where(pr < c, 0, -1)
            st["pair"] = [jnp.where(visible, _mm_nt(st["left"][g], st["right"][g]), 0.0) for g in gs]
            st["inv"] = [eye_w - st["pair"][g][0:c, 0:gc] for g in gs]
            st["xp"] = [_mm(st["pair"][g][0:c, 0:gc], stack(st["pair"][g][0:c, 0:gc])) for g in gs]

        def double():
            both = [_mm(jnp.concatenate([st["inv"][g], st["xp"][g]], axis=0), stack(st["xp"][g])) for g in gs]
            st["inv"] = [st["inv"][g] + both[g][0:c] for g in gs]
            st["xp"] = [both[g][c:2 * c] for g in gs]

        def last_double():
            st["inv"] = [st["inv"][g] + _mm(st["inv"][g], stack(st["xp"][g])) for g in gs]

        def residual():
            st["resid"] = [(eye_w - st["inv"][g]) - _mm(st["pair"][g][0:c, 0:gc], stack(st["inv"][g])) for g in gs]

        def newton():
            st["inv"] = [st["inv"][g] + _mm(st["inv"][g], stack(st["resid"][g])) for g in gs]

        def read_state():
            st["s"] = [s_ref[di, g] for g in gs]
            st["from_state"] = [_mm_nt(st["left"][g], st["s"][g]) for g in gs]
            st["from_v"] = [_mm(st["pair"][g][:, gc:2 * gc], st["vm"][g]) for g in gs]

        def solve():
            st["u"] = [-_mm(st["inv"][g], stack(st["from_state"][g][0:c] + st["from_v"][g][0:c])) for g in gs]

        def emit():
            for g in gs:
                y_ref[0, :, lns[g]] = (st["from_state"][g][c:2 * c] + st["from_v"][g][c:2 * c]
                                       + _mm(st["pair"][g][c:2 * c, 0:gc], stack(st["u"][g])))
                grown = _mm_tn(jnp.concatenate([st["u"][g].astype(BF16), st["v_nat"][g]], axis=0),
                               st["carry_in"][g])
                s_ref[di, g] = st["s"][g] * st["decay"][g] + jnp.where(same_head, grown, 0.0)

        return ([prepare, pair_products] + [double] * (int(math.log2(c)) - 3)
                + [last_double, residual, newton, read_state, solve, emit])

    forward = direction(0, rf_ref, vf_ref, kkf_ref, lwf_ref, kdf_ref, kaf_ref, yf_ref)
    backward = direction(1, rb_ref, vb_ref, kkb_ref, lwb_ref, kdb_ref, kab_ref, yb_ref)
    for stage_f, stage_b in zip(forward, backward):
        stage_f()
        stage_b()


def _rwkv_scan(r, v, kk, lw, kd, ka, ctx_len):
    b, t, d = r.shape
    c = RW_CHUNK
    width = RW_GROUP * RW_HEAD
    nc, nck = t // c, ctx_len // c

    def back(i):
        return jnp.where(i < nck, nck - 1 - i, nc - 1 - (i - nck))

    fwd = pl.BlockSpec((1, c, d), lambda bb, i: (bb, i, 0))
    bwd = pl.BlockSpec((1, c, d), lambda bb, i: (bb, back(i), 0))
    fwd_dir = pl.BlockSpec((1, 1, c, d), lambda bb, i: (0, bb, i, 0))
    bwd_dir = pl.BlockSpec((1, 1, c, d), lambda bb, i: (1, bb, back(i), 0))
    body = functools.partial(_rwkv_scan_body, chunk=c, group=RW_GROUP)
    return pl.pallas_call(
        body,
        grid=(b, nc),
        in_specs=[fwd, fwd, fwd, fwd_dir, fwd_dir, fwd_dir, bwd, bwd, bwd, bwd_dir, bwd_dir, bwd_dir],
        out_specs=[fwd, bwd],
        out_shape=[jax.ShapeDtypeStruct((b, t, d), F32)] * 2,
        scratch_shapes=[pltpu.VMEM((2, d // width, width, width), F32)],
        compiler_params=_params(("parallel", "arbitrary"), VMEM_LIMIT_BYTES),
        name="rwkv_scan",
    )(r, v, kk, lw, kd, ka, r, v, kk, lw, kd, ka)


def _rwkv_post_body(yf_ref, yb_ref, bonus_ref, g_ref, x_ref, m_ref, lng_ref, lnb_ref, ones_ref, wout_ref, o_ref):
    ones_bd = ones_ref[...]
    y = yf_ref[0] + yb_ref[0]
    mu = _seg_sum(y, ones_bd) * (1.0 / RW_HEAD)
    yc = y - mu
    var = _seg_sum(yc * yc, ones_bd) * (1.0 / RW_HEAD)
    yn = yc * lax.rsqrt(var + RW_LNX_EPS) * lng_ref[...] + lnb_ref[...]
    out = _mm((yn + bonus_ref[0]) * g_ref[0], wout_ref[...])
    o_ref[0] = x_ref[0] + m_ref[0, 0][2:3] * out


def _rwkv_post(yf, yb, bonus, g, xs, modsel, lnx_g, lnx_b, ones_bd, w_out, nct):
    b, t, d = xs.shape
    return pl.pallas_call(
        _rwkv_post_body,
        grid=(b, t // ROW_TILE),
        in_specs=[_row_spec(d)] * 5 + [_mod_spec(d, nct), _full_spec((1, d)), _full_spec((1, d)),
                                      _full_spec((d, d)), _full_spec((d, d))],
        out_specs=_row_spec(d),
        out_shape=jax.ShapeDtypeStruct((b, t, d), F32),
        compiler_params=_params(("parallel", "parallel"), VMEM_LIMIT_BYTES),
        name="rwkv_post",
    )(yf, yb, bonus, g, xs, modsel, lnx_g.reshape(1, d), lnx_b.reshape(1, d), ones_bd, w_out.astype(BF16))


def _rwkv_layer(xs, n1g, modsel, p, ones_bd, nct, ctx_len):
    r, v, kk, g, bonus, lw, kd, ka = _rwkv_prep(
        xs, n1g, modsel, p["mix"], p["w_rkv"], p["w0"], p["w1"], p["w2"], p["a0"], p["a1"], p["a2"], p["g1"], p["g2"],
        p["k_k"], p["k_a"], p["r_k"], ones_bd, nct)
    yf, yb = _rwkv_scan(r, v, kk, lw, kd, ka, ctx_len)
    return _rwkv_post(yf, yb, bonus, g, xs, modsel, p["lnx_g"], p["lnx_b"], ones_bd, p["w_out"], nct)


def _mlstm_in_body(x_ref, g_ref, m_ref, w_ref, b_ref, q_ref, k_ref, v_ref, o_ref, gf_ref, gb_ref,
                   *, heads, dk, dv):
    m = m_ref[0, 0]
    h = _normmod(x_ref[0], g_ref[...], m[1:2], m[0:1])
    z = _mm(h, w_ref[...]) + b_ref[...]
    nqk = heads * dk
    nv = heads * dv
    q_ref[0] = z[:, 0:nqk]
    k_ref[0] = z[:, nqk:2 * nqk] * (dk ** -0.5)
    v_ref[0] = z[:, 2 * nqk:2 * nqk + nv]
    d_model = o_ref.shape[-1]
    o_ref[0] = jax.nn.sigmoid(z[:, 2 * nqk + nv:2 * nqk + nv + d_model])
    gates = z[:, 2 * nqk + nv + d_model:]
    lane = lax.broadcasted_iota(jnp.int32, (1, LANES), 1)
    is_forget = jnp.logical_and(lane >= heads, lane < 2 * heads)
    for ref, off in ((gf_ref, 0), (gb_ref, LANES)):
        gd = gates[:, off:off + LANES]
        ref[0] = jnp.where(is_forget, jax.nn.log_sigmoid(gd), gd)


def _mlstm_in(xs, g, modsel, w_in, b_in, nct):
    b, t, d = xs.shape
    heads, dk, dv = ML_HEADS, d // (2 * ML_HEADS), d // ML_HEADS
    main = 2 * heads * dk + heads * dv + d
    pad = jnp.zeros((d, LANES - 2 * heads), F32)
    w_cat = jnp.concatenate([w_in[:, :main], w_in[:, main:main + 2 * heads], pad,
                             w_in[:, main + 2 * heads:], pad], axis=1)
    padb = jnp.zeros((LANES - 2 * heads,), F32)
    b_cat = jnp.concatenate([b_in[:main], b_in[main:main + 2 * heads], padb, b_in[main + 2 * heads:], padb])
    n = w_cat.shape[1]
    body = functools.partial(_mlstm_in_body, heads=heads, dk=dk, dv=dv)
    shapes = [(heads * dk), (heads * dk), (heads * dv), d, LANES, LANES]
    return pl.pallas_call(
        body,
        grid=(b, t // ROW_TILE),
        in_specs=[_row_spec(d), _full_spec((1, d)), _mod_spec(d, nct), _full_spec((d, n)), _full_spec((1, n))],
        out_specs=[_row_spec(s) for s in shapes],
        out_shape=[jax.ShapeDtypeStruct((b, t, s), F32) for s in shapes],
        compiler_params=_params(("parallel", "parallel"), VMEM_LIMIT_BYTES),
        name="mlstm_in",
    )(xs, g.reshape(1, d), modsel, w_cat.astype(BF16), b_cat.reshape(1, n))


def _mlstm_chunk_body(qf_ref, kf_ref, vf_ref, gcf_ref, grf_ref, qb_ref, kb_ref, vb_ref, gcb_ref, grb_ref,
                      hf_ref, hb_ref, c_ref, n_ref, m_ref, *, heads, dk, dv):
    @pl.when(pl.program_id(1) == 0)
    def _():
        c_ref[...] = jnp.zeros_like(c_ref)
        n_ref[...] = jnp.zeros_like(n_ref)
        m_ref[...] = jnp.zeros_like(m_ref)

    t = qf_ref.shape[1]
    rr = lax.broadcasted_iota(jnp.int32, (t, t), 0)
    cc = lax.broadcasted_iota(jnp.int32, (t, t), 1)
    chains, seen = [], []
    li_col, li_row, b_col, b_row, b_end, q32, q, k, v = [], [], [], [], [], [], [], [], []
    for di, (q_ref, k_ref, v_ref, gc_ref, gr_ref, h_ref) in enumerate(
            ((qf_ref, kf_ref, vf_ref, gcf_ref, grf_ref, hf_ref), (qb_ref, kb_ref, vb_ref, gcb_ref, grb_ref, hb_ref))):
        sgn = 1 - 2 * di
        vis = (rr - cc) * sgn >= 0
        tri = jnp.where(vis, 1.0, 0.0).astype(BF16)
        gcol = gc_ref[0, 0]
        grow = gr_ref[0, 0]
        bcol = _exact_left(tri, gcol)
        brow = sum(_mm_nt(p, tri) for p in _split3(grow))
        for h in range(heads):
            chains.append((di, h, h_ref))
            seen.append(vis)
            li_col.append(gcol[:, h:h + 1])
            li_row.append(grow[h:h + 1, :])
            b_col.append(bcol[:, heads + h:heads + h + 1])
            b_row.append(brow[heads + h:heads + h + 1, :])
            b_end.append(jnp.sum(gcol[:, heads + h:heads + h + 1], axis=0, keepdims=True))
            q32.append(q_ref[0, :, h * dk:(h + 1) * dk])
            q.append(q32[-1].astype(BF16))
            k.append(k_ref[0, :, h * dk:(h + 1) * dk])
            v.append(v_ref[0, :, h * dv:(h + 1) * dv].astype(BF16))
    cs = range(len(chains))
    c0 = [c_ref[chains[i][0], chains[i][1]] for i in cs]
    n0 = [n_ref[chains[i][0], chains[i][1]] for i in cs]
    m0 = [m_ref[chains[i][0], chains[i][1]][:, 0:1] for i in cs]
    g_end = [b_end[i] - b_col[i] + li_col[i] for i in cs]
    m_loc = [jnp.max(g_end[i], axis=0, keepdims=True) for i in cs]
    kw = [k[i] * jnp.exp(g_end[i] - m_loc[i]) for i in cs]
    c_loc = [_mm_tn(kw[i], v[i]) for i in cs]
    n_loc = [jnp.sum(kw[i], axis=0, keepdims=True) for i in cs]
    qk = [_mm_nt(q[i], k[i]) for i in cs]
    qc = [_mm(q[i], c0[i]) for i in cs]
    d_log = [jnp.where(seen[i], b_col[i] - b_row[i] + li_row[i], -jnp.inf) for i in cs]
    inter = [b_col[i] + m0[i] for i in cs]
    m_t = [jnp.maximum(jnp.max(d_log[i], axis=-1, keepdims=True), inter[i]) for i in cs]
    p = [jnp.exp(d_log[i] - m_t[i]) * qk[i] for i in cs]
    e_inter = [jnp.exp(inter[i] - m_t[i]) for i in cs]
    num = [_mm(p[i], v[i]) + e_inter[i] * qc[i] for i in cs]
    for i in cs:
        di, h, h_ref = chains[i]
        den = (jnp.sum(p[i], axis=-1, keepdims=True)
               + e_inter[i] * jnp.sum(q32[i] * n0[i], axis=-1, keepdims=True))
        h_ref[0, :, h * dv:(h + 1) * dv] = num[i] / jnp.maximum(jnp.abs(den), jnp.exp(-m_t[i]))
        m_new = jnp.maximum(b_end[i] + m0[i], m_loc[i])
        a = jnp.exp(b_end[i] + m0[i] - m_new)
        e = jnp.exp(m_loc[i] - m_new)
        c_ref[di, h] = a * c0[i] + e * c_loc[i]
        n_ref[di, h] = a * n0[i] + e * n_loc[i]
        m_ref[di, h] = jnp.broadcast_to(m_new, (1, LANES))


def _mlstm_chunks(q, k, v, gcol, grow, ctx_len):
    b, t, _ = q.shape
    heads = ML_HEADS
    dk, dv = q.shape[-1] // heads, v.shape[-1] // heads
    c = ML_CHUNK
    nc, nck = t // c, ctx_len // c

    def back(i):
        return jnp.where(i < nck, nck - 1 - i, nc - 1 - (i - nck))

    def specs(chunk_of, di):
        seq = lambda width: pl.BlockSpec((1, c, width), lambda bb, i: (bb, chunk_of(i), 0))
        return [seq(heads * dk), seq(heads * dk), seq(heads * dv),
                pl.BlockSpec((1, 1, c, LANES), lambda bb, i: (di, bb, chunk_of(i), 0)),
                pl.BlockSpec((1, 1, SUBLANES, c), lambda bb, i: (di, bb, 0, chunk_of(i)))]

    body = functools.partial(_mlstm_chunk_body, heads=heads, dk=dk, dv=dv)
    return pl.pallas_call(
        body,
        grid=(b, nc),
        in_specs=specs(lambda i: i, 0) + specs(back, 1),
        out_specs=[pl.BlockSpec((1, c, heads * dv), lambda bb, i: (bb, i, 0)),
                   pl.BlockSpec((1, c, heads * dv), lambda bb, i: (bb, back(i), 0))],
        out_shape=[jax.ShapeDtypeStruct((b, t, heads * dv), F32)] * 2,
        scratch_shapes=[pltpu.VMEM((2, heads, dk, dv), F32), pltpu.VMEM((2, heads, 1, dk), F32),
                        pltpu.VMEM((2, heads, 1, LANES), F32)],
        compiler_params=_params(("parallel", "arbitrary")),
        name="mlstm_chunks",
    )(q, k, v, gcol, grow, q, k, v, gcol, grow)


def _mlstm_post_body(hf_ref, hb_ref, o_ref, x_ref, m_ref, ng_ref, wout_ref, out_ref, *, heads):
    hs = hf_ref[0] + hb_ref[0]
    dv = hs.shape[-1] // heads
    parts = []
    for h in range(heads):
        hh = hs[:, h * dv:(h + 1) * dv]
        parts.append(hh * lax.rsqrt(jnp.mean(hh * hh, axis=-1, keepdims=True) + NORM_EPS))
    hn = jnp.concatenate(parts, axis=-1)
    out = _mm(hn * ng_ref[...] * o_ref[0], wout_ref[...])
    out_ref[0] = x_ref[0] + m_ref[0, 0][2:3] * out


def _mlstm_post(hf, hb, o, xs, modsel, norm_g, w_out, nct):
    b, t, d = xs.shape
    body = functools.partial(_mlstm_post_body, heads=ML_HEADS)
    return pl.pallas_call(
        body,
        grid=(b, t // ROW_TILE),
        in_specs=[_row_spec(d)] * 4 + [_mod_spec(d, nct), _full_spec((1, d)), _full_spec((d, d))],
        out_specs=_row_spec(d),
        out_shape=jax.ShapeDtypeStruct((b, t, d), F32),
        compiler_params=_params(("parallel", "parallel"), VMEM_LIMIT_BYTES),
        name="mlstm_post",
    )(hf, hb, o, xs, modsel, norm_g.reshape(1, d), w_out.astype(BF16))


def _mlstm_layer(xs, n1g, modsel, p, nct, ctx_len):
    q, k, v, o, gf, gb = _mlstm_in(xs, n1g, modsel, p["w_in"], p["b_in"], nct)
    gcol = jnp.stack([gf, gb])
    grow = jnp.swapaxes(gcol[..., :SUBLANES], -1, -2)
    hf, hb = _mlstm_chunks(q, k, v, gcol, grow, ctx_len)
    return _mlstm_post(hf, hb, o, xs, modsel, p["norm_g"], p["w_out"], nct)


def _attn_in_body(x_ref, g_ref, m_ref, w_ref, cos_ref, sin_ref, qt_ref, k_ref, vt_ref, *, d_model):
    m = m_ref[0, 0]
    h = _normmod(x_ref[0], g_ref[...], m[1:2], m[0:1])
    z = _mm(h, w_ref[...])
    reps = d_model // LANES
    cos = jnp.concatenate([cos_ref[...]] * reps, axis=1)
    sin = jnp.concatenate([sin_ref[...]] * reps, axis=1)
    lane = lax.broadcasted_iota(jnp.int32, (1, d_model), 1)
    low = (lane % DA_HEAD) < (DA_HEAD // 2)
    half = DA_HEAD // 2

    def rope(u):
        partner = jnp.where(low, pltpu.roll(u, d_model - half, 1), pltpu.roll(u, half, 1))
        return u * cos + partner * sin

    q = rope(z[:, 0:d_model]) * (DA_HEAD ** -0.5 * math.log2(math.e))
    qt_ref[0] = q.T.astype(BF16)
    k_ref[0] = rope(z[:, d_model:2 * d_model]).astype(BF16)
    vt = z[:, 2 * d_model:].T.astype(BF16)
    dh2 = 2 * DA_HEAD
    ones = jnp.ones((ATT_ONES, vt.shape[1]), BF16)
    for hh in range(d_model // dh2):
        vt_ref[0, 0, hh] = jnp.concatenate([vt[hh * dh2:(hh + 1) * dh2], ones], axis=0)


def _attn_in(xs, g, modsel, w_qkv, cos, sin, nct):
    b, t, d = xs.shape
    nt = t // ROW_TILE
    heads, vrows = d // (2 * DA_HEAD), 2 * DA_HEAD + ATT_ONES
    body = functools.partial(_attn_in_body, d_model=d)
    tab = pl.BlockSpec((ROW_TILE, LANES), lambda bb, tt: (tt, 0))
    return pl.pallas_call(
        body,
        grid=(b, nt),
        in_specs=[_row_spec(d), _full_spec((1, d)), _mod_spec(d, nct), _full_spec((d, 3 * d)), tab, tab],
        out_specs=[pl.BlockSpec((1, d, ROW_TILE), lambda bb, tt: (bb, 0, tt)), _row_spec(d),
                   pl.BlockSpec((1, 1, heads, vrows, ROW_TILE), lambda bb, tt: (bb, tt, 0, 0, 0))],
        out_shape=[jax.ShapeDtypeStruct((b, d, t), BF16), jax.ShapeDtypeStruct((b, t, d), BF16),
                   jax.ShapeDtypeStruct((b, nt, heads, vrows, ROW_TILE), BF16)],
        compiler_params=_params(("parallel", "parallel"), VMEM_LIMIT_BYTES),
        name="attn_in",
    )(xs, g.reshape(1, d), modsel, w_qkv.astype(BF16), cos, sin)


def _attn_body(lam_ref, qt_ref, k_ref, vt_ref, o_ref, m_ref, acc_ref, sa_ref, sb_ref, *, nct, lambda_init,
               unroll, n_blocks):
    qi = pl.program_id(2)
    qt = qt_ref[0]
    dh2, tq = qt.shape
    tk = vt_ref.shape[-1]
    zero = jnp.zeros((DA_HEAD, tq), qt.dtype)
    qs = jnp.concatenate([jnp.concatenate([qt[0:DA_HEAD], zero], axis=1),
                          jnp.concatenate([zero, qt[DA_HEAD:dh2]], axis=1)], axis=0)
    m_ref[...] = jnp.full(m_ref.shape, -jnp.inf, F32)
    acc_ref[...] = jnp.zeros(acc_ref.shape, F32)

    def scores(j):
        start = pl.multiple_of(j * tk, tk)
        return jnp.dot(k_ref[0, pl.ds(start, tk), :], qs, preferred_element_type=F32)

    def new_max(s_list):
        m_new = m_ref[...]
        for sj in s_list:
            m_new = jnp.maximum(m_new, jnp.max(sj, axis=0, keepdims=True))
        return m_new

    def weigh(j, sj, m_new):
        return jnp.dot(vt_ref[0, j, 0], jnp.exp2(sj - m_new).astype(BF16), preferred_element_type=F32)

    s_ctx = [scores(j) for j in range(nct)]
    m_new = new_max(s_ctx)
    acc = acc_ref[...]
    for j in range(nct):
        acc = acc + weigh(j, s_ctx[j], m_new)
    acc_ref[...] = acc
    m_ref[...] = m_new

    @pl.when(qi >= nct)
    def _():
        def fill(blk, buf):
            for u in range(unroll):
                buf[u] = scores(nct + unroll * blk + u)

        def step(blk, buf, nxt_blk, nxt_buf):
            m_old = m_ref[...]
            m_new = new_max([buf[u] for u in range(unroll)])
            acc = acc_ref[...] * jnp.exp2(m_old - m_new)
            for u in range(unroll):
                if nxt_buf is not None:
                    nxt_buf[u] = scores(nct + unroll * nxt_blk + u)
                acc = acc + weigh(nct + unroll * blk + u, buf[u], m_new)
            acc_ref[...] = acc
            m_ref[...] = m_new

        fill(0, sa_ref)

        def two_blocks(ii, carry):
            step(2 * ii, sa_ref, 2 * ii + 1, sb_ref)
            step(2 * ii + 1, sb_ref, 2 * ii + 2, sa_ref)
            return carry
        lax.fori_loop(0, n_blocks // 2 - 1, two_blocks, 0)
        step(n_blocks - 2, sa_ref, n_blocks - 1, sb_ref)
        step(n_blocks - 1, sb_ref, None, None)

    lam = lam_ref[...]
    lam_full = (jnp.exp(jnp.sum(lam[0:1] * lam[1:2], axis=-1, keepdims=True))
                - jnp.exp(jnp.sum(lam[2:3] * lam[3:4], axis=-1, keepdims=True)) + lambda_init)
    acc = acc_ref[...]
    o = acc[0:dh2] / acc[dh2:dh2 + 1]
    o_ref[0] = (o[:, 0:tq] - lam_full * o[:, tq:2 * tq]).T


def _attn(qt, k, vt, lam, lambda_init, nct):
    b, t, d = k.shape
    dh2 = 2 * DA_HEAD
    nt, vrows = vt.shape[1], vt.shape[3]
    n_lat = nt - nct
    unroll = max(u for u in range(1, ATT_UNROLL + 1) if n_lat % (2 * u) == 0)
    body = functools.partial(_attn_body, nct=nct, lambda_init=lambda_init, unroll=unroll,
                             n_blocks=n_lat // unroll)
    score_buf = pltpu.VMEM((unroll, ROW_TILE, 2 * ROW_TILE), F32)
    return pl.pallas_call(
        body,
        grid=(b, d // dh2, t // ROW_TILE),
        in_specs=[pl.BlockSpec(lam.shape, lambda bb, hh, qq: (0, 0)),
                  pl.BlockSpec((1, dh2, ROW_TILE), lambda bb, hh, qq: (bb, hh, qq)),
                  pl.BlockSpec((1, t, dh2), lambda bb, hh, qq: (bb, 0, hh)),
                  pl.BlockSpec((1, nt, 1, vrows, ROW_TILE), lambda bb, hh, qq: (bb, 0, hh, 0, 0))],
        out_specs=pl.BlockSpec((1, ROW_TILE, dh2), lambda bb, hh, qq: (bb, qq, hh)),
        out_shape=jax.ShapeDtypeStruct((b, t, d), F32),
        scratch_shapes=[pltpu.VMEM((1, 2 * ROW_TILE), F32), pltpu.VMEM((vrows, 2 * ROW_TILE), F32),
                        score_buf, score_buf],
        compiler_params=_params(("parallel", "parallel", "parallel"), VMEM_LIMIT_BYTES),
        name="diff_attn",
    )(lam, qt, k, vt)


def _attn_post_body(a_ref, x_ref, m_ref, ng_ref, wout_ref, o_ref, *, heads, scale):
    a = a_ref[0]
    dh2 = a.shape[-1] // heads
    parts = []
    for h in range(heads):
        ah = a[:, h * dh2:(h + 1) * dh2]
        parts.append(ah * lax.rsqrt(jnp.mean(ah * ah, axis=-1, keepdims=True) + DA_EPS))
    an = jnp.concatenate(parts, axis=-1) * ng_ref[...] * scale
    o_ref[0] = x_ref[0] + m_ref[0, 0][2:3] * _mm(an, wout_ref[...])


def _attn_post(a, xs, modsel, norm_g, w_out, lambda_init, nct):
    b, t, d = xs.shape
    body = functools.partial(_attn_post_body, heads=DA_HEADS, scale=1.0 - lambda_init)
    return pl.pallas_call(
        body,
        grid=(b, t // ROW_TILE),
        in_specs=[_row_spec(d), _row_spec(d), _mod_spec(d, nct), _full_spec((1, d)), _full_spec((d, d))],
        out_specs=_row_spec(d),
        out_shape=jax.ShapeDtypeStruct((b, t, d), F32),
        compiler_params=_params(("parallel", "parallel"), VMEM_LIMIT_BYTES),
        name="attn_post",
    )(a, xs, modsel, jnp.tile(norm_g, DA_HEADS).reshape(1, d), w_out.astype(BF16))


def _rope_tables(seq, ctx_len):
    rows = seq // GRID_W
    row = jnp.repeat(jnp.arange(rows), GRID_W).astype(F32)
    col = jnp.tile(jnp.arange(GRID_W), rows).astype(F32)
    nf = DA_HEAD // 4
    inv = jnp.power(ROPE_BASE, -jnp.arange(nf, dtype=F32) / nf)
    ang = jnp.concatenate([row[:, None] * inv, col[:, None] * inv], axis=-1)
    cos, sin = jnp.cos(ang), jnp.sin(ang)
    cos = jnp.concatenate([jnp.ones((ctx_len, DA_HEAD // 2), F32), cos], axis=0)
    sin = jnp.concatenate([jnp.zeros((ctx_len, DA_HEAD // 2), F32), sin], axis=0)
    reps = LANES // DA_HEAD
    return (jnp.tile(jnp.concatenate([cos, cos], axis=-1), (1, reps)),
            jnp.tile(jnp.concatenate([-sin, sin], axis=-1), (1, reps)))


def _attn_layer(xs, n1g, modsel, p, lambda_init, nct, ctx_len):
    cos, sin = _rope_tables(xs.shape[1] - ctx_len, ctx_len)
    qt, k, vt = _attn_in(xs, n1g, modsel, p["w_qkv"], cos, sin, nct)
    a = _attn(qt, k, vt, p["lam"].astype(F32), lambda_init, nct)
    return _attn_post(a, xs, modsel, p["norm_g"], p["w_out"], lambda_init, nct)


def kernel(x, c, ctx, c_ctx, ada_w, ada_b, norm1_g, norm2_g, ffn_w_in, ffn_conv_w, ffn_conv_b, ffn_w_out,
           ra_mix, ra_w_rkv, ra_w0, ra_w1, ra_w2, ra_a0, ra_a1, ra_a2, ra_g1, ra_g2, ra_k_k, ra_k_a, ra_r_k,
           ra_lnx_g, ra_lnx_b, ra_w_out, ml_w_in, ml_b_in, ml_norm_g, ml_w_out, da_w_qkv, da_lambda,
           da_norm_g, da_w_out, final_g):
    batch, seq, d = x.shape
    ctx_len = ctx.shape[1]
    assert ctx_len % ROW_TILE == 0 and seq % ROW_TILE == 0 and batch + 1 <= SUBLANES
    nct = ctx_len // ROW_TILE
    depth = ada_w.shape[0]

    xs = jnp.concatenate([ctx, x], axis=1)
    cc = jnp.concatenate([c_ctx[None], c, jnp.zeros((SUBLANES - 1 - batch, d), F32)], axis=0)
    mod = _ada(cc, ada_w, ada_b).reshape(depth, SUBLANES, 6, d)
    modsel = jnp.stack([jnp.broadcast_to(mod[:, 0:1], (depth, batch, 6, d)), mod[:, 1:1 + batch]], axis=2)

    head_id = jnp.arange(d) // RW_HEAD
    ones_bd = (head_id[:, None] == head_id[None, :]).astype(BF16)

    for i in range(depth):
        kind, j = i % N_MIXERS, i // N_MIXERS
        if kind == 0:
            p = dict(mix=ra_mix[j], w_rkv=ra_w_rkv[j], w0=ra_w0[j], w1=ra_w1[j], w2=ra_w2[j], a0=ra_a0[j],
                     a1=ra_a1[j], a2=ra_a2[j], g1=ra_g1[j], g2=ra_g2[j], k_k=ra_k_k[j], k_a=ra_k_a[j],
                     r_k=ra_r_k[j], lnx_g=ra_lnx_g[j], lnx_b=ra_lnx_b[j], w_out=ra_w_out[j])
            xs = _rwkv_layer(xs, norm1_g[i], modsel[i], p, ones_bd, nct, ctx_len)
        elif kind == 1:
            p = dict(w_in=ml_w_in[j], b_in=ml_b_in[j], norm_g=ml_norm_g[j], w_out=ml_w_out[j])
            xs = _mlstm_layer(xs, norm1_g[i], modsel[i], p, nct, ctx_len)
        else:
            lambda_init = 0.8 - 0.6 * math.exp(-0.3 * i)
            p = dict(w_qkv=da_w_qkv[j], lam=da_lambda[j], norm_g=da_norm_g[j], w_out=da_w_out[j])
            xs = _attn_layer(xs, norm1_g[i], modsel[i], p, lambda_init, nct, ctx_len)
        xs = _ffn_call(xs, norm2_g[i], modsel[i], ffn_w_in[i], ffn_conv_w[i], ffn_conv_b[i], ffn_w_out[i], nct)
    return _final_norm(xs, final_g, nct)
```

```python
import functools
import math

import jax
import jax.numpy as jnp
from jax import lax
from jax.experimental import pallas as pl
from jax.experimental.pallas import tpu as pltpu

F32 = jnp.float32
BF16 = jnp.bfloat16

DEPTH = 4
N_MIXERS = 3
GRID_W = 64
NORM_EPS = 1e-6
RW_HEAD = 64
RW_LNX_EPS = 6.4e-4
ML_HEADS = 4
ML_CHUNK = 128
DA_HEADS = 8
DA_HEAD = 64
DA_EPS = 1e-5
ROPE_BASE = 10000.0

LANES = 128
SUBLANES = 8
VMEM_LIMIT_BYTES = 56 * 1024 * 1024

ROW_TILE = 256
RW_CHUNK = 64
RW_GROUP = 4
FFN_COLS = 256
ATT_UNROLL = 4
ATT_ONES = 16


def _params(sem, vmem=None):
    return pltpu.CompilerParams(dimension_semantics=sem, vmem_limit_bytes=vmem)


def _mm(a, b):
    return jnp.dot(a.astype(BF16), b.astype(BF16), preferred_element_type=F32)


def _mm_nt(a, b):
    return lax.dot_general(a.astype(BF16), b.astype(BF16), (((1,), (1,)), ((), ())),
                           preferred_element_type=F32)


def _mm_tn(a, b):
    return lax.dot_general(a.astype(BF16), b.astype(BF16), (((0,), (0,)), ((), ())),
                           preferred_element_type=F32)


def _split3(x):
    hi = x.astype(BF16)
    r1 = x - hi.astype(F32)
    mid = r1.astype(BF16)
    lo = (r1 - mid.astype(F32)).astype(BF16)
    return hi, mid, lo


def _exact_left(m_bf16, x, terms=3):
    return sum(jnp.dot(m_bf16, p, preferred_element_type=F32) for p in _split3(x)[:terms])


def _normmod(x, g, sc, sh):
    y = x * lax.rsqrt(jnp.mean(x * x, axis=-1, keepdims=True) + NORM_EPS)
    return y * g * (1.0 + sc) + sh


def _ada_body(c_ref, w_ref, b_ref, o_ref):
    a = c_ref[...]
    a = a * jax.nn.sigmoid(a)
    o_ref[0] = jnp.dot(a, w_ref[0], precision=lax.Precision.HIGHEST,
                       preferred_element_type=F32) + b_ref[0]


def _ada(cc, ada_w, ada_b):
    depth, d, n = ada_w.shape
    nb = n // 4
    return pl.pallas_call(
        _ada_body,
        grid=(depth, n // nb),
        in_specs=[pl.BlockSpec((SUBLANES, d), lambda l, j: (0, 0)),
                  pl.BlockSpec((1, d, nb), lambda l, j: (l, 0, j)),
                  pl.BlockSpec((1, 1, nb), lambda l, j: (l, 0, j))],
        out_specs=pl.BlockSpec((1, SUBLANES, nb), lambda l, j: (l, 0, j)),
        out_shape=jax.ShapeDtypeStruct((depth, SUBLANES, n), F32),
        compiler_params=_params(("arbitrary", "arbitrary"), VMEM_LIMIT_BYTES),
        name="ada_mod",
    )(cc, ada_w, ada_b.reshape(depth, 1, n))


def _row_spec(d, tm=ROW_TILE):
    return pl.BlockSpec((1, tm, d), lambda b, t: (b, t, 0))


def _mod_spec(d, nct):
    return pl.BlockSpec((1, 1, 6, d), lambda b, t: (b, jnp.where(t >= nct, 1, 0), 0, 0))


def _full_spec(shape):
    nd = len(shape)
    return pl.BlockSpec(shape, lambda b, t: (0,) * nd)


def _halo_specs(d, tm, n_rows):
    per = tm // SUBLANES
    last = n_rows // SUBLANES - 1
    prev = pl.BlockSpec((1, SUBLANES, d), lambda b, t: (b, jnp.maximum(t * per - 1, 0), 0))
    nxt = pl.BlockSpec((1, SUBLANES, d), lambda b, t: (b, jnp.minimum((t + 1) * per, last), 0))
    return prev, nxt


def _final_norm_body(x_ref, g_ref, o_ref):
    x = x_ref[0]
    o_ref[0] = x * lax.rsqrt(jnp.mean(x * x, axis=-1, keepdims=True) + NORM_EPS) * g_ref[...]


def _final_norm(xs, g, nct):
    b, t, d = xs.shape
    nt = t // ROW_TILE - nct
    return pl.pallas_call(
        _final_norm_body,
        grid=(b, nt),
        in_specs=[pl.BlockSpec((1, ROW_TILE, d), lambda bb, tt: (bb, tt + nct, 0)),
                  _full_spec((1, d))],
        out_specs=_row_spec(d),
        out_shape=jax.ShapeDtypeStruct((b, nt * ROW_TILE, d), F32),
        compiler_params=_params(("parallel", "parallel")),
        name="final_norm",
    )(xs, g.reshape(1, d))


def _ffn_body(x_ref, xp_ref, xn_ref, g_ref, m_ref, win_ref, cw_ref, cb_ref, wout_ref, o_ref, act_ref,
              *, nct, nt, hidden):
    t = pl.program_id(1)
    x = x_ref[0]
    tm = x.shape[0]
    m = m_ref[0, 0]
    xa = jnp.concatenate([xp_ref[0], x, xn_ref[0]], axis=0)
    h32 = _normmod(xa, g_ref[...], m[4:5], m[3:4])
    ha = h32.astype(BF16)
    hm = h32[SUBLANES:SUBLANES + tm].astype(BF16)
    first = jnp.logical_or(t == 0, t == nct)
    last = jnp.logical_or(t == nct - 1, t == nt - 1)
    rows = lax.broadcasted_iota(jnp.int32, (tm + 2 * SUBLANES, 1), 0)
    dead = jnp.logical_or(rows == jnp.where(first, SUBLANES - 1, -1),
                          rows == jnp.where(last, tm + SUBLANES, -1))
    keep = jnp.where(dead, 0.0, 1.0)
    for j in range(hidden // FFN_COLS):
        lo, hi = j * FFN_COLS, (j + 1) * FFN_COLS
        val = jnp.dot(hm, win_ref[:, lo:hi], preferred_element_type=F32)
        gate = jnp.dot(ha, win_ref[:, hidden + lo:hidden + hi], preferred_element_type=F32) * keep
        g_prev = pltpu.roll(gate, 1, 0)[SUBLANES:SUBLANES + tm]
        g_next = pltpu.roll(gate, tm + 2 * SUBLANES - 1, 0)[SUBLANES:SUBLANES + tm]
        g_mid = gate[SUBLANES:SUBLANES + tm]
        cw = cw_ref[:, lo:hi]
        conv = cw[0:1] * g_prev + cw[1:2] * g_mid + cw[2:3] * g_next + cb_ref[:, lo:hi]
        act_ref[:, lo:hi] = (jax.nn.gelu(conv) * val).astype(BF16)
    out = jnp.dot(act_ref[...], wout_ref[...], preferred_element_type=F32)
    o_ref[0] = x + m[5:6] * out


def _ffn_call(xs, g, modsel, w_in, conv_w, conv_b, w_out, nct):
    b, t, d = xs.shape
    hidden = w_out.shape[0]
    nt = t // ROW_TILE
    prev, nxt = _halo_specs(d, ROW_TILE, t)
    body = functools.partial(_ffn_body, nct=nct, nt=nt, hidden=hidden)
    return pl.pallas_call(
        body,
        grid=(b, nt),
        in_specs=[_row_spec(d), prev, nxt, _full_spec((1, d)), _mod_spec(d, nct),
                  _full_spec((d, 2 * hidden)), _full_spec((3, hidden)), _full_spec((1, hidden)),
                  _full_spec((hidden, d))],
        out_specs=_row_spec(d),
        out_shape=jax.ShapeDtypeStruct((b, t, d), F32),
        scratch_shapes=[pltpu.VMEM((ROW_TILE, hidden), BF16)],
        compiler_params=_params(("parallel", "parallel"), VMEM_LIMIT_BYTES),
        name="conv_ffn",
    )(xs, xs, xs, g.reshape(1, d), modsel, w_in.astype(BF16), conv_w, conv_b.reshape(1, hidden),
      w_out.astype(BF16))


def _seg_sum(x, ones_bd):
    return jnp.dot(x.astype(BF16), ones_bd, preferred_element_type=F32)


def _rwkv_prep_body(x_ref, xp_ref, xn_ref, ng_ref, m_ref, mix_ref, wrkv_ref, w1_ref, a1_ref, g1_ref, w2_ref, a2_ref,
                    g2_ref, vec_ref, ones_ref,
                    r_ref, v_ref, kk_ref, g_ref, bonus_ref, lw_ref, kd_ref, ka_ref, *, nct, nt):
    t = pl.program_id(1)
    m = m_ref[0, 0]
    h = _normmod(x_ref[0], ng_ref[...], m[1:2], m[0:1])
    tm = h.shape[0]
    first = jnp.logical_or(t == 0, t == nct)
    last = jnp.logical_or(t == nct - 1, t == nt - 1)
    p_row = _normmod(xp_ref[0], ng_ref[...], m[1:2], m[0:1])[SUBLANES - 1:SUBLANES] * jnp.where(first, 0.0, 1.0)
    n_row = _normmod(xn_ref[0], ng_ref[...], m[1:2], m[0:1])[0:1] * jnp.where(last, 0.0, 1.0)
    rows = lax.broadcasted_iota(jnp.int32, (tm, 1), 0)
    h_prev = jnp.where(rows == 0, p_row, pltpu.roll(h, 1, 0))
    h_next = jnp.where(rows == tm - 1, n_row, pltpu.roll(h, tm - 1, 0))
    xx = 0.5 * (h_prev + h_next) - h
    mix = mix_ref[...]
    vec = vec_ref[...]
    ones_bd = ones_ref[...]

    def shifted(j):
        return h + xx * mix[j:j + 1]

    r = _mm(shifted(0), wrkv_ref[0])
    k = _mm(shifted(1), wrkv_ref[1])
    v = _mm(shifted(2), wrkv_ref[2])
    w_in = jnp.tanh(_mm(shifted(3), w1_ref[...]))
    a_in = _mm(shifted(4), a1_ref[...])
    g = _mm(jax.nn.sigmoid(_mm(shifted(5), g1_ref[...])), g2_ref[...])

    kk = k * vec[4:5]
    kk = kk / jnp.maximum(jnp.sqrt(_seg_sum(kk * kk, ones_bd)), 1e-12)
    kd_sum = None
    for d in range(2):
        w_pre = vec[d:d + 1] + _mm(w_in, w2_ref[d])
        lw_ref[d, 0] = -jax.nn.sigmoid(w_pre) * math.exp(-0.5)
        a = jax.nn.sigmoid(vec[2 + d:3 + d] + _mm(a_in, a2_ref[d]))
        kd = k * (1.0 + (a - 1.0) * vec[5:6])
        kd_ref[d, 0] = kd
        ka_ref[d, 0] = kk * a
        kd_sum = kd if kd_sum is None else kd_sum + kd
    r_ref[0] = r
    v_ref[0] = v
    kk_ref[0] = kk
    g_ref[0] = g
    bonus_ref[0] = _seg_sum(r * kd_sum * vec[6:7], ones_bd) * v


def _rwkv_prep(xs, ng, modsel, mix, w_rkv, w0, w1, w2, a0, a1, a2, g1, g2, k_k, k_a, r_k, ones_bd, nct):
    b, t, d = xs.shape
    nt = t // ROW_TILE
    lora = w1.shape[-1]
    prev, nxt = _halo_specs(d, ROW_TILE, t)
    zeros = jnp.zeros((lora, d), F32)
    w2p = jnp.stack([jnp.concatenate([w2[0], zeros], 0), jnp.concatenate([zeros, w2[1]], 0)]).astype(BF16)
    a2p = jnp.stack([jnp.concatenate([a2[0], zeros], 0), jnp.concatenate([zeros, a2[1]], 0)]).astype(BF16)
    w1c = jnp.concatenate([w1[0], w1[1]], axis=1).astype(BF16)
    a1c = jnp.concatenate([a1[0], a1[1]], axis=1).astype(BF16)
    vec = jnp.stack([w0[0], w0[1], a0[0], a0[1], k_k, k_a, r_k.reshape(d), jnp.zeros((d,), F32)])
    shared = jax.ShapeDtypeStruct((b, t, d), F32)
    directional = jax.ShapeDtypeStruct((2, b, t, d), F32)
    dir_spec = pl.BlockSpec((2, 1, ROW_TILE, d), lambda bb, tt: (0, bb, tt, 0))
    body = functools.partial(_rwkv_prep_body, nct=nct, nt=nt)
    return pl.pallas_call(
        body,
        grid=(b, nt),
        in_specs=[_row_spec(d), prev, nxt, _full_spec((1, d)), _mod_spec(d, nct), _full_spec((6, d)),
                  _full_spec((3, d, d)), _full_spec(w1c.shape), _full_spec(a1c.shape), _full_spec(g1.shape),
                  _full_spec(w2p.shape), _full_spec(a2p.shape), _full_spec(g2.shape),
                  _full_spec((SUBLANES, d)), _full_spec((d, d))],
        out_specs=[_row_spec(d)] * 5 + [dir_spec] * 3,
        out_shape=[shared] * 5 + [directional] * 3,
        compiler_params=_params(("parallel", "parallel"), VMEM_LIMIT_BYTES),
        name="rwkv_prep",
    )(xs, xs, xs, ng.reshape(1, d), modsel, mix, w_rkv.astype(BF16), w1c, a1c, g1.astype(BF16), w2p, a2p, g2.astype(BF16), vec,
      ones_bd)


def _rwkv_scan_body(rf_ref, vf_ref, kkf_ref, lwf_ref, kdf_ref, kaf_ref, rb_ref, vb_ref, kkb_ref, lwb_ref, kdb_ref,
                    kab_ref, yf_ref, yb_ref, s_ref, *, chunk, group):
    @pl.when(pl.program_id(1) == 0)
    def _():
        s_ref[...] = jnp.zeros_like(s_ref)

    c, gc, width = chunk, group * chunk, group * RW_HEAD
    pr = lax.broadcasted_iota(jnp.int32, (2 * c, 2 * gc), 0)
    pc = lax.broadcasted_iota(jnp.int32, (2 * c, 2 * gc), 1)
    wr = lax.broadcasted_iota(jnp.int32, (c, gc), 0)
    wc = lax.broadcasted_iota(jnp.int32, (c, gc), 1)
    eye_w = jnp.where(wr == wc % c, 1.0, 0.0)
    tr = lax.broadcasted_iota(jnp.int32, (c, c), 0)
    tc = lax.broadcasted_iota(jnp.int32, (c, c), 1)
    own = (lax.broadcasted_iota(jnp.int32, (gc, width), 0) // c
           == lax.broadcasted_iota(jnp.int32, (gc, width), 1) // RW_HEAD)
    same_head = (lax.broadcasted_iota(jnp.int32, (width, width), 0) // RW_HEAD
                 == lax.broadcasted_iota(jnp.int32, (width, width), 1) // RW_HEAD)

    def stack(x):
        return jnp.where(own, jnp.concatenate([x] * group, axis=0), 0.0).astype(BF16)

    def direction(di, r_ref, v_ref, kk_ref, lw_ref, kd_ref, ka_ref, y_ref):
        sgn = 1 - 2 * di
        gs = range(lw_ref.shape[-1] // width)
        lns = [slice(g * width, (g + 1) * width) for g in gs]
        st = {}

        def prepare():
            tri = jnp.where((tr - tc) * sgn >= 0, 1.0, 0.0).astype(BF16)
            lw_all = lw_ref[0, 0]
            cum_all = _exact_left(tri, lw_all, terms=2)
            tot_all = jnp.sum(lw_all, axis=0, keepdims=True)
            st["left"], st["right"], st["carry_in"], st["v_nat"], st["vm"], st["decay"] = [], [], [], [], [], []
            for ln in lns:
                lw, cum, tot = lw_all[:, ln], cum_all[:, ln], tot_all[:, ln]
                e_neg = jnp.exp(-cum)
                e_out = jnp.exp(tot - cum)
                kk, ka, kd = kk_ref[0, :, ln], ka_ref[0, 0, :, ln], kd_ref[0, 0, :, ln]
                st["left"].append(jnp.concatenate([kk * jnp.exp(cum - lw), r_ref[0, :, ln] * jnp.exp(cum)],
                                                  axis=0).astype(BF16))
                st["right"].append(jnp.concatenate([stack(ka * e_neg), stack(kd * e_neg)], axis=0))
                st["carry_in"].append(jnp.concatenate([ka * e_out, kd * e_out], axis=0).astype(BF16))
                st["v_nat"].append(v_ref[0, :, ln].astype(BF16))
                st["vm"].append(stack(v_ref[0, :, ln]))
                st["decay"].append(jnp.exp(tot))

        def pair_products():
            visible = ((pr % c) - (pc % c)) * sgn > jnp.where(pr < c, 0, -1)
            st["pair"] = [jnp.where(visible, _mm_nt(st["left"][g], st["right"][g]), 0.0) for g in gs]
            st["inv"] = [eye_w - st["pair"][g][0:c, 0:gc] for g in gs]
            st["xp"] = [_mm(st["pair"][g][0:c, 0:gc], stack(st["pair"][g][0:c, 0:gc])) for g in gs]

        def double():
            both = [_mm(jnp.concatenate([st["inv"][g], st["xp"][g]], axis=0), stack(st["xp"][g])) for g in gs]
            st["inv"] = [st["inv"][g] + both[g][0:c] for g in gs]
            st["xp"] = [both[g][c:2 * c] for g in gs]

        def last_double():
            st["inv"] = [st["inv"][g] + _mm(st["inv"][g], stack(st["xp"][g])) for g in gs]

        def residual():
            st["resid"] = [(eye_w - st["inv"][g]) - _mm(st["pair"][g][0:c, 0:gc], stack(st["inv"][g])) for g in gs]

        def newton():
            st["inv"] = [st["inv"][g] + _mm(st["inv"][g], stack(st["resid"][g])) for g in gs]

        def read_state():
            st["s"] = [s_ref[di, g] for g in gs]
            st["from_state"] = [_mm_nt(st["left"][g], st["s"][g]) for g in gs]
            st["from_v"] = [_mm(st["pair"][g][:, gc:2 * gc], st["vm"][g]) for g in gs]

        def solve():
            st["u"] = [-_mm(st["inv"][g], stack(st["from_state"][g][0:c] + st["from_v"][g][0:c])) for g in gs]

        def emit():
            for g in gs:
                y_ref[0, :, lns[g]] = (st["from_state"][g][c:2 * c] + st["from_v"][g][c:2 * c]
                                       + _mm(st["pair"][g][c:2 * c, 0:gc], stack(st["u"][g])))
                grown = _mm_tn(jnp.concatenate([st["u"][g].astype(BF16), st["v_nat"][g]], axis=0),
                               st["carry_in"][g])
                s_ref[di, g] = st["s"][g] * st["decay"][g] + jnp.where(same_head, grown, 0.0)

        return ([prepare, pair_products] + [double] * (int(math.log2(c)) - 3)
                + [last_double, residual, newton, read_state, solve, emit])

    forward = direction(0, rf_ref, vf_ref, kkf_ref, lwf_ref, kdf_ref, kaf_ref, yf_ref)
    backward = direction(1, rb_ref, vb_ref, kkb_ref, lwb_ref, kdb_ref, kab_ref, yb_ref)
    for stage_f, stage_b in zip(forward, backward):
        stage_f()
        stage_b()


def _rwkv_scan(r, v, kk, lw, kd, ka, ctx_len):
    b, t, d = r.shape
    c = RW_CHUNK
    width = RW_GROUP * RW_HEAD
    nc, nck = t // c, ctx_len // c

    def back(i):
        return jnp.where(i < nck, nck - 1 - i, nc - 1 - (i - nck))

    fwd = pl.BlockSpec((1, c, d), lambda bb, i: (bb, i, 0))
    bwd = pl.BlockSpec((1, c, d), lambda bb, i: (bb, back(i), 0))
    fwd_dir = pl.BlockSpec((1, 1, c, d), lambda bb, i: (0, bb, i, 0))
    bwd_dir = pl.BlockSpec((1, 1, c, d), lambda bb, i: (1, bb, back(i), 0))
    body = functools.partial(_rwkv_scan_body, chunk=c, group=RW_GROUP)
    return pl.pallas_call(
        body,
        grid=(b, nc),
        in_specs=[fwd, fwd, fwd, fwd_dir, fwd_dir, fwd_dir, bwd, bwd, bwd, bwd_dir, bwd_dir, bwd_dir],
        out_specs=[fwd, bwd],
        out_shape=[jax.ShapeDtypeStruct((b, t, d), F32)] * 2,
        scratch_shapes=[pltpu.VMEM((2, d // width, width, width), F32)],
        compiler_params=_params(("parallel", "arbitrary"), VMEM_LIMIT_BYTES),
        name="rwkv_scan",
    )(r, v, kk, lw, kd, ka, r, v, kk, lw, kd, ka)


def _rwkv_post_body(yf_ref, yb_ref, bonus_ref, g_ref, x_ref, m_ref, lng_ref, lnb_ref, ones_ref, wout_ref, o_ref):
    ones_bd = ones_ref[...]
    y = yf_ref[0] + yb_ref[0]
    mu = _seg_sum(y, ones_bd) * (1.0 / RW_HEAD)
    yc = y - mu
    var = _seg_sum(yc * yc, ones_bd) * (1.0 / RW_HEAD)
    yn = yc * lax.rsqrt(var + RW_LNX_EPS) * lng_ref[...] + lnb_ref[...]
    out = _mm((yn + bonus_ref[0]) * g_ref[0], wout_ref[...])
    o_ref[0] = x_ref[0] + m_ref[0, 0][2:3] * out


def _rwkv_post(yf, yb, bonus, g, xs, modsel, lnx_g, lnx_b, ones_bd, w_out, nct):
    b, t, d = xs.shape
    return pl.pallas_call(
        _rwkv_post_body,
        grid=(b, t // ROW_TILE),
        in_specs=[_row_spec(d)] * 5 + [_mod_spec(d, nct), _full_spec((1, d)), _full_spec((1, d)),
                                      _full_spec((d, d)), _full_spec((d, d))],
        out_specs=_row_spec(d),
        out_shape=jax.ShapeDtypeStruct((b, t, d), F32),
        compiler_params=_params(("parallel", "parallel"), VMEM_LIMIT_BYTES),
        name="rwkv_post",
    )(yf, yb, bonus, g, xs, modsel, lnx_g.reshape(1, d), lnx_b.reshape(1, d), ones_bd, w_out.astype(BF16))


def _rwkv_layer(xs, n1g, modsel, p, ones_bd, nct, ctx_len):
    r, v, kk, g, bonus, lw, kd, ka = _rwkv_prep(
        xs, n1g, modsel, p["mix"], p["w_rkv"], p["w0"], p["w1"], p["w2"], p["a0"], p["a1"], p["a2"], p["g1"], p["g2"],
        p["k_k"], p["k_a"], p["r_k"], ones_bd, nct)
    yf, yb = _rwkv_scan(r, v, kk, lw, kd, ka, ctx_len)
    return _rwkv_post(yf, yb, bonus, g, xs, modsel, p["lnx_g"], p["lnx_b"], ones_bd, p["w_out"], nct)


def _mlstm_in_body(x_ref, g_ref, m_ref, w_ref, b_ref, q_ref, k_ref, v_ref, o_ref, gf_ref, gb_ref,
                   *, heads, dk, dv):
    m = m_ref[0, 0]
    h = _normmod(x_ref[0], g_ref[...], m[1:2], m[0:1])
    z = _mm(h, w_ref[...]) + b_ref[...]
    nqk = heads * dk
    nv = heads * dv
    q_ref[0] = z[:, 0:nqk]
    k_ref[0] = z[:, nqk:2 * nqk] * (dk ** -0.5)
    v_ref[0] = z[:, 2 * nqk:2 * nqk + nv]
    d_model = o_ref.shape[-1]
    o_ref[0] = jax.nn.sigmoid(z[:, 2 * nqk + nv:2 * nqk + nv + d_model])
    gates = z[:, 2 * nqk + nv + d_model:]
    lane = lax.broadcasted_iota(jnp.int32, (1, LANES), 1)
    is_forget = jnp.logical_and(lane >= heads, lane < 2 * heads)
    for ref, off in ((gf_ref, 0), (gb_ref, LANES)):
        gd = gates[:, off:off + LANES]
        ref[0] = jnp.where(is_forget, jax.nn.log_sigmoid(gd), gd)


def _mlstm_in(xs, g, modsel, w_in, b_in, nct):
    b, t, d = xs.shape
    heads, dk, dv = ML_HEADS, d // (2 * ML_HEADS), d // ML_HEADS
    main = 2 * heads * dk + heads * dv + d
    pad = jnp.zeros((d, LANES - 2 * heads), F32)
    w_cat = jnp.concatenate([w_in[:, :main], w_in[:, main:main + 2 * heads], pad,
                             w_in[:, main + 2 * heads:], pad], axis=1)
    padb = jnp.zeros((LANES - 2 * heads,), F32)
    b_cat = jnp.concatenate([b_in[:main], b_in[main:main + 2 * heads], padb, b_in[main + 2 * heads:], padb])
    n = w_cat.shape[1]
    body = functools.partial(_mlstm_in_body, heads=heads, dk=dk, dv=dv)
    shapes = [(heads * dk), (heads * dk), (heads * dv), d, LANES, LANES]
    return pl.pallas_call(
        body,
        grid=(b, t // ROW_TILE),
        in_specs=[_row_spec(d), _full_spec((1, d)), _mod_spec(d, nct), _full_spec((d, n)), _full_spec((1, n))],
        out_specs=[_row_spec(s) for s in shapes],
        out_shape=[jax.ShapeDtypeStruct((b, t, s), F32) for s in shapes],
        compiler_params=_params(("parallel", "parallel"), VMEM_LIMIT_BYTES),
        name="mlstm_in",
    )(xs, g.reshape(1, d), modsel, w_cat.astype(BF16), b_cat.reshape(1, n))


def _mlstm_chunk_body(qf_ref, kf_ref, vf_ref, gcf_ref, grf_ref, qb_ref, kb_ref, vb_ref, gcb_ref, grb_ref,
                      hf_ref, hb_ref, c_ref, n_ref, m_ref, *, heads, dk, dv):
    @pl.when(pl.program_id(1) == 0)
    def _():
        c_ref[...] = jnp.zeros_like(c_ref)
        n_ref[...] = jnp.zeros_like(n_ref)
        m_ref[...] = jnp.zeros_like(m_ref)

    t = qf_ref.shape[1]
    rr = lax.broadcasted_iota(jnp.int32, (t, t), 0)
    cc = lax.broadcasted_iota(jnp.int32, (t, t), 1)
    chains, seen = [], []
    li_col, li_row, b_col, b_row, b_end, q32, q, k, v = [], [], [], [], [], [], [], [], []
    for di, (q_ref, k_ref, v_ref, gc_ref, gr_ref, h_ref) in enumerate(
            ((qf_ref, kf_ref, vf_ref, gcf_ref, grf_ref, hf_ref), (qb_ref, kb_ref, vb_ref, gcb_ref, grb_ref, hb_ref))):
        sgn = 1 - 2 * di
        vis = (rr - cc) * sgn >= 0
        tri = jnp.where(vis, 1.0, 0.0).astype(BF16)
        gcol = gc_ref[0, 0]
        grow = gr_ref[0, 0]
        bcol = _exact_left(tri, gcol)
        brow = sum(_mm_nt(p, tri) for p in _split3(grow))
        for h in range(heads):
            chains.append((di, h, h_ref))
            seen.append(vis)
            li_col.append(gcol[:, h:h + 1])
            li_row.append(grow[h:h + 1, :])
            b_col.append(bcol[:, heads + h:heads + h + 1])
            b_row.append(brow[heads + h:heads + h + 1, :])
            b_end.append(jnp.sum(gcol[:, heads + h:heads + h + 1], axis=0, keepdims=True))
            q32.append(q_ref[0, :, h * dk:(h + 1) * dk])
            q.append(q32[-1].astype(BF16))
            k.append(k_ref[0, :, h * dk:(h + 1) * dk])
            v.append(v_ref[0, :, h * dv:(h + 1) * dv].astype(BF16))
    cs = range(len(chains))
    c0 = [c_ref[chains[i][0], chains[i][1]] for i in cs]
    n0 = [n_ref[chains[i][0], chains[i][1]] for i in cs]
    m0 = [m_ref[chains[i][0], chains[i][1]][:, 0:1] for i in cs]
    g_end = [b_end[i] - b_col[i] + li_col[i] for i in cs]
    m_loc = [jnp.max(g_end[i], axis=0, keepdims=True) for i in cs]
    kw = [k[i] * jnp.exp(g_end[i] - m_loc[i]) for i in cs]
    c_loc = [_mm_tn(kw[i], v[i]) for i in cs]
    n_loc = [jnp.sum(kw[i], axis=0, keepdims=True) for i in cs]
    qk = [_mm_nt(q[i], k[i]) for i in cs]
    qc = [_mm(q[i], c0[i]) for i in cs]
    d_log = [jnp.where(seen[i], b_col[i] - b_row[i] + li_row[i], -jnp.inf) for i in cs]
    inter = [b_col[i] + m0[i] for i in cs]
    m_t = [jnp.maximum(jnp.max(d_log[i], axis=-1, keepdims=True), inter[i]) for i in cs]
    p = [jnp.exp(d_log[i] - m_t[i]) * qk[i] for i in cs]
    e_inter = [jnp.exp(inter[i] - m_t[i]) for i in cs]
    num = [_mm(p[i], v[i]) + e_inter[i] * qc[i] for i in cs]
    for i in cs:
        di, h, h_ref = chains[i]
        den = (jnp.sum(p[i], axis=-1, keepdims=True)
               + e_inter[i] * jnp.sum(q32[i] * n0[i], axis=-1, keepdims=True))
        h_ref[0, :, h * dv:(h + 1) * dv] = num[i] / jnp.maximum(jnp.abs(den), jnp.exp(-m_t[i]))
        m_new = jnp.maximum(b_end[i] + m0[i], m_loc[i])
        a = jnp.exp(b_end[i] + m0[i] - m_new)
        e = jnp.exp(m_loc[i] - m_new)
        c_ref[di, h] = a * c0[i] + e * c_loc[i]
        n_ref[di, h] = a * n0[i] + e * n_loc[i]
        m_ref[di, h] = jnp.broadcast_to(m_new, (1, LANES))


def _mlstm_chunks(q, k, v, gcol, grow, ctx_len):
    b, t, _ = q.shape
    heads = ML_HEADS
    dk, dv = q.shape[-1] // heads, v.shape[-1] // heads
    c = ML_CHUNK
    nc, nck = t // c, ctx_len // c

    def back(i):
        return jnp.where(i < nck, nck - 1 - i, nc - 1 - (i - nck))

    def specs(chunk_of, di):
        seq = lambda width: pl.BlockSpec((1, c, width), lambda bb, i: (bb, chunk_of(i), 0))
        return [seq(heads * dk), seq(heads * dk), seq(heads * dv),
                pl.BlockSpec((1, 1, c, LANES), lambda bb, i: (di, bb, chunk_of(i), 0)),
                pl.BlockSpec((1, 1, SUBLANES, c), lambda bb, i: (di, bb, 0, chunk_of(i)))]

    body = functools.partial(_mlstm_chunk_body, heads=heads, dk=dk, dv=dv)
    return pl.pallas_call(
        body,
        grid=(b, nc),
        in_specs=specs(lambda i: i, 0) + specs(back, 1),
        out_specs=[pl.BlockSpec((1, c, heads * dv), lambda bb, i: (bb, i, 0)),
                   pl.BlockSpec((1, c, heads * dv), lambda bb, i: (bb, back(i), 0))],
        out_shape=[jax.ShapeDtypeStruct((b, t, heads * dv), F32)] * 2,
        scratch_shapes=[pltpu.VMEM((2, heads, dk, dv), F32), pltpu.VMEM((2, heads, 1, dk), F32),
                        pltpu.VMEM((2, heads, 1, LANES), F32)],
        compiler_params=_params(("parallel", "arbitrary")),
        name="mlstm_chunks",
    )(q, k, v, gcol, grow, q, k, v, gcol, grow)


def _mlstm_post_body(hf_ref, hb_ref, o_ref, x_ref, m_ref, ng_ref, wout_ref, out_ref, *, heads):
    hs = hf_ref[0] + hb_ref[0]
    dv = hs.shape[-1] // heads
    parts = []
    for h in range(heads):
        hh = hs[:, h * dv:(h + 1) * dv]
        parts.append(hh * lax.rsqrt(jnp.mean(hh * hh, axis=-1, keepdims=True) + NORM_EPS))
    hn = jnp.concatenate(parts, axis=-1)
    out = _mm(hn * ng_ref[...] * o_ref[0], wout_ref[...])
    out_ref[0] = x_ref[0] + m_ref[0, 0][2:3] * out


def _mlstm_post(hf, hb, o, xs, modsel, norm_g, w_out, nct):
    b, t, d = xs.shape
    body = functools.partial(_mlstm_post_body, heads=ML_HEADS)
    return pl.pallas_call(
        body,
        grid=(b, t // ROW_TILE),
        in_specs=[_row_spec(d)] * 4 + [_mod_spec(d, nct), _full_spec((1, d)), _full_spec((d, d))],
        out_specs=_row_spec(d),
        out_shape=jax.ShapeDtypeStruct((b, t, d), F32),
        compiler_params=_params(("parallel", "parallel"), VMEM_LIMIT_BYTES),
        name="mlstm_post",
    )(hf, hb, o, xs, modsel, norm_g.reshape(1, d), w_out.astype(BF16))


def _mlstm_layer(xs, n1g, modsel, p, nct, ctx_len):
    q, k, v, o, gf, gb = _mlstm_in(xs, n1g, modsel, p["w_in"], p["b_in"], nct)
    gcol = jnp.stack([gf, gb])
    grow = jnp.swapaxes(gcol[..., :SUBLANES], -1, -2)
    hf, hb = _mlstm_chunks(q, k, v, gcol, grow, ctx_len)
    return _mlstm_post(hf, hb, o, xs, modsel, p["norm_g"], p["w_out"], nct)


def _attn_in_body(x_ref, g_ref, m_ref, w_ref, cos_ref, sin_ref, qt_ref, k_ref, vt_ref, *, d_model):
    m = m_ref[0, 0]
    h = _normmod(x_ref[0], g_ref[...], m[1:2], m[0:1])
    z = _mm(h, w_ref[...])
    reps = d_model // LANES
    cos = jnp.concatenate([cos_ref[...]] * reps, axis=1)
    sin = jnp.concatenate([sin_ref[...]] * reps, axis=1)
    lane = lax.broadcasted_iota(jnp.int32, (1, d_model), 1)
    low = (lane % DA_HEAD) < (DA_HEAD // 2)
    half = DA_HEAD // 2

    def rope(u):
        partner = jnp.where(low, pltpu.roll(u, d_model - half, 1), pltpu.roll(u, half, 1))
        return u * cos + partner * sin

    q = rope(z[:, 0:d_model]) * (DA_HEAD ** -0.5 * math.log2(math.e))
    qt_ref[0] = q.T.astype(BF16)
    k_ref[0] = rope(z[:, d_model:2 * d_model]).astype(BF16)
    vt = z[:, 2 * d_model:].T.astype(BF16)
    dh2 = 2 * DA_HEAD
    ones = jnp.ones((ATT_ONES, vt.shape[1]), BF16)
    for hh in range(d_model // dh2):
        vt_ref[0, 0, hh] = jnp.concatenate([vt[hh * dh2:(hh + 1) * dh2], ones], axis=0)


def _attn_in(xs, g, modsel, w_qkv, cos, sin, nct):
    b, t, d = xs.shape
    nt = t // ROW_TILE
    heads, vrows = d // (2 * DA_HEAD), 2 * DA_HEAD + ATT_ONES
    body = functools.partial(_attn_in_body, d_model=d)
    tab = pl.BlockSpec((ROW_TILE, LANES), lambda bb, tt: (tt, 0))
    return pl.pallas_call(
        body,
        grid=(b, nt),
        in_specs=[_row_spec(d), _full_spec((1, d)), _mod_spec(d, nct), _full_spec((d, 3 * d)), tab, tab],
        out_specs=[pl.BlockSpec((1, d, ROW_TILE), lambda bb, tt: (bb, 0, tt)), _row_spec(d),
                   pl.BlockSpec((1, 1, heads, vrows, ROW_TILE), lambda bb, tt: (bb, tt, 0, 0, 0))],
        out_shape=[jax.ShapeDtypeStruct((b, d, t), BF16), jax.ShapeDtypeStruct((b, t, d), BF16),
                   jax.ShapeDtypeStruct((b, nt, heads, vrows, ROW_TILE), BF16)],
        compiler_params=_params(("parallel", "parallel"), VMEM_LIMIT_BYTES),
        name="attn_in",
    )(xs, g.reshape(1, d), modsel, w_qkv.astype(BF16), cos, sin)


def _attn_body(lam_ref, qt_ref, k_ref, vt_ref, o_ref, acc_ref, sa_ref, sb_ref, sc_ref, *, nct, lambda_init,
               unroll, n_blocks):
    qi = pl.program_id(2)
    qt = qt_ref[0]
    dh2, tq = qt.shape
    tk = vt_ref.shape[-1]
    zero = jnp.zeros((DA_HEAD, tq), qt.dtype)
    qs = jnp.concatenate([jnp.concatenate([qt[0:DA_HEAD], zero], axis=1),
                          jnp.concatenate([zero, qt[DA_HEAD:dh2]], axis=1)], axis=0)

    def scores(j):
        return jnp.dot(k_ref[0, j * tk:(j + 1) * tk, :], qs, preferred_element_type=F32)

    def weigh(j, sj, m_new):
        return jnp.dot(vt_ref[0, j, 0], jnp.exp2(sj - m_new).astype(BF16), preferred_element_type=F32)

    def col_max(s):
        return jnp.max(s, axis=0, keepdims=True)

    @pl.when(qi < nct)
    def _():
        s_ctx = [scores(j) for j in range(nct)]
        top = functools.reduce(jnp.maximum, [col_max(sj) for sj in s_ctx])
        acc_ref[...] = sum(weigh(j, s_ctx[j], top) for j in range(nct))

    @pl.when(qi >= nct)
    def _():
        bufs = (sa_ref, sb_ref, sc_ref)
        blocks = [list(range(nct + unroll))] + [list(range(nct + unroll * b, nct + unroll * (b + 1)))
                                                for b in range(1, n_blocks)]

        def fill(b, u):
            bufs[b % 3][u] = scores(blocks[b][u])

        for u in range(len(blocks[0])):
            fill(0, u)
        top = functools.reduce(jnp.maximum, [col_max(bufs[0][u]) for u in range(len(blocks[0]))])
        if n_blocks > 1:
            for u in range(len(blocks[1])):
                fill(1, u)
        m_run, acc = None, None
        for b in range(n_blocks):
            m_new = top if m_run is None else jnp.maximum(m_run, top)
            if acc is not None:
                acc = acc * jnp.exp2(m_run - m_new)
            next_top = None
            for u in range(len(blocks[b])):
                if b + 2 < n_blocks and u < len(blocks[b + 2]):
                    fill(b + 2, u)
                part = weigh(blocks[b][u], bufs[b % 3][u], m_new)
                acc = part if acc is None else acc + part
                if b + 1 < n_blocks and u < len(blocks[b + 1]):
                    top_u = col_max(bufs[(b + 1) % 3][u])
                    next_top = top_u if next_top is None else jnp.maximum(next_top, top_u)
            m_run, top = m_new, next_top
        acc_ref[...] = acc

    lam = lam_ref[...]
    lam_full = (jnp.exp(jnp.sum(lam[0:1] * lam[1:2], axis=-1, keepdims=True))
                - jnp.exp(jnp.sum(lam[2:3] * lam[3:4], axis=-1, keepdims=True)) + lambda_init)
    acc = acc_ref[...]
    o = acc[0:dh2] / acc[dh2:dh2 + 1]
    o_ref[0] = (o[:, 0:tq] - lam_full * o[:, tq:2 * tq]).T


def _attn(qt, k, vt, lam, lambda_init, nct):
    b, t, d = k.shape
    dh2 = 2 * DA_HEAD
    nt, vrows = vt.shape[1], vt.shape[3]
    n_lat = nt - nct
    unroll = max(u for u in range(1, ATT_UNROLL + 1) if n_lat % u == 0)
    body = functools.partial(_attn_body, nct=nct, lambda_init=lambda_init, unroll=unroll,
                             n_blocks=n_lat // unroll)
    score_buf = pltpu.VMEM((nct + unroll, ROW_TILE, 2 * ROW_TILE), F32)
    return pl.pallas_call(
        body,
        grid=(b, d // dh2, t // ROW_TILE),
        in_specs=[pl.BlockSpec(lam.shape, lambda bb, hh, qq: (0, 0)),
                  pl.BlockSpec((1, dh2, ROW_TILE), lambda bb, hh, qq: (bb, hh, qq)),
                  pl.BlockSpec((1, t, dh2), lambda bb, hh, qq: (bb, 0, hh)),
                  pl.BlockSpec((1, nt, 1, vrows, ROW_TILE), lambda bb, hh, qq: (bb, 0, hh, 0, 0))],
        out_specs=pl.BlockSpec((1, ROW_TILE, dh2), lambda bb, hh, qq: (bb, qq, hh)),
        out_shape=jax.ShapeDtypeStruct((b, t, d), F32),
        scratch_shapes=[pltpu.VMEM((vrows, 2 * ROW_TILE), F32), score_buf, score_buf, score_buf],
        compiler_params=_params(("parallel", "parallel", "parallel"), VMEM_LIMIT_BYTES),
        name="diff_attn",
    )(lam, qt, k, vt)


def _attn_post_body(a_ref, x_ref, m_ref, ng_ref, wout_ref, o_ref, *, heads, scale):
    a = a_ref[0]
    dh2 = a.shape[-1] // heads
    parts = []
    for h in range(heads):
        ah = a[:, h * dh2:(h + 1) * dh2]
        parts.append(ah * lax.rsqrt(jnp.mean(ah * ah, axis=-1, keepdims=True) + DA_EPS))
    an = jnp.concatenate(parts, axis=-1) * ng_ref[...] * scale
    o_ref[0] = x_ref[0] + m_ref[0, 0][2:3] * _mm(an, wout_ref[...])


def _attn_post(a, xs, modsel, norm_g, w_out, lambda_init, nct):
    b, t, d = xs.shape
    body = functools.partial(_attn_post_body, heads=DA_HEADS, scale=1.0 - lambda_init)
    return pl.pallas_call(
        body,
        grid=(b, t // ROW_TILE),
        in_specs=[_row_spec(d), _row_spec(d), _mod_spec(d, nct), _full_spec((1, d)), _full_spec((d, d))],
        out_specs=_row_spec(d),
        out_shape=jax.ShapeDtypeStruct((b, t, d), F32),
        compiler_params=_params(("parallel", "parallel"), VMEM_LIMIT_BYTES),
        name="attn_post",
    )(a, xs, modsel, jnp.tile(norm_g, DA_HEADS).reshape(1, d), w_out.astype(BF16))


def _rope_tables(seq, ctx_len):
    rows = seq // GRID_W
    row = jnp.repeat(jnp.arange(rows), GRID_W).astype(F32)
    col = jnp.tile(jnp.arange(GRID_W), rows).astype(F32)
    nf = DA_HEAD // 4
    inv = jnp.power(ROPE_BASE, -jnp.arange(nf, dtype=F32) / nf)
    ang = jnp.concatenate([row[:, None] * inv, col[:, None] * inv], axis=-1)
    cos, sin = jnp.cos(ang), jnp.sin(ang)
    cos = jnp.concatenate([jnp.ones((ctx_len, DA_HEAD // 2), F32), cos], axis=0)
    sin = jnp.concatenate([jnp.zeros((ctx_len, DA_HEAD // 2), F32), sin], axis=0)
    reps = LANES // DA_HEAD
    return (jnp.tile(jnp.concatenate([cos, cos], axis=-1), (1, reps)),
            jnp.tile(jnp.concatenate([-sin, sin], axis=-1), (1, reps)))


def _attn_layer(xs, n1g, modsel, p, lambda_init, nct, ctx_len):
    cos, sin = _rope_tables(xs.shape[1] - ctx_len, ctx_len)
    qt, k, vt = _attn_in(xs, n1g, modsel, p["w_qkv"], cos, sin, nct)
    a = _attn(qt, k, vt, p["lam"].astype(F32), lambda_init, nct)
    return _attn_post(a, xs, modsel, p["norm_g"], p["w_out"], lambda_init, nct)


def kernel(x, c, ctx, c_ctx, ada_w, ada_b, norm1_g, norm2_g, ffn_w_in, ffn_conv_w, ffn_conv_b, ffn_w_out,
           ra_mix, ra_w_rkv, ra_w0, ra_w1, ra_w2, ra_a0, ra_a1, ra_a2, ra_g1, ra_g2, ra_k_k, ra_k_a, ra_r_k,
           ra_lnx_g, ra_lnx_b, ra_w_out, ml_w_in, ml_b_in, ml_norm_g, ml_w_out, da_w_qkv, da_lambda,
           da_norm_g, da_w_out, final_g):
    batch, seq, d = x.shape
    ctx_len = ctx.shape[1]
    assert ctx_len % ROW_TILE == 0 and seq % ROW_TILE == 0 and batch + 1 <= SUBLANES
    nct = ctx_len // ROW_TILE
    depth = ada_w.shape[0]

    xs = jnp.concatenate([ctx, x], axis=1)
    cc = jnp.concatenate([c_ctx[None], c, jnp.zeros((SUBLANES - 1 - batch, d), F32)], axis=0)
    mod = _ada(cc, ada_w, ada_b).reshape(depth, SUBLANES, 6, d)
    modsel = jnp.stack([jnp.broadcast_to(mod[:, 0:1], (depth, batch, 6, d)), mod[:, 1:1 + batch]], axis=2)

    head_id = jnp.arange(d) // RW_HEAD
    ones_bd = (head_id[:, None] == head_id[None, :]).astype(BF16)

    for i in range(depth):
        kind, j = i % N_MIXERS, i // N_MIXERS
        if kind == 0:
            p = dict(mix=ra_mix[j], w_rkv=ra_w_rkv[j], w0=ra_w0[j], w1=ra_w1[j], w2=ra_w2[j], a0=ra_a0[j],
                     a1=ra_a1[j], a2=ra_a2[j], g1=ra_g1[j], g2=ra_g2[j], k_k=ra_k_k[j], k_a=ra_k_a[j],
                     r_k=ra_r_k[j], lnx_g=ra_lnx_g[j], lnx_b=ra_lnx_b[j], w_out=ra_w_out[j])
            xs = _rwkv_layer(xs, norm1_g[i], modsel[i], p, ones_bd, nct, ctx_len)
        elif kind == 1:
            p = dict(w_in=ml_w_in[j], b_in=ml_b_in[j], norm_g=ml_norm_g[j], w_out=ml_w_out[j])
            xs = _mlstm_layer(xs, norm1_g[i], modsel[i], p, nct, ctx_len)
        else:
            lambda_init = 0.8 - 0.6 * math.exp(-0.3 * i)
            p = dict(w_qkv=da_w_qkv[j], lam=da_lambda[j], norm_g=da_norm_g[j], w_out=da_w_out[j])
            xs = _attn_layer(xs, norm1_g[i], modsel[i], p, lambda_init, nct, ctx_len)
        xs = _ffn_call(xs, norm2_g[i], modsel[i], ffn_w_in[i], ffn_conv_w[i], ffn_conv_b[i], ffn_w_out[i], nct)
    return _final_norm(xs, final_g, nct)
```

```python
import functools
import math

import jax
import jax.numpy as jnp
from jax import lax
from jax.experimental import pallas as pl
from jax.experimental.pallas import tpu as pltpu

F32 = jnp.float32
BF16 = jnp.bfloat16

DEPTH = 4
N_MIXERS = 3
GRID_W = 64
NORM_EPS = 1e-6
RW_HEAD = 64
RW_LNX_EPS = 6.4e-4
ML_HEADS = 4
ML_CHUNK = 128
DA_HEADS = 8
DA_HEAD = 64
DA_EPS = 1e-5
ROPE_BASE = 10000.0

LANES = 128
SUBLANES = 8
VMEM_LIMIT_BYTES = 56 * 1024 * 1024

ROW_TILE = 256
RW_CHUNK = 64
RW_GROUP = 4
FFN_COLS = 256
ATT_UNROLL = 4
ATT_ONES = 16


def _params(sem, vmem=None):
    return pltpu.CompilerParams(dimension_semantics=sem, vmem_limit_bytes=vmem)


def _mm(a, b):
    return jnp.dot(a.astype(BF16), b.astype(BF16), preferred_element_type=F32)


def _mm_nt(a, b):
    return lax.dot_general(a.astype(BF16), b.astype(BF16), (((1,), (1,)), ((), ())),
                           preferred_element_type=F32)


def _mm_tn(a, b):
    return lax.dot_general(a.astype(BF16), b.astype(BF16), (((0,), (0,)), ((), ())),
                           preferred_element_type=F32)


def _split3(x):
    hi = x.astype(BF16)
    r1 = x - hi.astype(F32)
    mid = r1.astype(BF16)
    lo = (r1 - mid.astype(F32)).astype(BF16)
    return hi, mid, lo


def _exact_left(m_bf16, x, terms=3):
    return sum(jnp.dot(m_bf16, p, preferred_element_type=F32) for p in _split3(x)[:terms])


def _normmod(x, g, sc, sh):
    y = x * lax.rsqrt(jnp.mean(x * x, axis=-1, keepdims=True) + NORM_EPS)
    return y * g * (1.0 + sc) + sh


def _ada_body(c_ref, w_ref, b_ref, o_ref):
    a = c_ref[...]
    a = a * jax.nn.sigmoid(a)
    o_ref[0] = jnp.dot(a, w_ref[0], precision=lax.Precision.HIGHEST,
                       preferred_element_type=F32) + b_ref[0]


def _ada(cc, ada_w, ada_b):
    depth, d, n = ada_w.shape
    nb = n // 4
    return pl.pallas_call(
        _ada_body,
        grid=(depth, n // nb),
        in_specs=[pl.BlockSpec((SUBLANES, d), lambda l, j: (0, 0)),
                  pl.BlockSpec((1, d, nb), lambda l, j: (l, 0, j)),
                  pl.BlockSpec((1, 1, nb), lambda l, j: (l, 0, j))],
        out_specs=pl.BlockSpec((1, SUBLANES, nb), lambda l, j: (l, 0, j)),
        out_shape=jax.ShapeDtypeStruct((depth, SUBLANES, n), F32),
        compiler_params=_params(("arbitrary", "arbitrary"), VMEM_LIMIT_BYTES),
        name="ada_mod",
    )(cc, ada_w, ada_b.reshape(depth, 1, n))


def _row_spec(d, tm=ROW_TILE):
    return pl.BlockSpec((1, tm, d), lambda b, t: (b, t, 0))


def _mod_spec(d, nct):
    return pl.BlockSpec((1, 1, 6, d), lambda b, t: (b, jnp.where(t >= nct, 1, 0), 0, 0))


def _full_spec(shape):
    nd = len(shape)
    return pl.BlockSpec(shape, lambda b, t: (0,) * nd)


def _halo_specs(d, tm, n_rows):
    per = tm // SUBLANES
    last = n_rows // SUBLANES - 1
    prev = pl.BlockSpec((1, SUBLANES, d), lambda b, t: (b, jnp.maximum(t * per - 1, 0), 0))
    nxt = pl.BlockSpec((1, SUBLANES, d), lambda b, t: (b, jnp.minimum((t + 1) * per, last), 0))
    return prev, nxt


def _ffn_body(x_ref, xp_ref, xn_ref, g_ref, m_ref, win_ref, cw_ref, cb_ref, wout_ref, fg_ref, o_ref, act_ref,
              *, nct, nt, t0, hidden, final):
    t = pl.program_id(1) + t0
    x = x_ref[0]
    tm = x.shape[0]
    m = m_ref[0, 0]
    xa = jnp.concatenate([xp_ref[0], x, xn_ref[0]], axis=0)
    h32 = _normmod(xa, g_ref[...], m[4:5], m[3:4])
    ha = h32.astype(BF16)
    hm = h32[SUBLANES:SUBLANES + tm].astype(BF16)
    first = jnp.logical_or(t == 0, t == nct)
    last = jnp.logical_or(t == nct - 1, t == nt - 1)
    rows = lax.broadcasted_iota(jnp.int32, (tm + 2 * SUBLANES, 1), 0)
    dead = jnp.logical_or(rows == jnp.where(first, SUBLANES - 1, -1),
                          rows == jnp.where(last, tm + SUBLANES, -1))
    keep = jnp.where(dead, 0.0, 1.0)
    for j in range(hidden // FFN_COLS):
        lo, hi = j * FFN_COLS, (j + 1) * FFN_COLS
        val = jnp.dot(hm, win_ref[:, lo:hi], preferred_element_type=F32)
        gate = jnp.dot(ha, win_ref[:, hidden + lo:hidden + hi], preferred_element_type=F32) * keep
        g_prev = pltpu.roll(gate, 1, 0)[SUBLANES:SUBLANES + tm]
        g_next = pltpu.roll(gate, tm + 2 * SUBLANES - 1, 0)[SUBLANES:SUBLANES + tm]
        g_mid = gate[SUBLANES:SUBLANES + tm]
        cw = cw_ref[:, lo:hi]
        conv = cw[0:1] * g_prev + cw[1:2] * g_mid + cw[2:3] * g_next + cb_ref[:, lo:hi]
        act_ref[:, lo:hi] = (jax.nn.gelu(conv) * val).astype(BF16)
    out = jnp.dot(act_ref[...], wout_ref[...], preferred_element_type=F32)
    y = x + m[5:6] * out
    if final:
        y = y * lax.rsqrt(jnp.mean(y * y, axis=-1, keepdims=True) + NORM_EPS) * fg_ref[...]
    o_ref[0] = y


def _ffn_call(xs, g, modsel, w_in, conv_w, conv_b, w_out, nct, final_g=None):
    b, t, d = xs.shape
    hidden = w_out.shape[0]
    nt = t // ROW_TILE
    final = final_g is not None
    t0 = nct if final else 0
    per = ROW_TILE // SUBLANES
    last = t // SUBLANES - 1
    body = functools.partial(_ffn_body, nct=nct, nt=nt, t0=t0, hidden=hidden, final=final)
    return pl.pallas_call(
        body,
        grid=(b, nt - t0),
        in_specs=[pl.BlockSpec((1, ROW_TILE, d), lambda bb, tt: (bb, tt + t0, 0)),
                  pl.BlockSpec((1, SUBLANES, d), lambda bb, tt: (bb, jnp.maximum((tt + t0) * per - 1, 0), 0)),
                  pl.BlockSpec((1, SUBLANES, d), lambda bb, tt: (bb, jnp.minimum((tt + t0 + 1) * per, last), 0)),
                  _full_spec((1, d)),
                  pl.BlockSpec((1, 1, 6, d), lambda bb, tt: (bb, jnp.where(tt + t0 >= nct, 1, 0), 0, 0)),
                  _full_spec((d, 2 * hidden)), _full_spec((3, hidden)), _full_spec((1, hidden)),
                  _full_spec((hidden, d)), _full_spec((1, d))],
        out_specs=_row_spec(d),
        out_shape=jax.ShapeDtypeStruct((b, t - t0 * ROW_TILE, d), F32),
        scratch_shapes=[pltpu.VMEM((ROW_TILE, hidden), BF16)],
        compiler_params=_params(("parallel", "parallel"), VMEM_LIMIT_BYTES),
        name="conv_ffn",
    )(xs, xs, xs, g.reshape(1, d), modsel, w_in.astype(BF16), conv_w, conv_b.reshape(1, hidden),
      w_out.astype(BF16), (final_g if final else g).reshape(1, d))


def _seg_sum(x, ones_bd):
    return jnp.dot(x.astype(BF16), ones_bd, preferred_element_type=F32)


def _rwkv_prep_body(x_ref, xp_ref, xn_ref, ng_ref, m_ref, mix_ref, wrkv_ref, w1_ref, a1_ref, g1_ref, w2_ref, a2_ref,
                    g2_ref, vec_ref, ones_ref,
                    r_ref, v_ref, kk_ref, g_ref, bonus_ref, lw_ref, kd_ref, ka_ref, *, nct, nt):
    t = pl.program_id(1)
    m = m_ref[0, 0]
    h = _normmod(x_ref[0], ng_ref[...], m[1:2], m[0:1])
    tm = h.shape[0]
    first = jnp.logical_or(t == 0, t == nct)
    last = jnp.logical_or(t == nct - 1, t == nt - 1)
    p_row = _normmod(xp_ref[0], ng_ref[...], m[1:2], m[0:1])[SUBLANES - 1:SUBLANES] * jnp.where(first, 0.0, 1.0)
    n_row = _normmod(xn_ref[0], ng_ref[...], m[1:2], m[0:1])[0:1] * jnp.where(last, 0.0, 1.0)
    rows = lax.broadcasted_iota(jnp.int32, (tm, 1), 0)
    h_prev = jnp.where(rows == 0, p_row, pltpu.roll(h, 1, 0))
    h_next = jnp.where(rows == tm - 1, n_row, pltpu.roll(h, tm - 1, 0))
    xx = 0.5 * (h_prev + h_next) - h
    mix = mix_ref[...]
    vec = vec_ref[...]
    ones_bd = ones_ref[...]

    def shifted(j):
        return h + xx * mix[j:j + 1]

    r = _mm(shifted(0), wrkv_ref[0])
    k = _mm(shifted(1), wrkv_ref[1])
    v = _mm(shifted(2), wrkv_ref[2])
    w_in = jnp.tanh(_mm(shifted(3), w1_ref[...]))
    a_in = _mm(shifted(4), a1_ref[...])
    g = _mm(jax.nn.sigmoid(_mm(shifted(5), g1_ref[...])), g2_ref[...])

    kk = k * vec[4:5]
    kk = kk / jnp.maximum(jnp.sqrt(_seg_sum(kk * kk, ones_bd)), 1e-12)
    kd_sum = None
    for d in range(2):
        w_pre = vec[d:d + 1] + _mm(w_in, w2_ref[d])
        lw_ref[d, 0] = -jax.nn.sigmoid(w_pre) * math.exp(-0.5)
        a = jax.nn.sigmoid(vec[2 + d:3 + d] + _mm(a_in, a2_ref[d]))
        kd = k * (1.0 + (a - 1.0) * vec[5:6])
        kd_ref[d, 0] = kd
        ka_ref[d, 0] = kk * a
        kd_sum = kd if kd_sum is None else kd_sum + kd
    r_ref[0] = r
    v_ref[0] = v
    kk_ref[0] = kk
    g_ref[0] = g
    bonus_ref[0] = _seg_sum(r * kd_sum * vec[6:7], ones_bd) * v


def _rwkv_prep(xs, ng, modsel, mix, w_rkv, w0, w1, w2, a0, a1, a2, g1, g2, k_k, k_a, r_k, ones_bd, nct):
    b, t, d = xs.shape
    nt = t // ROW_TILE
    lora = w1.shape[-1]
    prev, nxt = _halo_specs(d, ROW_TILE, t)
    zeros = jnp.zeros((lora, d), F32)
    w2p = jnp.stack([jnp.concatenate([w2[0], zeros], 0), jnp.concatenate([zeros, w2[1]], 0)]).astype(BF16)
    a2p = jnp.stack([jnp.concatenate([a2[0], zeros], 0), jnp.concatenate([zeros, a2[1]], 0)]).astype(BF16)
    w1c = jnp.concatenate([w1[0], w1[1]], axis=1).astype(BF16)
    a1c = jnp.concatenate([a1[0], a1[1]], axis=1).astype(BF16)
    vec = jnp.stack([w0[0], w0[1], a0[0], a0[1], k_k, k_a, r_k.reshape(d), jnp.zeros((d,), F32)])
    shared = jax.ShapeDtypeStruct((b, t, d), F32)
    directional = jax.ShapeDtypeStruct((2, b, t, d), F32)
    dir_spec = pl.BlockSpec((2, 1, ROW_TILE, d), lambda bb, tt: (0, bb, tt, 0))
    body = functools.partial(_rwkv_prep_body, nct=nct, nt=nt)
    return pl.pallas_call(
        body,
        grid=(b, nt),
        in_specs=[_row_spec(d), prev, nxt, _full_spec((1, d)), _mod_spec(d, nct), _full_spec((6, d)),
                  _full_spec((3, d, d)), _full_spec(w1c.shape), _full_spec(a1c.shape), _full_spec(g1.shape),
                  _full_spec(w2p.shape), _full_spec(a2p.shape), _full_spec(g2.shape),
                  _full_spec((SUBLANES, d)), _full_spec((d, d))],
        out_specs=[_row_spec(d)] * 5 + [dir_spec] * 3,
        out_shape=[shared] * 5 + [directional] * 3,
        compiler_params=_params(("parallel", "parallel"), VMEM_LIMIT_BYTES),
        name="rwkv_prep",
    )(xs, xs, xs, ng.reshape(1, d), modsel, mix, w_rkv.astype(BF16), w1c, a1c, g1.astype(BF16), w2p, a2p, g2.astype(BF16), vec,
      ones_bd)


def _rwkv_scan_body(rf_ref, vf_ref, kkf_ref, lwf_ref, kdf_ref, kaf_ref, rb_ref, vb_ref, kkb_ref, lwb_ref, kdb_ref,
                    kab_ref, yf_ref, yb_ref, s_ref, *, chunk, group):
    @pl.when(pl.program_id(1) == 0)
    def _():
        s_ref[...] = jnp.zeros_like(s_ref)

    c, gc, width = chunk, group * chunk, group * RW_HEAD
    pr = lax.broadcasted_iota(jnp.int32, (2 * c, 2 * gc), 0)
    pc = lax.broadcasted_iota(jnp.int32, (2 * c, 2 * gc), 1)
    wr = lax.broadcasted_iota(jnp.int32, (c, gc), 0)
    wc = lax.broadcasted_iota(jnp.int32, (c, gc), 1)
    eye_w = jnp.where(wr == wc % c, 1.0, 0.0)
    tr = lax.broadcasted_iota(jnp.int32, (c, c), 0)
    tc = lax.broadcasted_iota(jnp.int32, (c, c), 1)
    own = (lax.broadcasted_iota(jnp.int32, (gc, width), 0) // c
           == lax.broadcasted_iota(jnp.int32, (gc, width), 1) // RW_HEAD)
    same_head = (lax.broadcasted_iota(jnp.int32, (width, width), 0) // RW_HEAD
                 == lax.broadcasted_iota(jnp.int32, (width, width), 1) // RW_HEAD)

    def stack(x):
        return jnp.where(own, jnp.concatenate([x] * group, axis=0), 0.0).astype(BF16)

    def direction(di, r_ref, v_ref, kk_ref, lw_ref, kd_ref, ka_ref, y_ref):
        sgn = 1 - 2 * di
        gs = range(lw_ref.shape[-1] // width)
        lns = [slice(g * width, (g + 1) * width) for g in gs]
        st = {}

        def prepare():
            tri = jnp.where((tr - tc) * sgn >= 0, 1.0, 0.0).astype(BF16)
            lw_all = lw_ref[0, 0]
            cum_all = _exact_left(tri, lw_all, terms=2)
            tot_all = jnp.sum(lw_all, axis=0, keepdims=True)
            st["left"], st["right"], st["carry_in"], st["v_nat"], st["vm"], st["decay"] = [], [], [], [], [], []
            for ln in lns:
                lw, cum, tot = lw_all[:, ln], cum_all[:, ln], tot_all[:, ln]
                e_neg = jnp.exp(-cum)
                e_out = jnp.exp(tot - cum)
                kk, ka, kd = kk_ref[0, :, ln], ka_ref[0, 0, :, ln], kd_ref[0, 0, :, ln]
                st["left"].append(jnp.concatenate([kk * jnp.exp(cum - lw), r_ref[0, :, ln] * jnp.exp(cum)],
                                                  axis=0).astype(BF16))
                st["right"].append(jnp.concatenate([stack(ka * e_neg), stack(kd * e_neg)], axis=0))
                st["carry_in"].append(jnp.concatenate([ka * e_out, kd * e_out], axis=0).astype(BF16))
                st["v_nat"].append(v_ref[0, :, ln].astype(BF16))
                st["vm"].append(stack(v_ref[0, :, ln]))
                st["decay"].append(jnp.exp(tot))

        def pair_products():
            visible = ((pr % c) - (pc % c)) * sgn > jnp.where(pr < c, 0, -1)
            st["pair"] = [jnp.where(visible, _mm_nt(st["left"][g], st["right"][g]), 0.0) for g in gs]
            st["inv"] = [eye_w - st["pair"][g][0:c, 0:gc] for g in gs]
            st["xp"] = [_mm(st["pair"][g][0:c, 0:gc], stack(st["pair"][g][0:c, 0:gc])) for g in gs]

        def double():
            both = [_mm(jnp.concatenate([st["inv"][g], st["xp"][g]], axis=0), stack(st["xp"][g])) for g in gs]
            st["inv"] = [st["inv"][g] + both[g][0:c] for g in gs]
            st["xp"] = [both[g][c:2 * c] for g in gs]

        def last_double():
            st["inv"] = [st["inv"][g] + _mm(st["inv"][g], stack(st["xp"][g])) for g in gs]

        def residual():
            st["resid"] = [(eye_w - st["inv"][g]) - _mm(st["pair"][g][0:c, 0:gc], stack(st["inv"][g])) for g in gs]

        def newton():
            st["inv"] = [st["inv"][g] + _mm(st["inv"][g], stack(st["resid"][g])) for g in gs]

        def read_state():
            st["s"] = [s_ref[di, g] for g in gs]
            st["from_state"] = [_mm_nt(st["left"][g], st["s"][g]) for g in gs]
            st["from_v"] = [_mm(st["pair"][g][:, gc:2 * gc], st["vm"][g]) for g in gs]

        def solve():
            st["u"] = [-_mm(st["inv"][g], stack(st["from_state"][g][0:c] + st["from_v"][g][0:c])) for g in gs]

        def emit():
            for g in gs:
                y_ref[0, :, lns[g]] = (st["from_state"][g][c:2 * c] + st["from_v"][g][c:2 * c]
                                       + _mm(st["pair"][g][c:2 * c, 0:gc], stack(st["u"][g])))
                grown = _mm_tn(jnp.concatenate([st["u"][g].astype(BF16), st["v_nat"][g]], axis=0),
                               st["carry_in"][g])
                s_ref[di, g] = st["s"][g] * st["decay"][g] + jnp.where(same_head, grown, 0.0)

        return ([prepare, pair_products] + [double] * (int(math.log2(c)) - 3)
                + [last_double, residual, newton, read_state, solve, emit])

    forward = direction(0, rf_ref, vf_ref, kkf_ref, lwf_ref, kdf_ref, kaf_ref, yf_ref)
    backward = direction(1, rb_ref, vb_ref, kkb_ref, lwb_ref, kdb_ref, kab_ref, yb_ref)
    for stage_f, stage_b in zip(forward, backward):
        stage_f()
        stage_b()


def _rwkv_scan(r, v, kk, lw, kd, ka, ctx_len):
    b, t, d = r.shape
    c = RW_CHUNK
    width = RW_GROUP * RW_HEAD
    nc, nck = t // c, ctx_len // c

    def back(i):
        return jnp.where(i < nck, nck - 1 - i, nc - 1 - (i - nck))

    fwd = pl.BlockSpec((1, c, d), lambda bb, i: (bb, i, 0))
    bwd = pl.BlockSpec((1, c, d), lambda bb, i: (bb, back(i), 0))
    fwd_dir = pl.BlockSpec((1, 1, c, d), lambda bb, i: (0, bb, i, 0))
    bwd_dir = pl.BlockSpec((1, 1, c, d), lambda bb, i: (1, bb, back(i), 0))
    body = functools.partial(_rwkv_scan_body, chunk=c, group=RW_GROUP)
    return pl.pallas_call(
        body,
        grid=(b, nc),
        in_specs=[fwd, fwd, fwd, fwd_dir, fwd_dir, fwd_dir, bwd, bwd, bwd, bwd_dir, bwd_dir, bwd_dir],
        out_specs=[fwd, bwd],
        out_shape=[jax.ShapeDtypeStruct((b, t, d), F32)] * 2,
        scratch_shapes=[pltpu.VMEM((2, d // width, width, width), F32)],
        compiler_params=_params(("parallel", "arbitrary"), VMEM_LIMIT_BYTES),
        name="rwkv_scan",
    )(r, v, kk, lw, kd, ka, r, v, kk, lw, kd, ka)


def _rwkv_post_body(yf_ref, yb_ref, bonus_ref, g_ref, x_ref, m_ref, lng_ref, lnb_ref, ones_ref, wout_ref, o_ref):
    ones_bd = ones_ref[...]
    y = yf_ref[0] + yb_ref[0]
    mu = _seg_sum(y, ones_bd) * (1.0 / RW_HEAD)
    yc = y - mu
    var = _seg_sum(yc * yc, ones_bd) * (1.0 / RW_HEAD)
    yn = yc * lax.rsqrt(var + RW_LNX_EPS) * lng_ref[...] + lnb_ref[...]
    out = _mm((yn + bonus_ref[0]) * g_ref[0], wout_ref[...])
    o_ref[0] = x_ref[0] + m_ref[0, 0][2:3] * out


def _rwkv_post(yf, yb, bonus, g, xs, modsel, lnx_g, lnx_b, ones_bd, w_out, nct):
    b, t, d = xs.shape
    return pl.pallas_call(
        _rwkv_post_body,
        grid=(b, t // ROW_TILE),
        in_specs=[_row_spec(d)] * 5 + [_mod_spec(d, nct), _full_spec((1, d)), _full_spec((1, d)),
                                      _full_spec((d, d)), _full_spec((d, d))],
        out_specs=_row_spec(d),
        out_shape=jax.ShapeDtypeStruct((b, t, d), F32),
        compiler_params=_params(("parallel", "parallel"), VMEM_LIMIT_BYTES),
        name="rwkv_post",
    )(yf, yb, bonus, g, xs, modsel, lnx_g.reshape(1, d), lnx_b.reshape(1, d), ones_bd, w_out.astype(BF16))


def _rwkv_layer(xs, n1g, modsel, p, ones_bd, nct, ctx_len):
    r, v, kk, g, bonus, lw, kd, ka = _rwkv_prep(
        xs, n1g, modsel, p["mix"], p["w_rkv"], p["w0"], p["w1"], p["w2"], p["a0"], p["a1"], p["a2"], p["g1"], p["g2"],
        p["k_k"], p["k_a"], p["r_k"], ones_bd, nct)
    yf, yb = _rwkv_scan(r, v, kk, lw, kd, ka, ctx_len)
    return _rwkv_post(yf, yb, bonus, g, xs, modsel, p["lnx_g"], p["lnx_b"], ones_bd, p["w_out"], nct)


def _mlstm_in_body(x_ref, g_ref, m_ref, w_ref, b_ref, q_ref, k_ref, v_ref, o_ref, gf_ref, gb_ref,
                   *, heads, dk, dv):
    m = m_ref[0, 0]
    h = _normmod(x_ref[0], g_ref[...], m[1:2], m[0:1])
    z = _mm(h, w_ref[...]) + b_ref[...]
    nqk = heads * dk
    nv = heads * dv
    q_ref[0] = z[:, 0:nqk]
    k_ref[0] = z[:, nqk:2 * nqk] * (dk ** -0.5)
    v_ref[0] = z[:, 2 * nqk:2 * nqk + nv]
    d_model = o_ref.shape[-1]
    o_ref[0] = jax.nn.sigmoid(z[:, 2 * nqk + nv:2 * nqk + nv + d_model])
    gates = z[:, 2 * nqk + nv + d_model:]
    lane = lax.broadcasted_iota(jnp.int32, (1, LANES), 1)
    is_forget = jnp.logical_and(lane >= heads, lane < 2 * heads)
    for ref, off in ((gf_ref, 0), (gb_ref, LANES)):
        gd = gates[:, off:off + LANES]
        ref[0] = jnp.where(is_forget, jax.nn.log_sigmoid(gd), gd)


def _mlstm_in(xs, g, modsel, w_in, b_in, nct):
    b, t, d = xs.shape
    heads, dk, dv = ML_HEADS, d // (2 * ML_HEADS), d // ML_HEADS
    main = 2 * heads * dk + heads * dv + d
    pad = jnp.zeros((d, LANES - 2 * heads), F32)
    w_cat = jnp.concatenate([w_in[:, :main], w_in[:, main:main + 2 * heads], pad,
                             w_in[:, main + 2 * heads:], pad], axis=1)
    padb = jnp.zeros((LANES - 2 * heads,), F32)
    b_cat = jnp.concatenate([b_in[:main], b_in[main:main + 2 * heads], padb, b_in[main + 2 * heads:], padb])
    n = w_cat.shape[1]
    body = functools.partial(_mlstm_in_body, heads=heads, dk=dk, dv=dv)
    shapes = [(heads * dk), (heads * dk), (heads * dv), d, LANES, LANES]
    return pl.pallas_call(
        body,
        grid=(b, t // ROW_TILE),
        in_specs=[_row_spec(d), _full_spec((1, d)), _mod_spec(d, nct), _full_spec((d, n)), _full_spec((1, n))],
        out_specs=[_row_spec(s) for s in shapes],
        out_shape=[jax.ShapeDtypeStruct((b, t, s), F32) for s in shapes],
        compiler_params=_params(("parallel", "parallel"), VMEM_LIMIT_BYTES),
        name="mlstm_in",
    )(xs, g.reshape(1, d), modsel, w_cat.astype(BF16), b_cat.reshape(1, n))


def _mlstm_chunk_body(qf_ref, kf_ref, vf_ref, gcf_ref, grf_ref, qb_ref, kb_ref, vb_ref, gcb_ref, grb_ref,
                      hf_ref, hb_ref, c_ref, n_ref, m_ref, *, heads, dk, dv):
    @pl.when(pl.program_id(1) == 0)
    def _():
        c_ref[...] = jnp.zeros_like(c_ref)
        n_ref[...] = jnp.zeros_like(n_ref)
        m_ref[...] = jnp.zeros_like(m_ref)

    t = qf_ref.shape[1]
    rr = lax.broadcasted_iota(jnp.int32, (t, t), 0)
    cc = lax.broadcasted_iota(jnp.int32, (t, t), 1)
    chains, seen = [], []
    li_col, li_row, b_col, b_row, b_end, q32, q, k, v = [], [], [], [], [], [], [], [], []
    for di, (q_ref, k_ref, v_ref, gc_ref, gr_ref, h_ref) in enumerate(
            ((qf_ref, kf_ref, vf_ref, gcf_ref, grf_ref, hf_ref), (qb_ref, kb_ref, vb_ref, gcb_ref, grb_ref, hb_ref))):
        sgn = 1 - 2 * di
        vis = (rr - cc) * sgn >= 0
        tri = jnp.where(vis, 1.0, 0.0).astype(BF16)
        gcol = gc_ref[0, 0]
        grow = gr_ref[0, 0]
        bcol = _exact_left(tri, gcol)
        brow = sum(_mm_nt(p, tri) for p in _split3(grow))
        for h in range(heads):
            chains.append((di, h, h_ref))
            seen.append(vis)
            li_col.append(gcol[:, h:h + 1])
            li_row.append(grow[h:h + 1, :])
            b_col.append(bcol[:, heads + h:heads + h + 1])
            b_row.append(brow[heads + h:heads + h + 1, :])
            b_end.append(jnp.sum(gcol[:, heads + h:heads + h + 1], axis=0, keepdims=True))
            q32.append(q_ref[0, :, h * dk:(h + 1) * dk])
            q.append(q32[-1].astype(BF16))
            k.append(k_ref[0, :, h * dk:(h + 1) * dk])
            v.append(v_ref[0, :, h * dv:(h + 1) * dv].astype(BF16))
    cs = range(len(chains))
    c0 = [c_ref[chains[i][0], chains[i][1]] for i in cs]
    n0 = [n_ref[chains[i][0], chains[i][1]] for i in cs]
    m0 = [m_ref[chains[i][0], chains[i][1]][:, 0:1] for i in cs]
    g_end = [b_end[i] - b_col[i] + li_col[i] for i in cs]
    m_loc = [jnp.max(g_end[i], axis=0, keepdims=True) for i in cs]
    kw = [k[i] * jnp.exp(g_end[i] - m_loc[i]) for i in cs]
    c_loc = [_mm_tn(kw[i], v[i]) for i in cs]
    n_loc = [jnp.sum(kw[i], axis=0, keepdims=True) for i in cs]
    qk = [_mm_nt(q[i], k[i]) for i in cs]
    qc = [_mm(q[i], c0[i]) for i in cs]
    d_log = [jnp.where(seen[i], b_col[i] - b_row[i] + li_row[i], -jnp.inf) for i in cs]
    inter = [b_col[i] + m0[i] for i in cs]
    m_t = [jnp.maximum(jnp.max(d_log[i], axis=-1, keepdims=True), inter[i]) for i in cs]
    p = [jnp.exp(d_log[i] - m_t[i]) * qk[i] for i in cs]
    e_inter = [jnp.exp(inter[i] - m_t[i]) for i in cs]
    num = [_mm(p[i], v[i]) + e_inter[i] * qc[i] for i in cs]
    for i in cs:
        di, h, h_ref = chains[i]
        den = (jnp.sum(p[i], axis=-1, keepdims=True)
               + e_inter[i] * jnp.sum(q32[i] * n0[i], axis=-1, keepdims=True))
        h_ref[0, :, h * dv:(h + 1) * dv] = num[i] / jnp.maximum(jnp.abs(den), jnp.exp(-m_t[i]))
        m_new = jnp.maximum(b_end[i] + m0[i], m_loc[i])
        a = jnp.exp(b_end[i] + m0[i] - m_new)
        e = jnp.exp(m_loc[i] - m_new)
        c_ref[di, h] = a * c0[i] + e * c_loc[i]
        n_ref[di, h] = a * n0[i] + e * n_loc[i]
        m_ref[di, h] = jnp.broadcast_to(m_new, (1, LANES))


def _mlstm_chunks(q, k, v, gcol, grow, ctx_len):
    b, t, _ = q.shape
    heads = ML_HEADS
    dk, dv = q.shape[-1] // heads, v.shape[-1] // heads
    c = ML_CHUNK
    nc, nck = t // c, ctx_len // c

    def back(i):
        return jnp.where(i < nck, nck - 1 - i, nc - 1 - (i - nck))

    def specs(chunk_of, di):
        seq = lambda width: pl.BlockSpec((1, c, width), lambda bb, i: (bb, chunk_of(i), 0))
        return [seq(heads * dk), seq(heads * dk), seq(heads * dv),
                pl.BlockSpec((1, 1, c, LANES), lambda bb, i: (di, bb, chunk_of(i), 0)),
                pl.BlockSpec((1, 1, SUBLANES, c), lambda bb, i: (di, bb, 0, chunk_of(i)))]

    body = functools.partial(_mlstm_chunk_body, heads=heads, dk=dk, dv=dv)
    return pl.pallas_call(
        body,
        grid=(b, nc),
        in_specs=specs(lambda i: i, 0) + specs(back, 1),
        out_specs=[pl.BlockSpec((1, c, heads * dv), lambda bb, i: (bb, i, 0)),
                   pl.BlockSpec((1, c, heads * dv), lambda bb, i: (bb, back(i), 0))],
        out_shape=[jax.ShapeDtypeStruct((b, t, heads * dv), F32)] * 2,
        scratch_shapes=[pltpu.VMEM((2, heads, dk, dv), F32), pltpu.VMEM((2, heads, 1, dk), F32),
                        pltpu.VMEM((2, heads, 1, LANES), F32)],
        compiler_params=_params(("parallel", "arbitrary")),
        name="mlstm_chunks",
    )(q, k, v, gcol, grow, q, k, v, gcol, grow)


def _mlstm_post_body(hf_ref, hb_ref, o_ref, x_ref, m_ref, ng_ref, wout_ref, out_ref, *, heads):
    hs = hf_ref[0] + hb_ref[0]
    dv = hs.shape[-1] // heads
    parts = []
    for h in range(heads):
        hh = hs[:, h * dv:(h + 1) * dv]
        parts.append(hh * lax.rsqrt(jnp.mean(hh * hh, axis=-1, keepdims=True) + NORM_EPS))
    hn = jnp.concatenate(parts, axis=-1)
    out = _mm(hn * ng_ref[...] * o_ref[0], wout_ref[...])
    out_ref[0] = x_ref[0] + m_ref[0, 0][2:3] * out


def _mlstm_post(hf, hb, o, xs, modsel, norm_g, w_out, nct):
    b, t, d = xs.shape
    body = functools.partial(_mlstm_post_body, heads=ML_HEADS)
    return pl.pallas_call(
        body,
        grid=(b, t // ROW_TILE),
        in_specs=[_row_spec(d)] * 4 + [_mod_spec(d, nct), _full_spec((1, d)), _full_spec((d, d))],
        out_specs=_row_spec(d),
        out_shape=jax.ShapeDtypeStruct((b, t, d), F32),
        compiler_params=_params(("parallel", "parallel"), VMEM_LIMIT_BYTES),
        name="mlstm_post",
    )(hf, hb, o, xs, modsel, norm_g.reshape(1, d), w_out.astype(BF16))


def _mlstm_layer(xs, n1g, modsel, p, nct, ctx_len):
    q, k, v, o, gf, gb = _mlstm_in(xs, n1g, modsel, p["w_in"], p["b_in"], nct)
    gcol = jnp.stack([gf, gb])
    grow = jnp.swapaxes(gcol[..., :SUBLANES], -1, -2)
    hf, hb = _mlstm_chunks(q, k, v, gcol, grow, ctx_len)
    return _mlstm_post(hf, hb, o, xs, modsel, p["norm_g"], p["w_out"], nct)


def _attn_in_body(x_ref, g_ref, m_ref, w_ref, cos_ref, sin_ref, qt_ref, k_ref, vt_ref, *, d_model):
    m = m_ref[0, 0]
    h = _normmod(x_ref[0], g_ref[...], m[1:2], m[0:1])
    z = _mm(h, w_ref[...])
    reps = d_model // LANES
    cos = jnp.concatenate([cos_ref[...]] * reps, axis=1)
    sin = jnp.concatenate([sin_ref[...]] * reps, axis=1)
    lane = lax.broadcasted_iota(jnp.int32, (1, d_model), 1)
    low = (lane % DA_HEAD) < (DA_HEAD // 2)
    half = DA_HEAD // 2

    def rope(u):
        partner = jnp.where(low, pltpu.roll(u, d_model - half, 1), pltpu.roll(u, half, 1))
        return u * cos + partner * sin

    q = rope(z[:, 0:d_model]) * (DA_HEAD ** -0.5 * math.log2(math.e))
    qt_ref[0] = q.T.astype(BF16)
    k_ref[0] = rope(z[:, d_model:2 * d_model]).astype(BF16)
    vt = z[:, 2 * d_model:].T.astype(BF16)
    dh2 = 2 * DA_HEAD
    ones = jnp.ones((ATT_ONES, vt.shape[1]), BF16)
    for hh in range(d_model // dh2):
        vt_ref[0, 0, hh] = jnp.concatenate([vt[hh * dh2:(hh + 1) * dh2], ones], axis=0)


def _attn_in(xs, g, modsel, w_qkv, cos, sin, nct):
    b, t, d = xs.shape
    nt = t // ROW_TILE
    heads, vrows = d // (2 * DA_HEAD), 2 * DA_HEAD + ATT_ONES
    body = functools.partial(_attn_in_body, d_model=d)
    tab = pl.BlockSpec((ROW_TILE, LANES), lambda bb, tt: (tt, 0))
    return pl.pallas_call(
        body,
        grid=(b, nt),
        in_specs=[_row_spec(d), _full_spec((1, d)), _mod_spec(d, nct), _full_spec((d, 3 * d)), tab, tab],
        out_specs=[pl.BlockSpec((1, d, ROW_TILE), lambda bb, tt: (bb, 0, tt)), _row_spec(d),
                   pl.BlockSpec((1, 1, heads, vrows, ROW_TILE), lambda bb, tt: (bb, tt, 0, 0, 0))],
        out_shape=[jax.ShapeDtypeStruct((b, d, t), BF16), jax.ShapeDtypeStruct((b, t, d), BF16),
                   jax.ShapeDtypeStruct((b, nt, heads, vrows, ROW_TILE), BF16)],
        compiler_params=_params(("parallel", "parallel"), VMEM_LIMIT_BYTES),
        name="attn_in",
    )(xs, g.reshape(1, d), modsel, w_qkv.astype(BF16), cos, sin)


def _attn_body(lam_ref, qt_ref, k_ref, vt_ref, o_ref, acc_ref, sa_ref, sb_ref, sc_ref, *, nct, lambda_init,
               unroll, n_blocks):
    qi = pl.program_id(2)
    qt = qt_ref[0]
    dh2, tq = qt.shape
    tk = vt_ref.shape[-1]
    zero = jnp.zeros((DA_HEAD, tq), qt.dtype)
    qs = jnp.concatenate([jnp.concatenate([qt[0:DA_HEAD], zero], axis=1),
                          jnp.concatenate([zero, qt[DA_HEAD:dh2]], axis=1)], axis=0)

    def scores(j):
        return jnp.dot(k_ref[0, j * tk:(j + 1) * tk, :], qs, preferred_element_type=F32)

    def weigh(j, sj, m_new):
        return jnp.dot(vt_ref[0, j, 0], jnp.exp2(sj - m_new).astype(BF16), preferred_element_type=F32)

    def col_max(s):
        return jnp.max(s, axis=0, keepdims=True)

    @pl.when(qi < nct)
    def _():
        s_ctx = [scores(j) for j in range(nct)]
        top = functools.reduce(jnp.maximum, [col_max(sj) for sj in s_ctx])
        acc_ref[...] = sum(weigh(j, s_ctx[j], top) for j in range(nct))

    @pl.when(qi >= nct)
    def _():
        bufs = (sa_ref, sb_ref, sc_ref)
        blocks = [list(range(nct + unroll))] + [list(range(nct + unroll * b, nct + unroll * (b + 1)))
                                                for b in range(1, n_blocks)]

        def fill(b, u):
            bufs[b % 3][u] = scores(blocks[b][u])

        for u in range(len(blocks[0])):
            fill(0, u)
        top = functools.reduce(jnp.maximum, [col_max(bufs[0][u]) for u in range(len(blocks[0]))])
        if n_blocks > 1:
            for u in range(len(blocks[1])):
                fill(1, u)
        m_run, acc = None, None
        for b in range(n_blocks):
            m_new = top if m_run is None else jnp.maximum(m_run, top)
            if acc is not None:
                acc = acc * jnp.exp2(m_run - m_new)
            next_top = None
            for u in range(len(blocks[b])):
                if b + 2 < n_blocks and u < len(blocks[b + 2]):
                    fill(b + 2, u)
                part = weigh(blocks[b][u], bufs[b % 3][u], m_new)
                acc = part if acc is None else acc + part
                if b + 1 < n_blocks and u < len(blocks[b + 1]):
                    top_u = col_max(bufs[(b + 1) % 3][u])
                    next_top = top_u if next_top is None else jnp.maximum(next_top, top_u)
            m_run, top = m_new, next_top
        acc_ref[...] = acc

    lam = lam_ref[...]
    lam_full = (jnp.exp(jnp.sum(lam[0:1] * lam[1:2], axis=-1, keepdims=True))
                - jnp.exp(jnp.sum(lam[2:3] * lam[3:4], axis=-1, keepdims=True)) + lambda_init)
    acc = acc_ref[...]
    o = acc[0:dh2] / acc[dh2:dh2 + 1]
    o_ref[0] = (o[:, 0:tq] - lam_full * o[:, tq:2 * tq]).T


def _attn(qt, k, vt, lam, lambda_init, nct):
    b, t, d = k.shape
    dh2 = 2 * DA_HEAD
    nt, vrows = vt.shape[1], vt.shape[3]
    n_lat = nt - nct
    unroll = max(u for u in range(1, ATT_UNROLL + 1) if n_lat % u == 0)
    body = functools.partial(_attn_body, nct=nct, lambda_init=lambda_init, unroll=unroll,
                             n_blocks=n_lat // unroll)
    score_buf = pltpu.VMEM((nct + unroll, ROW_TILE, 2 * ROW_TILE), F32)
    return pl.pallas_call(
        body,
        grid=(b, d // dh2, t // ROW_TILE),
        in_specs=[pl.BlockSpec(lam.shape, lambda bb, hh, qq: (0, 0)),
                  pl.BlockSpec((1, dh2, ROW_TILE), lambda bb, hh, qq: (bb, hh, qq)),
                  pl.BlockSpec((1, t, dh2), lambda bb, hh, qq: (bb, 0, hh)),
                  pl.BlockSpec((1, nt, 1, vrows, ROW_TILE), lambda bb, hh, qq: (bb, 0, hh, 0, 0))],
        out_specs=pl.BlockSpec((1, ROW_TILE, dh2), lambda bb, hh, qq: (bb, qq, hh)),
        out_shape=jax.ShapeDtypeStruct((b, t, d), F32),
        scratch_shapes=[pltpu.VMEM((vrows, 2 * ROW_TILE), F32), score_buf, score_buf, score_buf],
        compiler_params=_params(("parallel", "parallel", "parallel"), VMEM_LIMIT_BYTES),
        name="diff_attn",
    )(lam, qt, k, vt)


def _attn_post_body(a_ref, x_ref, m_ref, ng_ref, wout_ref, o_ref, *, heads, scale):
    a = a_ref[0]
    dh2 = a.shape[-1] // heads
    parts = []
    for h in range(heads):
        ah = a[:, h * dh2:(h + 1) * dh2]
        parts.append(ah * lax.rsqrt(jnp.mean(ah * ah, axis=-1, keepdims=True) + DA_EPS))
    an = jnp.concatenate(parts, axis=-1) * ng_ref[...] * scale
    o_ref[0] = x_ref[0] + m_ref[0, 0][2:3] * _mm(an, wout_ref[...])


def _attn_post(a, xs, modsel, norm_g, w_out, lambda_init, nct):
    b, t, d = xs.shape
    body = functools.partial(_attn_post_body, heads=DA_HEADS, scale=1.0 - lambda_init)
    return pl.pallas_call(
        body,
        grid=(b, t // ROW_TILE),
        in_specs=[_row_spec(d), _row_spec(d), _mod_spec(d, nct), _full_spec((1, d)), _full_spec((d, d))],
        out_specs=_row_spec(d),
        out_shape=jax.ShapeDtypeStruct((b, t, d), F32),
        compiler_params=_params(("parallel", "parallel"), VMEM_LIMIT_BYTES),
        name="attn_post",
    )(a, xs, modsel, jnp.tile(norm_g, DA_HEADS).reshape(1, d), w_out.astype(BF16))


def _rope_tables(seq, ctx_len):
    rows = seq // GRID_W
    row = jnp.repeat(jnp.arange(rows), GRID_W).astype(F32)
    col = jnp.tile(jnp.arange(GRID_W), rows).astype(F32)
    nf = DA_HEAD // 4
    inv = jnp.power(ROPE_BASE, -jnp.arange(nf, dtype=F32) / nf)
    ang = jnp.concatenate([row[:, None] * inv, col[:, None] * inv], axis=-1)
    cos, sin = jnp.cos(ang), jnp.sin(ang)
    cos = jnp.concatenate([jnp.ones((ctx_len, DA_HEAD // 2), F32), cos], axis=0)
    sin = jnp.concatenate([jnp.zeros((ctx_len, DA_HEAD // 2), F32), sin], axis=0)
    reps = LANES // DA_HEAD
    return (jnp.tile(jnp.concatenate([cos, cos], axis=-1), (1, reps)),
            jnp.tile(jnp.concatenate([-sin, sin], axis=-1), (1, reps)))


def _attn_layer(xs, n1g, modsel, p, lambda_init, nct, ctx_len):
    cos, sin = _rope_tables(xs.shape[1] - ctx_len, ctx_len)
    qt, k, vt = _attn_in(xs, n1g, modsel, p["w_qkv"], cos, sin, nct)
    a = _attn(qt, k, vt, p["lam"].astype(F32), lambda_init, nct)
    return _attn_post(a, xs, modsel, p["norm_g"], p["w_out"], lambda_init, nct)


def kernel(x, c, ctx, c_ctx, ada_w, ada_b, norm1_g, norm2_g, ffn_w_in, ffn_conv_w, ffn_conv_b, ffn_w_out,
           ra_mix, ra_w_rkv, ra_w0, ra_w1, ra_w2, ra_a0, ra_a1, ra_a2, ra_g1, ra_g2, ra_k_k, ra_k_a, ra_r_k,
           ra_lnx_g, ra_lnx_b, ra_w_out, ml_w_in, ml_b_in, ml_norm_g, ml_w_out, da_w_qkv, da_lambda,
           da_norm_g, da_w_out, final_g):
    batch, seq, d = x.shape
    ctx_len = ctx.shape[1]
    assert ctx_len % ROW_TILE == 0 and seq % ROW_TILE == 0 and batch + 1 <= SUBLANES
    nct = ctx_len // ROW_TILE
    depth = ada_w.shape[0]

    xs = jnp.concatenate([ctx, x], axis=1)
    cc = jnp.concatenate([c_ctx[None], c, jnp.zeros((SUBLANES - 1 - batch, d), F32)], axis=0)
    mod = _ada(cc, ada_w, ada_b).reshape(depth, SUBLANES, 6, d)
    modsel = jnp.stack([jnp.broadcast_to(mod[:, 0:1], (depth, batch, 6, d)), mod[:, 1:1 + batch]], axis=2)

    head_id = jnp.arange(d) // RW_HEAD
    ones_bd = (head_id[:, None] == head_id[None, :]).astype(BF16)

    for i in range(depth):
        kind, j = i % N_MIXERS, i // N_MIXERS
        if kind == 0:
            p = dict(mix=ra_mix[j], w_rkv=ra_w_rkv[j], w0=ra_w0[j], w1=ra_w1[j], w2=ra_w2[j], a0=ra_a0[j],
                     a1=ra_a1[j], a2=ra_a2[j], g1=ra_g1[j], g2=ra_g2[j], k_k=ra_k_k[j], k_a=ra_k_a[j],
                     r_k=ra_r_k[j], lnx_g=ra_lnx_g[j], lnx_b=ra_lnx_b[j], w_out=ra_w_out[j])
            xs = _rwkv_layer(xs, norm1_g[i], modsel[i], p, ones_bd, nct, ctx_len)
        elif kind == 1:
            p = dict(w_in=ml_w_in[j], b_in=ml_b_in[j], norm_g=ml_norm_g[j], w_out=ml_w_out[j])
            xs = _mlstm_layer(xs, norm1_g[i], modsel[i], p, nct, ctx_len)
        else:
            lambda_init = 0.8 - 0.6 * math.exp(-0.3 * i)
            p = dict(w_qkv=da_w_qkv[j], lam=da_lambda[j], norm_g=da_norm_g[j], w_out=da_w_out[j])
            xs = _attn_layer(xs, norm1_g[i], modsel[i], p, lambda_init, nct, ctx_len)
        xs = _ffn_call(xs, norm2_g[i], modsel[i], ffn_w_in[i], ffn_conv_w[i], ffn_conv_b[i], ffn_w_out[i], nct,
                       final_g=final_g if i == depth - 1 else None)
    return xs
```

```python
import functools
import math

import jax
import jax.numpy as jnp
from jax import lax
from jax.experimental import pallas as pl
from jax.experimental.pallas import tpu as pltpu

F32 = jnp.float32
BF16 = jnp.bfloat16

DEPTH = 4
N_MIXERS = 3
GRID_W = 64
NORM_EPS = 1e-6
RW_HEAD = 64
RW_LNX_EPS = 6.4e-4
ML_HEADS = 4
ML_CHUNK = 128
DA_HEADS = 8
DA_HEAD = 64
DA_EPS = 1e-5
ROPE_BASE = 10000.0

LANES = 128
SUBLANES = 8
VMEM_LIMIT_BYTES = 56 * 1024 * 1024

ROW_TILE = 256
RW_CHUNK = 64
RW_GROUP = 4
FFN_COLS = 256
ATT_UNROLL = 4
ATT_ONES = 16
ATT_HEADS = 4


def _params(sem, vmem=None):
    return pltpu.CompilerParams(dimension_semantics=sem, vmem_limit_bytes=vmem)


def _mm(a, b):
    return jnp.dot(a.astype(BF16), b.astype(BF16), preferred_element_type=F32)


def _mm_nt(a, b):
    return lax.dot_general(a.astype(BF16), b.astype(BF16), (((1,), (1,)), ((), ())),
                           preferred_element_type=F32)


def _mm_tn(a, b):
    return lax.dot_general(a.astype(BF16), b.astype(BF16), (((0,), (0,)), ((), ())),
                           preferred_element_type=F32)


def _split3(x):
    hi = x.astype(BF16)
    r1 = x - hi.astype(F32)
    mid = r1.astype(BF16)
    lo = (r1 - mid.astype(F32)).astype(BF16)
    return hi, mid, lo


def _exact_left(m_bf16, x, terms=3):
    return sum(jnp.dot(m_bf16, p, preferred_element_type=F32) for p in _split3(x)[:terms])


def _normmod(x, g, sc, sh):
    y = x * lax.rsqrt(jnp.mean(x * x, axis=-1, keepdims=True) + NORM_EPS)
    return y * g * (1.0 + sc) + sh


def _ada_body(c_ref, w_ref, b_ref, o_ref):
    a = c_ref[...]
    a = a * jax.nn.sigmoid(a)
    o_ref[0] = jnp.dot(a, w_ref[0], precision=lax.Precision.HIGHEST,
                       preferred_element_type=F32) + b_ref[0]


def _ada(cc, ada_w, ada_b):
    depth, d, n = ada_w.shape
    nb = n // 4
    return pl.pallas_call(
        _ada_body,
        grid=(depth, n // nb),
        in_specs=[pl.BlockSpec((SUBLANES, d), lambda l, j: (0, 0)),
                  pl.BlockSpec((1, d, nb), lambda l, j: (l, 0, j)),
                  pl.BlockSpec((1, 1, nb), lambda l, j: (l, 0, j))],
        out_specs=pl.BlockSpec((1, SUBLANES, nb), lambda l, j: (l, 0, j)),
        out_shape=jax.ShapeDtypeStruct((depth, SUBLANES, n), F32),
        compiler_params=_params(("arbitrary", "arbitrary"), VMEM_LIMIT_BYTES),
        name="ada_mod",
    )(cc, ada_w, ada_b.reshape(depth, 1, n))


def _row_spec(d, tm=ROW_TILE):
    return pl.BlockSpec((1, tm, d), lambda b, t: (b, t, 0))


def _mod_spec(d, nct):
    return pl.BlockSpec((1, 1, 6, d), lambda b, t: (b, jnp.where(t >= nct, 1, 0), 0, 0))


def _full_spec(shape):
    nd = len(shape)
    return pl.BlockSpec(shape, lambda b, t: (0,) * nd)


def _halo_specs(d, tm, n_rows):
    per = tm // SUBLANES
    last = n_rows // SUBLANES - 1
    prev = pl.BlockSpec((1, SUBLANES, d), lambda b, t: (b, jnp.maximum(t * per - 1, 0), 0))
    nxt = pl.BlockSpec((1, SUBLANES, d), lambda b, t: (b, jnp.minimum((t + 1) * per, last), 0))
    return prev, nxt


def _ffn_body(x_ref, xp_ref, xn_ref, g_ref, m_ref, win_ref, cw_ref, cb_ref, wout_ref, fg_ref, o_ref, act_ref,
              *, nct, nt, t0, hidden, final):
    t = pl.program_id(1) + t0
    x = x_ref[0]
    tm = x.shape[0]
    m = m_ref[0, 0]
    xa = jnp.concatenate([xp_ref[0], x, xn_ref[0]], axis=0)
    h32 = _normmod(xa, g_ref[...], m[4:5], m[3:4])
    ha = h32.astype(BF16)
    hm = h32[SUBLANES:SUBLANES + tm].astype(BF16)
    first = jnp.logical_or(t == 0, t == nct)
    last = jnp.logical_or(t == nct - 1, t == nt - 1)
    rows = lax.broadcasted_iota(jnp.int32, (tm + 2 * SUBLANES, 1), 0)
    dead = jnp.logical_or(rows == jnp.where(first, SUBLANES - 1, -1),
                          rows == jnp.where(last, tm + SUBLANES, -1))
    keep = jnp.where(dead, 0.0, 1.0)
    for j in range(hidden // FFN_COLS):
        lo, hi = j * FFN_COLS, (j + 1) * FFN_COLS
        val = jnp.dot(hm, win_ref[:, lo:hi], preferred_element_type=F32)
        gate = jnp.dot(ha, win_ref[:, hidden + lo:hidden + hi], preferred_element_type=F32) * keep
        g_prev = pltpu.roll(gate, 1, 0)[SUBLANES:SUBLANES + tm]
        g_next = pltpu.roll(gate, tm + 2 * SUBLANES - 1, 0)[SUBLANES:SUBLANES + tm]
        g_mid = gate[SUBLANES:SUBLANES + tm]
        cw = cw_ref[:, lo:hi]
        conv = cw[0:1] * g_prev + cw[1:2] * g_mid + cw[2:3] * g_next + cb_ref[:, lo:hi]
        act_ref[:, lo:hi] = (jax.nn.gelu(conv) * val).astype(BF16)
    out = jnp.dot(act_ref[...], wout_ref[...], preferred_element_type=F32)
    y = x + m[5:6] * out
    if final:
        y = y * lax.rsqrt(jnp.mean(y * y, axis=-1, keepdims=True) + NORM_EPS) * fg_ref[...]
    o_ref[0] = y


def _ffn_call(xs, g, modsel, w_in, conv_w, conv_b, w_out, nct, final_g=None):
    b, t, d = xs.shape
    hidden = w_out.shape[0]
    nt = t // ROW_TILE
    final = final_g is not None
    t0 = nct if final else 0
    per = ROW_TILE // SUBLANES
    last = t // SUBLANES - 1
    body = functools.partial(_ffn_body, nct=nct, nt=nt, t0=t0, hidden=hidden, final=final)
    return pl.pallas_call(
        body,
        grid=(b, nt - t0),
        in_specs=[pl.BlockSpec((1, ROW_TILE, d), lambda bb, tt: (bb, tt + t0, 0)),
                  pl.BlockSpec((1, SUBLANES, d), lambda bb, tt: (bb, jnp.maximum((tt + t0) * per - 1, 0), 0)),
                  pl.BlockSpec((1, SUBLANES, d), lambda bb, tt: (bb, jnp.minimum((tt + t0 + 1) * per, last), 0)),
                  _full_spec((1, d)),
                  pl.BlockSpec((1, 1, 6, d), lambda bb, tt: (bb, jnp.where(tt + t0 >= nct, 1, 0), 0, 0)),
                  _full_spec((d, 2 * hidden)), _full_spec((3, hidden)), _full_spec((1, hidden)),
                  _full_spec((hidden, d)), _full_spec((1, d))],
        out_specs=_row_spec(d),
        out_shape=jax.ShapeDtypeStruct((b, t - t0 * ROW_TILE, d), F32),
        scratch_shapes=[pltpu.VMEM((ROW_TILE, hidden), BF16)],
        compiler_params=_params(("parallel", "parallel"), VMEM_LIMIT_BYTES),
        name="conv_ffn",
    )(xs, xs, xs, g.reshape(1, d), modsel, w_in.astype(BF16), conv_w, conv_b.reshape(1, hidden),
      w_out.astype(BF16), (final_g if final else g).reshape(1, d))


def _seg_sum(x, ones_bd):
    return jnp.dot(x.astype(BF16), ones_bd, preferred_element_type=F32)


def _rwkv_prep_body(x_ref, xp_ref, xn_ref, ng_ref, m_ref, mix_ref, wrkv_ref, w1_ref, a1_ref, g1_ref, w2_ref, a2_ref,
                    g2_ref, vec_ref, ones_ref,
                    r_ref, v_ref, kk_ref, g_ref, bonus_ref, lw_ref, kd_ref, ka_ref, *, nct, nt):
    t = pl.program_id(1)
    m = m_ref[0, 0]
    h = _normmod(x_ref[0], ng_ref[...], m[1:2], m[0:1])
    tm = h.shape[0]
    first = jnp.logical_or(t == 0, t == nct)
    last = jnp.logical_or(t == nct - 1, t == nt - 1)
    p_row = _normmod(xp_ref[0], ng_ref[...], m[1:2], m[0:1])[SUBLANES - 1:SUBLANES] * jnp.where(first, 0.0, 1.0)
    n_row = _normmod(xn_ref[0], ng_ref[...], m[1:2], m[0:1])[0:1] * jnp.where(last, 0.0, 1.0)
    rows = lax.broadcasted_iota(jnp.int32, (tm, 1), 0)
    h_prev = jnp.where(rows == 0, p_row, pltpu.roll(h, 1, 0))
    h_next = jnp.where(rows == tm - 1, n_row, pltpu.roll(h, tm - 1, 0))
    xx = 0.5 * (h_prev + h_next) - h
    mix = mix_ref[...]
    vec = vec_ref[...]
    ones_bd = ones_ref[...]

    def shifted(j):
        return h + xx * mix[j:j + 1]

    r = _mm(shifted(0), wrkv_ref[0])
    k = _mm(shifted(1), wrkv_ref[1])
    v = _mm(shifted(2), wrkv_ref[2])
    w_in = jnp.tanh(_mm(shifted(3), w1_ref[...]))
    a_in = _mm(shifted(4), a1_ref[...])
    g = _mm(jax.nn.sigmoid(_mm(shifted(5), g1_ref[...])), g2_ref[...])

    kk = k * vec[4:5]
    kk = kk / jnp.maximum(jnp.sqrt(_seg_sum(kk * kk, ones_bd)), 1e-12)
    kd_sum = None
    for d in range(2):
        w_pre = vec[d:d + 1] + _mm(w_in, w2_ref[d])
        lw_ref[d, 0] = -jax.nn.sigmoid(w_pre) * math.exp(-0.5)
        a = jax.nn.sigmoid(vec[2 + d:3 + d] + _mm(a_in, a2_ref[d]))
        kd = k * (1.0 + (a - 1.0) * vec[5:6])
        kd_ref[d, 0] = kd
        ka_ref[d, 0] = kk * a
        kd_sum = kd if kd_sum is None else kd_sum + kd
    r_ref[0] = r
    v_ref[0] = v
    kk_ref[0] = kk
    g_ref[0] = g
    bonus_ref[0] = _seg_sum(r * kd_sum * vec[6:7], ones_bd) * v


def _rwkv_prep(xs, ng, modsel, mix, w_rkv, w0, w1, w2, a0, a1, a2, g1, g2, k_k, k_a, r_k, ones_bd, nct):
    b, t, d = xs.shape
    nt = t // ROW_TILE
    lora = w1.shape[-1]
    prev, nxt = _halo_specs(d, ROW_TILE, t)
    zeros = jnp.zeros((lora, d), F32)
    w2p = jnp.stack([jnp.concatenate([w2[0], zeros], 0), jnp.concatenate([zeros, w2[1]], 0)]).astype(BF16)
    a2p = jnp.stack([jnp.concatenate([a2[0], zeros], 0), jnp.concatenate([zeros, a2[1]], 0)]).astype(BF16)
    w1c = jnp.concatenate([w1[0], w1[1]], axis=1).astype(BF16)
    a1c = jnp.concatenate([a1[0], a1[1]], axis=1).astype(BF16)
    vec = jnp.stack([w0[0], w0[1], a0[0], a0[1], k_k, k_a, r_k.reshape(d), jnp.zeros((d,), F32)])
    shared = jax.ShapeDtypeStruct((b, t, d), F32)
    directional = jax.ShapeDtypeStruct((2, b, t, d), F32)
    dir_spec = pl.BlockSpec((2, 1, ROW_TILE, d), lambda bb, tt: (0, bb, tt, 0))
    body = functools.partial(_rwkv_prep_body, nct=nct, nt=nt)
    return pl.pallas_call(
        body,
        grid=(b, nt),
        in_specs=[_row_spec(d), prev, nxt, _full_spec((1, d)), _mod_spec(d, nct), _full_spec((6, d)),
                  _full_spec((3, d, d)), _full_spec(w1c.shape), _full_spec(a1c.shape), _full_spec(g1.shape),
                  _full_spec(w2p.shape), _full_spec(a2p.shape), _full_spec(g2.shape),
                  _full_spec((SUBLANES, d)), _full_spec((d, d))],
        out_specs=[_row_spec(d)] * 5 + [dir_spec] * 3,
        out_shape=[shared] * 5 + [directional] * 3,
        compiler_params=_params(("parallel", "parallel"), VMEM_LIMIT_BYTES),
        name="rwkv_prep",
    )(xs, xs, xs, ng.reshape(1, d), modsel, mix, w_rkv.astype(BF16), w1c, a1c, g1.astype(BF16), w2p, a2p, g2.astype(BF16), vec,
      ones_bd)


def _rwkv_scan_body(rf_ref, vf_ref, kkf_ref, lwf_ref, kdf_ref, kaf_ref, rb_ref, vb_ref, kkb_ref, lwb_ref, kdb_ref,
                    kab_ref, yf_ref, yb_ref, s_ref, *, chunk, group):
    @pl.when(pl.program_id(0) == 0)
    def _():
        s_ref[...] = jnp.zeros_like(s_ref)

    c, gc, width = chunk, group * chunk, group * RW_HEAD
    pr = lax.broadcasted_iota(jnp.int32, (2 * c, 2 * gc), 0)
    pc = lax.broadcasted_iota(jnp.int32, (2 * c, 2 * gc), 1)
    wr = lax.broadcasted_iota(jnp.int32, (c, gc), 0)
    wc = lax.broadcasted_iota(jnp.int32, (c, gc), 1)
    eye_w = jnp.where(wr == wc % c, 1.0, 0.0)
    tr = lax.broadcasted_iota(jnp.int32, (c, c), 0)
    tc = lax.broadcasted_iota(jnp.int32, (c, c), 1)
    own = (lax.broadcasted_iota(jnp.int32, (gc, width), 0) // c
           == lax.broadcasted_iota(jnp.int32, (gc, width), 1) // RW_HEAD)
    same_head = (lax.broadcasted_iota(jnp.int32, (width, width), 0) // RW_HEAD
                 == lax.broadcasted_iota(jnp.int32, (width, width), 1) // RW_HEAD)

    def stack(x):
        return jnp.where(own, jnp.concatenate([x] * group, axis=0), 0.0).astype(BF16)

    def direction(di, r_ref, v_ref, kk_ref, lw_ref, kd_ref, ka_ref, y_ref):
        sgn = 1 - 2 * di
        rows = range(r_ref.shape[0])
        chains = [(bi, hg) for bi in rows for hg in range(r_ref.shape[-1] // width)]
        lns = [slice(hg * width, (hg + 1) * width) for _, hg in chains]
        gs = range(len(chains))
        st = {}

        def prepare():
            tri = jnp.where((tr - tc) * sgn >= 0, 1.0, 0.0).astype(BF16)
            lw_all = [lw_ref[0, bi] for bi in rows]
            cum_all = [_exact_left(tri, lw_all[bi], terms=2) for bi in rows]
            tot_all = [jnp.sum(lw_all[bi], axis=0, keepdims=True) for bi in rows]
            st["left"], st["right"], st["carry_in"], st["v_nat"], st["vm"], st["decay"] = [], [], [], [], [], []
            for (bi, _), ln in zip(chains, lns):
                lw, cum, tot = lw_all[bi][:, ln], cum_all[bi][:, ln], tot_all[bi][:, ln]
                e_neg = jnp.exp(-cum)
                e_out = jnp.exp(tot - cum)
                kk, ka, kd = kk_ref[bi, :, ln], ka_ref[0, bi, :, ln], kd_ref[0, bi, :, ln]
                st["left"].append(jnp.concatenate([kk * jnp.exp(cum - lw), r_ref[bi, :, ln] * jnp.exp(cum)],
                                                  axis=0).astype(BF16))
                st["right"].append(jnp.concatenate([stack(ka * e_neg), stack(kd * e_neg)], axis=0))
                st["carry_in"].append(jnp.concatenate([ka * e_out, kd * e_out], axis=0).astype(BF16))
                st["v_nat"].append(v_ref[bi, :, ln].astype(BF16))
                st["vm"].append(stack(v_ref[bi, :, ln]))
                st["decay"].append(jnp.exp(tot))

        def pair_products():
            visible = ((pr % c) - (pc % c)) * sgn > jnp.where(pr < c, 0, -1)
            st["pair"] = [jnp.where(visible, _mm_nt(st["left"][g], st["right"][g]), 0.0) for g in gs]
            st["inv"] = [eye_w - st["pair"][g][0:c, 0:gc] for g in gs]
            st["xp"] = [_mm(st["pair"][g][0:c, 0:gc], stack(st["pair"][g][0:c, 0:gc])) for g in gs]

        def double():
            both = [_mm(jnp.concatenate([st["inv"][g], st["xp"][g]], axis=0), stack(st["xp"][g])) for g in gs]
            st["inv"] = [st["inv"][g] + both[g][0:c] for g in gs]
            st["xp"] = [both[g][c:2 * c] for g in gs]

        def last_double():
            st["inv"] = [st["inv"][g] + _mm(st["inv"][g], stack(st["xp"][g])) for g in gs]

        def residual():
            st["resid"] = [(eye_w - st["inv"][g]) - _mm(st["pair"][g][0:c, 0:gc], stack(st["inv"][g])) for g in gs]

        def newton():
            st["inv"] = [st["inv"][g] + _mm(st["inv"][g], stack(st["resid"][g])) for g in gs]

        def read_state():
            st["s"] = [s_ref[di, chains[g][0], chains[g][1]] for g in gs]
            st["from_state"] = [_mm_nt(st["left"][g], st["s"][g]) for g in gs]
            st["from_v"] = [_mm(st["pair"][g][:, gc:2 * gc], st["vm"][g]) for g in gs]

        def solve():
            st["u"] = [-_mm(st["inv"][g], stack(st["from_state"][g][0:c] + st["from_v"][g][0:c])) for g in gs]

        def emit():
            for g in gs:
                bi, hg = chains[g]
                y_ref[bi, :, lns[g]] = (st["from_state"][g][c:2 * c] + st["from_v"][g][c:2 * c]
                                        + _mm(st["pair"][g][c:2 * c, 0:gc], stack(st["u"][g])))
                grown = _mm_tn(jnp.concatenate([st["u"][g].astype(BF16), st["v_nat"][g]], axis=0),
                               st["carry_in"][g])
                s_ref[di, bi, hg] = st["s"][g] * st["decay"][g] + jnp.where(same_head, grown, 0.0)

        return ([prepare, pair_products] + [double] * (int(math.log2(c)) - 3)
                + [last_double, residual, newton, read_state, solve, emit])

    forward = direction(0, rf_ref, vf_ref, kkf_ref, lwf_ref, kdf_ref, kaf_ref, yf_ref)
    backward = direction(1, rb_ref, vb_ref, kkb_ref, lwb_ref, kdb_ref, kab_ref, yb_ref)
    for stage_f, stage_b in zip(forward, backward):
        stage_f()
        stage_b()


def _rwkv_scan(r, v, kk, lw, kd, ka, ctx_len):
    b, t, d = r.shape
    c = RW_CHUNK
    width = RW_GROUP * RW_HEAD
    nc, nck = t // c, ctx_len // c

    def back(i):
        return jnp.where(i < nck, nck - 1 - i, nc - 1 - (i - nck))

    fwd = pl.BlockSpec((b, c, d), lambda i: (0, i, 0))
    bwd = pl.BlockSpec((b, c, d), lambda i: (0, back(i), 0))
    fwd_dir = pl.BlockSpec((1, b, c, d), lambda i: (0, 0, i, 0))
    bwd_dir = pl.BlockSpec((1, b, c, d), lambda i: (1, 0, back(i), 0))
    body = functools.partial(_rwkv_scan_body, chunk=c, group=RW_GROUP)
    return pl.pallas_call(
        body,
        grid=(nc,),
        in_specs=[fwd, fwd, fwd, fwd_dir, fwd_dir, fwd_dir, bwd, bwd, bwd, bwd_dir, bwd_dir, bwd_dir],
        out_specs=[fwd, bwd],
        out_shape=[jax.ShapeDtypeStruct((b, t, d), F32)] * 2,
        scratch_shapes=[pltpu.VMEM((2, b, d // width, width, width), F32)],
        compiler_params=_params(("arbitrary",), VMEM_LIMIT_BYTES),
        name="rwkv_scan",
    )(r, v, kk, lw, kd, ka, r, v, kk, lw, kd, ka)


def _rwkv_post_body(yf_ref, yb_ref, bonus_ref, g_ref, x_ref, m_ref, lng_ref, lnb_ref, ones_ref, wout_ref, o_ref):
    ones_bd = ones_ref[...]
    y = yf_ref[0] + yb_ref[0]
    mu = _seg_sum(y, ones_bd) * (1.0 / RW_HEAD)
    yc = y - mu
    var = _seg_sum(yc * yc, ones_bd) * (1.0 / RW_HEAD)
    yn = yc * lax.rsqrt(var + RW_LNX_EPS) * lng_ref[...] + lnb_ref[...]
    out = _mm((yn + bonus_ref[0]) * g_ref[0], wout_ref[...])
    o_ref[0] = x_ref[0] + m_ref[0, 0][2:3] * out


def _rwkv_post(yf, yb, bonus, g, xs, modsel, lnx_g, lnx_b, ones_bd, w_out, nct):
    b, t, d = xs.shape
    return pl.pallas_call(
        _rwkv_post_body,
        grid=(b, t // ROW_TILE),
        in_specs=[_row_spec(d)] * 5 + [_mod_spec(d, nct), _full_spec((1, d)), _full_spec((1, d)),
                                      _full_spec((d, d)), _full_spec((d, d))],
        out_specs=_row_spec(d),
        out_shape=jax.ShapeDtypeStruct((b, t, d), F32),
        compiler_params=_params(("parallel", "parallel"), VMEM_LIMIT_BYTES),
        name="rwkv_post",
    )(yf, yb, bonus, g, xs, modsel, lnx_g.reshape(1, d), lnx_b.reshape(1, d), ones_bd, w_out.astype(BF16))


def _rwkv_layer(xs, n1g, modsel, p, ones_bd, nct, ctx_len):
    r, v, kk, g, bonus, lw, kd, ka = _rwkv_prep(
        xs, n1g, modsel, p["mix"], p["w_rkv"], p["w0"], p["w1"], p["w2"], p["a0"], p["a1"], p["a2"], p["g1"], p["g2"],
        p["k_k"], p["k_a"], p["r_k"], ones_bd, nct)
    yf, yb = _rwkv_scan(r, v, kk, lw, kd, ka, ctx_len)
    return _rwkv_post(yf, yb, bonus, g, xs, modsel, p["lnx_g"], p["lnx_b"], ones_bd, p["w_out"], nct)


def _mlstm_in_body(x_ref, g_ref, m_ref, w_ref, b_ref, q_ref, k_ref, v_ref, o_ref, gf_ref, gb_ref,
                   *, heads, dk, dv):
    m = m_ref[0, 0]
    h = _normmod(x_ref[0], g_ref[...], m[1:2], m[0:1])
    z = _mm(h, w_ref[...]) + b_ref[...]
    nqk = heads * dk
    nv = heads * dv
    q_ref[0] = z[:, 0:nqk]
    k_ref[0] = z[:, nqk:2 * nqk] * (dk ** -0.5)
    v_ref[0] = z[:, 2 * nqk:2 * nqk + nv]
    d_model = o_ref.shape[-1]
    o_ref[0] = jax.nn.sigmoid(z[:, 2 * nqk + nv:2 * nqk + nv + d_model])
    gates = z[:, 2 * nqk + nv + d_model:]
    lane = lax.broadcasted_iota(jnp.int32, (1, LANES), 1)
    is_forget = jnp.logical_and(lane >= heads, lane < 2 * heads)
    for ref, off in ((gf_ref, 0), (gb_ref, LANES)):
        gd = gates[:, off:off + LANES]
        ref[0] = jnp.where(is_forget, jax.nn.log_sigmoid(gd), gd)


def _mlstm_in(xs, g, modsel, w_in, b_in, nct):
    b, t, d = xs.shape
    heads, dk, dv = ML_HEADS, d // (2 * ML_HEADS), d // ML_HEADS
    main = 2 * heads * dk + heads * dv + d
    pad = jnp.zeros((d, LANES - 2 * heads), F32)
    w_cat = jnp.concatenate([w_in[:, :main], w_in[:, main:main + 2 * heads], pad,
                             w_in[:, main + 2 * heads:], pad], axis=1)
    padb = jnp.zeros((LANES - 2 * heads,), F32)
    b_cat = jnp.concatenate([b_in[:main], b_in[main:main + 2 * heads], padb, b_in[main + 2 * heads:], padb])
    n = w_cat.shape[1]
    body = functools.partial(_mlstm_in_body, heads=heads, dk=dk, dv=dv)
    shapes = [(heads * dk), (heads * dk), (heads * dv), d, LANES, LANES]
    return pl.pallas_call(
        body,
        grid=(b, t // ROW_TILE),
        in_specs=[_row_spec(d), _full_spec((1, d)), _mod_spec(d, nct), _full_spec((d, n)), _full_spec((1, n))],
        out_specs=[_row_spec(s) for s in shapes],
        out_shape=[jax.ShapeDtypeStruct((b, t, s), F32) for s in shapes],
        compiler_params=_params(("parallel", "parallel"), VMEM_LIMIT_BYTES),
        name="mlstm_in",
    )(xs, g.reshape(1, d), modsel, w_cat.astype(BF16), b_cat.reshape(1, n))


def _mlstm_chunk_body(qf_ref, kf_ref, vf_ref, gcf_ref, grf_ref, qb_ref, kb_ref, vb_ref, gcb_ref, grb_ref,
                      hf_ref, hb_ref, c_ref, n_ref, m_ref, *, heads, dk, dv):
    @pl.when(pl.program_id(1) == 0)
    def _():
        c_ref[...] = jnp.zeros_like(c_ref)
        n_ref[...] = jnp.zeros_like(n_ref)
        m_ref[...] = jnp.zeros_like(m_ref)

    t = qf_ref.shape[1]
    rr = lax.broadcasted_iota(jnp.int32, (t, t), 0)
    cc = lax.broadcasted_iota(jnp.int32, (t, t), 1)
    chains, seen = [], []
    li_col, li_row, b_col, b_row, b_end, q32, q, k, v = [], [], [], [], [], [], [], [], []
    for di, (q_ref, k_ref, v_ref, gc_ref, gr_ref, h_ref) in enumerate(
            ((qf_ref, kf_ref, vf_ref, gcf_ref, grf_ref, hf_ref), (qb_ref, kb_ref, vb_ref, gcb_ref, grb_ref, hb_ref))):
        sgn = 1 - 2 * di
        vis = (rr - cc) * sgn >= 0
        tri = jnp.where(vis, 1.0, 0.0).astype(BF16)
        gcol = gc_ref[0, 0]
        grow = gr_ref[0, 0]
        bcol = _exact_left(tri, gcol)
        brow = sum(_mm_nt(p, tri) for p in _split3(grow))
        for h in range(heads):
            chains.append((di, h, h_ref))
            seen.append(vis)
            li_col.append(gcol[:, h:h + 1])
            li_row.append(grow[h:h + 1, :])
            b_col.append(bcol[:, heads + h:heads + h + 1])
            b_row.append(brow[heads + h:heads + h + 1, :])
            b_end.append(jnp.sum(gcol[:, heads + h:heads + h + 1], axis=0, keepdims=True))
            q32.append(q_ref[0, :, h * dk:(h + 1) * dk])
            q.append(q32[-1].astype(BF16))
            k.append(k_ref[0, :, h * dk:(h + 1) * dk])
            v.append(v_ref[0, :, h * dv:(h + 1) * dv].astype(BF16))
    cs = range(len(chains))
    c0 = [c_ref[chains[i][0], chains[i][1]] for i in cs]
    n0 = [n_ref[chains[i][0], chains[i][1]] for i in cs]
    m0 = [m_ref[chains[i][0], chains[i][1]][:, 0:1] for i in cs]
    g_end = [b_end[i] - b_col[i] + li_col[i] for i in cs]
    m_loc = [jnp.max(g_end[i], axis=0, keepdims=True) for i in cs]
    kw = [k[i] * jnp.exp(g_end[i] - m_loc[i]) for i in cs]
    c_loc = [_mm_tn(kw[i], v[i]) for i in cs]
    n_loc = [jnp.sum(kw[i], axis=0, keepdims=True) for i in cs]
    qk = [_mm_nt(q[i], k[i]) for i in cs]
    qc = [_mm(q[i], c0[i]) for i in cs]
    d_log = [jnp.where(seen[i], b_col[i] - b_row[i] + li_row[i], -jnp.inf) for i in cs]
    inter = [b_col[i] + m0[i] for i in cs]
    m_t = [jnp.maximum(jnp.max(d_log[i], axis=-1, keepdims=True), inter[i]) for i in cs]
    p = [jnp.exp(d_log[i] - m_t[i]) * qk[i] for i in cs]
    e_inter = [jnp.exp(inter[i] - m_t[i]) for i in cs]
    num = [_mm(p[i], v[i]) + e_inter[i] * qc[i] for i in cs]
    for i in cs:
        di, h, h_ref = chains[i]
        den = (jnp.sum(p[i], axis=-1, keepdims=True)
               + e_inter[i] * jnp.sum(q32[i] * n0[i], axis=-1, keepdims=True))
        h_ref[0, :, h * dv:(h + 1) * dv] = num[i] / jnp.maximum(jnp.abs(den), jnp.exp(-m_t[i]))
        m_new = jnp.maximum(b_end[i] + m0[i], m_loc[i])
        a = jnp.exp(b_end[i] + m0[i] - m_new)
        e = jnp.exp(m_loc[i] - m_new)
        c_ref[di, h] = a * c0[i] + e * c_loc[i]
        n_ref[di, h] = a * n0[i] + e * n_loc[i]
        m_ref[di, h] = jnp.broadcast_to(m_new, (1, LANES))


def _mlstm_chunks(q, k, v, gcol, grow, ctx_len):
    b, t, _ = q.shape
    heads = ML_HEADS
    dk, dv = q.shape[-1] // heads, v.shape[-1] // heads
    c = ML_CHUNK
    nc, nck = t // c, ctx_len // c

    def back(i):
        return jnp.where(i < nck, nck - 1 - i, nc - 1 - (i - nck))

    def specs(chunk_of, di):
        seq = lambda width: pl.BlockSpec((1, c, width), lambda bb, i: (bb, chunk_of(i), 0))
        return [seq(heads * dk), seq(heads * dk), seq(heads * dv),
                pl.BlockSpec((1, 1, c, LANES), lambda bb, i: (di, bb, chunk_of(i), 0)),
                pl.BlockSpec((1, 1, SUBLANES, c), lambda bb, i: (di, bb, 0, chunk_of(i)))]

    body = functools.partial(_mlstm_chunk_body, heads=heads, dk=dk, dv=dv)
    return pl.pallas_call(
        body,
        grid=(b, nc),
        in_specs=specs(lambda i: i, 0) + specs(back, 1),
        out_specs=[pl.BlockSpec((1, c, heads * dv), lambda bb, i: (bb, i, 0)),
                   pl.BlockSpec((1, c, heads * dv), lambda bb, i: (bb, back(i), 0))],
        out_shape=[jax.ShapeDtypeStruct((b, t, heads * dv), F32)] * 2,
        scratch_shapes=[pltpu.VMEM((2, heads, dk, dv), F32), pltpu.VMEM((2, heads, 1, dk), F32),
                        pltpu.VMEM((2, heads, 1, LANES), F32)],
        compiler_params=_params(("parallel", "arbitrary")),
        name="mlstm_chunks",
    )(q, k, v, gcol, grow, q, k, v, gcol, grow)


def _mlstm_post_body(hf_ref, hb_ref, o_ref, x_ref, m_ref, ng_ref, wout_ref, out_ref, *, heads):
    hs = hf_ref[0] + hb_ref[0]
    dv = hs.shape[-1] // heads
    parts = []
    for h in range(heads):
        hh = hs[:, h * dv:(h + 1) * dv]
        parts.append(hh * lax.rsqrt(jnp.mean(hh * hh, axis=-1, keepdims=True) + NORM_EPS))
    hn = jnp.concatenate(parts, axis=-1)
    out = _mm(hn * ng_ref[...] * o_ref[0], wout_ref[...])
    out_ref[0] = x_ref[0] + m_ref[0, 0][2:3] * out


def _mlstm_post(hf, hb, o, xs, modsel, norm_g, w_out, nct):
    b, t, d = xs.shape
    body = functools.partial(_mlstm_post_body, heads=ML_HEADS)
    return pl.pallas_call(
        body,
        grid=(b, t // ROW_TILE),
        in_specs=[_row_spec(d)] * 4 + [_mod_spec(d, nct), _full_spec((1, d)), _full_spec((d, d))],
        out_specs=_row_spec(d),
        out_shape=jax.ShapeDtypeStruct((b, t, d), F32),
        compiler_params=_params(("parallel", "parallel"), VMEM_LIMIT_BYTES),
        name="mlstm_post",
    )(hf, hb, o, xs, modsel, norm_g.reshape(1, d), w_out.astype(BF16))


def _mlstm_layer(xs, n1g, modsel, p, nct, ctx_len):
    q, k, v, o, gf, gb = _mlstm_in(xs, n1g, modsel, p["w_in"], p["b_in"], nct)
    gcol = jnp.stack([gf, gb])
    grow = jnp.swapaxes(gcol[..., :SUBLANES], -1, -2)
    hf, hb = _mlstm_chunks(q, k, v, gcol, grow, ctx_len)
    return _mlstm_post(hf, hb, o, xs, modsel, p["norm_g"], p["w_out"], nct)


def _attn_in_body(x_ref, g_ref, m_ref, w_ref, cos_ref, sin_ref, qt_ref, k_ref, vt_ref, *, d_model):
    m = m_ref[0, 0]
    h = _normmod(x_ref[0], g_ref[...], m[1:2], m[0:1])
    z = _mm(h, w_ref[...])
    reps = d_model // LANES
    cos = jnp.concatenate([cos_ref[...]] * reps, axis=1)
    sin = jnp.concatenate([sin_ref[...]] * reps, axis=1)
    lane = lax.broadcasted_iota(jnp.int32, (1, d_model), 1)
    low = (lane % DA_HEAD) < (DA_HEAD // 2)
    half = DA_HEAD // 2

    def rope(u):
        partner = jnp.where(low, pltpu.roll(u, d_model - half, 1), pltpu.roll(u, half, 1))
        return u * cos + partner * sin

    q = rope(z[:, 0:d_model]) * (DA_HEAD ** -0.5 * math.log2(math.e))
    qt_ref[0] = q.T.astype(BF16)
    k_ref[0] = rope(z[:, d_model:2 * d_model]).astype(BF16)
    vt = z[:, 2 * d_model:].T.astype(BF16)
    dh2 = 2 * DA_HEAD
    ones = jnp.ones((ATT_ONES, vt.shape[1]), BF16)
    for hh in range(d_model // dh2):
        vt_ref[0, 0, hh] = jnp.concatenate([vt[hh * dh2:(hh + 1) * dh2], ones], axis=0)


def _attn_in(xs, g, modsel, w_qkv, cos, sin, nct):
    b, t, d = xs.shape
    nt = t // ROW_TILE
    heads, vrows = d // (2 * DA_HEAD), 2 * DA_HEAD + ATT_ONES
    body = functools.partial(_attn_in_body, d_model=d)
    tab = pl.BlockSpec((ROW_TILE, LANES), lambda bb, tt: (tt, 0))
    return pl.pallas_call(
        body,
        grid=(b, nt),
        in_specs=[_row_spec(d), _full_spec((1, d)), _mod_spec(d, nct), _full_spec((d, 3 * d)), tab, tab],
        out_specs=[pl.BlockSpec((1, d, ROW_TILE), lambda bb, tt: (bb, 0, tt)), _row_spec(d),
                   pl.BlockSpec((1, 1, heads, vrows, ROW_TILE), lambda bb, tt: (bb, tt, 0, 0, 0))],
        out_shape=[jax.ShapeDtypeStruct((b, d, t), BF16), jax.ShapeDtypeStruct((b, t, d), BF16),
                   jax.ShapeDtypeStruct((b, nt, heads, vrows, ROW_TILE), BF16)],
        compiler_params=_params(("parallel", "parallel"), VMEM_LIMIT_BYTES),
        name="attn_in",
    )(xs, g.reshape(1, d), modsel, w_qkv.astype(BF16), cos, sin)


def _attn_body(lam_ref, qt_ref, k_ref, vt_ref, o_ref, sa_ref, sb_ref, sc_ref, *, nct, lambda_init, unroll, n_blocks):
    qi = pl.program_id(2)
    dh2 = 2 * DA_HEAD
    heads = vt_ref.shape[2]
    tq = qt_ref.shape[-1]
    tk = vt_ref.shape[-1]
    zero = jnp.zeros((DA_HEAD, tq), qt_ref.dtype)
    qs = []
    for h in range(heads):
        qt = qt_ref[0, h * dh2:(h + 1) * dh2, :]
        qs.append(jnp.concatenate([jnp.concatenate([qt[0:DA_HEAD], zero], axis=1),
                                   jnp.concatenate([zero, qt[DA_HEAD:dh2]], axis=1)], axis=0))
    lam = lam_ref[...]
    lam_full = (jnp.exp(jnp.sum(lam[0:1] * lam[1:2], axis=-1, keepdims=True))
                - jnp.exp(jnp.sum(lam[2:3] * lam[3:4], axis=-1, keepdims=True)) + lambda_init)

    def scores(h, j):
        return jnp.dot(k_ref[0, j * tk:(j + 1) * tk, h * dh2:(h + 1) * dh2], qs[h],
                       preferred_element_type=F32)

    def weigh(h, j, sj, m_new):
        return jnp.dot(vt_ref[0, j, h], jnp.exp2(sj - m_new).astype(BF16), preferred_element_type=F32)

    def col_max(s):
        return jnp.max(s, axis=0, keepdims=True)

    def finish(h, acc):
        o = acc[0:dh2] / acc[dh2:dh2 + 1]
        o_ref[0, :, h * dh2:(h + 1) * dh2] = (o[:, 0:tq] - lam_full * o[:, tq:2 * tq]).T

    @pl.when(qi < nct)
    def _():
        for h in range(heads):
            s_ctx = [scores(h, j) for j in range(nct)]
            top = functools.reduce(jnp.maximum, [col_max(sj) for sj in s_ctx])
            finish(h, sum(weigh(h, j, s_ctx[j], top) for j in range(nct)))

    @pl.when(qi >= nct)
    def _():
        bufs = (sa_ref, sb_ref, sc_ref)
        per_head = [list(range(nct + unroll))] + [list(range(nct + unroll * b, nct + unroll * (b + 1)))
                                                  for b in range(1, n_blocks)]
        blocks = [(h, chunks, i == 0, i == n_blocks - 1) for h in range(heads) for i, chunks in enumerate(per_head)]
        n = len(blocks)

        def fill(g, u):
            bufs[g % 3][u] = scores(blocks[g][0], blocks[g][1][u])

        for u in range(len(blocks[0][1])):
            fill(0, u)
        top = functools.reduce(jnp.maximum, [col_max(bufs[0][u]) for u in range(len(blocks[0][1]))])
        if n > 1:
            for u in range(len(blocks[1][1])):
                fill(1, u)
        m_run, acc = None, None
        for g in range(n):
            h, chunks, first, last = blocks[g]
            if first:
                m_new, acc = top, None
            else:
                m_new = jnp.maximum(m_run, top)
                acc = acc * jnp.exp2(m_run - m_new)
            next_top = None
            for u in range(max(len(chunks), len(blocks[g + 1][1]) if g + 1 < n else 0,
                               len(blocks[g + 2][1]) if g + 2 < n else 0)):
                if g + 2 < n and u < len(blocks[g + 2][1]):
                    fill(g + 2, u)
                if u < len(chunks):
                    part = weigh(h, chunks[u], bufs[g % 3][u], m_new)
                    acc = part if acc is None else acc + part
                if g + 1 < n and u < len(blocks[g + 1][1]):
                    top_u = col_max(bufs[(g + 1) % 3][u])
                    next_top = top_u if next_top is None else jnp.maximum(next_top, top_u)
            if last:
                finish(h, acc)
            m_run, top = m_new, next_top


def _attn(qt, k, vt, lam, lambda_init, nct):
    b, t, d = k.shape
    dh2 = 2 * DA_HEAD
    nt, vrows = vt.shape[1], vt.shape[3]
    n_lat = nt - nct
    unroll = max(u for u in range(1, ATT_UNROLL + 1) if n_lat % u == 0)
    hp = ATT_HEADS
    body = functools.partial(_attn_body, nct=nct, lambda_init=lambda_init, unroll=unroll,
                             n_blocks=n_lat // unroll)
    score_buf = pltpu.VMEM((nct + unroll, ROW_TILE, 2 * ROW_TILE), F32)
    return pl.pallas_call(
        body,
        grid=(b, d // (hp * dh2), t // ROW_TILE),
        in_specs=[pl.BlockSpec(lam.shape, lambda bb, hh, qq: (0, 0)),
                  pl.BlockSpec((1, hp * dh2, ROW_TILE), lambda bb, hh, qq: (bb, hh, qq)),
                  pl.BlockSpec((1, t, hp * dh2), lambda bb, hh, qq: (bb, 0, hh)),
                  pl.BlockSpec((1, nt, hp, vrows, ROW_TILE), lambda bb, hh, qq: (bb, 0, hh, 0, 0))],
        out_specs=pl.BlockSpec((1, ROW_TILE, hp * dh2), lambda bb, hh, qq: (bb, qq, hh)),
        out_shape=jax.ShapeDtypeStruct((b, t, d), F32),
        scratch_shapes=[score_buf, score_buf, score_buf],
        compiler_params=_params(("parallel", "parallel", "parallel"), VMEM_LIMIT_BYTES),
        name="diff_attn",
    )(lam, qt, k, vt)


def _attn_post_body(a_ref, x_ref, m_ref, ng_ref, wout_ref, o_ref, *, heads, scale):
    a = a_ref[0]
    dh2 = a.shape[-1] // heads
    parts = []
    for h in range(heads):
        ah = a[:, h * dh2:(h + 1) * dh2]
        parts.append(ah * lax.rsqrt(jnp.mean(ah * ah, axis=-1, keepdims=True) + DA_EPS))
    an = jnp.concatenate(parts, axis=-1) * ng_ref[...] * scale
    o_ref[0] = x_ref[0] + m_ref[0, 0][2:3] * _mm(an, wout_ref[...])


def _attn_post(a, xs, modsel, norm_g, w_out, lambda_init, nct):
    b, t, d = xs.shape
    body = functools.partial(_attn_post_body, heads=DA_HEADS, scale=1.0 - lambda_init)
    return pl.pallas_call(
        body,
        grid=(b, t // ROW_TILE),
        in_specs=[_row_spec(d), _row_spec(d), _mod_spec(d, nct), _full_spec((1, d)), _full_spec((d, d))],
        out_specs=_row_spec(d),
        out_shape=jax.ShapeDtypeStruct((b, t, d), F32),
        compiler_params=_params(("parallel", "parallel"), VMEM_LIMIT_BYTES),
        name="attn_post",
    )(a, xs, modsel, jnp.tile(norm_g, DA_HEADS).reshape(1, d), w_out.astype(BF16))


def _rope_tables(seq, ctx_len):
    rows = seq // GRID_W
    row = jnp.repeat(jnp.arange(rows), GRID_W).astype(F32)
    col = jnp.tile(jnp.arange(GRID_W), rows).astype(F32)
    nf = DA_HEAD // 4
    inv = jnp.power(ROPE_BASE, -jnp.arange(nf, dtype=F32) / nf)
    ang = jnp.concatenate([row[:, None] * inv, col[:, None] * inv], axis=-1)
    cos, sin = jnp.cos(ang), jnp.sin(ang)
    cos = jnp.concatenate([jnp.ones((ctx_len, DA_HEAD // 2), F32), cos], axis=0)
    sin = jnp.concatenate([jnp.zeros((ctx_len, DA_HEAD // 2), F32), sin], axis=0)
    reps = LANES // DA_HEAD
    return (jnp.tile(jnp.concatenate([cos, cos], axis=-1), (1, reps)),
            jnp.tile(jnp.concatenate([-sin, sin], axis=-1), (1, reps)))


def _attn_layer(xs, n1g, modsel, p, lambda_init, nct, ctx_len):
    cos, sin = _rope_tables(xs.shape[1] - ctx_len, ctx_len)
    qt, k, vt = _attn_in(xs, n1g, modsel, p["w_qkv"], cos, sin, nct)
    a = _attn(qt, k, vt, p["lam"].astype(F32), lambda_init, nct)
    return _attn_post(a, xs, modsel, p["norm_g"], p["w_out"], lambda_init, nct)


def kernel(x, c, ctx, c_ctx, ada_w, ada_b, norm1_g, norm2_g, ffn_w_in, ffn_conv_w, ffn_conv_b, ffn_w_out,
           ra_mix, ra_w_rkv, ra_w0, ra_w1, ra_w2, ra_a0, ra_a1, ra_a2, ra_g1, ra_g2, ra_k_k, ra_k_a, ra_r_k,
           ra_lnx_g, ra_lnx_b, ra_w_out, ml_w_in, ml_b_in, ml_norm_g, ml_w_out, da_w_qkv, da_lambda,
           da_norm_g, da_w_out, final_g):
    batch, seq, d = x.shape
    ctx_len = ctx.shape[1]
    assert ctx_len % ROW_TILE == 0 and seq % ROW_TILE == 0 and batch + 1 <= SUBLANES
    nct = ctx_len // ROW_TILE
    depth = ada_w.shape[0]

    xs = jnp.concatenate([ctx, x], axis=1)
    cc = jnp.concatenate([c_ctx[None], c, jnp.zeros((SUBLANES - 1 - batch, d), F32)], axis=0)
    mod = _ada(cc, ada_w, ada_b).reshape(depth, SUBLANES, 6, d)
    modsel = jnp.stack([jnp.broadcast_to(mod[:, 0:1], (depth, batch, 6, d)), mod[:, 1:1 + batch]], axis=2)

    head_id = jnp.arange(d) // RW_HEAD
    ones_bd = (head_id[:, None] == head_id[None, :]).astype(BF16)

    for i in range(depth):
        kind, j = i % N_MIXERS, i // N_MIXERS
        if kind == 0:
            p = dict(mix=ra_mix[j], w_rkv=ra_w_rkv[j], w0=ra_w0[j], w1=ra_w1[j], w2=ra_w2[j], a0=ra_a0[j],
                     a1=ra_a1[j], a2=ra_a2[j], g1=ra_g1[j], g2=ra_g2[j], k_k=ra_k_k[j], k_a=ra_k_a[j],
                     r_k=ra_r_k[j], lnx_g=ra_lnx_g[j], lnx_b=ra_lnx_b[j], w_out=ra_w_out[j])
            xs = _rwkv_layer(xs, norm1_g[i], modsel[i], p, ones_bd, nct, ctx_len)
        elif kind == 1:
            p = dict(w_in=ml_w_in[j], b_in=ml_b_in[j], norm_g=ml_norm_g[j], w_out=ml_w_out[j])
            xs = _mlstm_layer(xs, norm1_g[i], modsel[i], p, nct, ctx_len)
        else:
            lambda_init = 0.8 - 0.6 * math.exp(-0.3 * i)
            p = dict(w_qkv=da_w_qkv[j], lam=da_lambda[j], norm_g=da_norm_g[j], w_out=da_w_out[j])
            xs = _attn_layer(xs, norm1_g[i], modsel[i], p, lambda_init, nct, ctx_len)
        xs = _ffn_call(xs, norm2_g[i], modsel[i], ffn_w_in[i], ffn_conv_w[i], ffn_conv_b[i], ffn_w_out[i], nct,
                       final_g=final_g if i == depth - 1 else None)
    return xs
```

```python
import functools
import math

import jax
import jax.numpy as jnp
from jax import lax
from jax.experimental import pallas as pl
from jax.experimental.pallas import tpu as pltpu

F32 = jnp.float32
BF16 = jnp.bfloat16

DEPTH = 4
N_MIXERS = 3
GRID_W = 64
NORM_EPS = 1e-6
RW_HEAD = 64
RW_LNX_EPS = 6.4e-4
ML_HEADS = 4
ML_CHUNK = 128
DA_HEADS = 8
DA_HEAD = 64
DA_EPS = 1e-5
ROPE_BASE = 10000.0

LANES = 128
SUBLANES = 8
VMEM_LIMIT_BYTES = 56 * 1024 * 1024

ROW_TILE = 256
RW_CHUNK = 64
RW_GROUP = 4
FFN_COLS = 256
ATT_UNROLL = 4
ATT_ONES = 16
ATT_HEADS = 4


def _params(sem, vmem=None):
    return pltpu.CompilerParams(dimension_semantics=sem, vmem_limit_bytes=vmem)


def _mm(a, b):
    return jnp.dot(a.astype(BF16), b.astype(BF16), preferred_element_type=F32)


def _mm_nt(a, b):
    return lax.dot_general(a.astype(BF16), b.astype(BF16), (((1,), (1,)), ((), ())),
                           preferred_element_type=F32)


def _mm_tn(a, b):
    return lax.dot_general(a.astype(BF16), b.astype(BF16), (((0,), (0,)), ((), ())),
                           preferred_element_type=F32)


def _split3(x):
    hi = x.astype(BF16)
    r1 = x - hi.astype(F32)
    mid = r1.astype(BF16)
    lo = (r1 - mid.astype(F32)).astype(BF16)
    return hi, mid, lo


def _exact_left(m_bf16, x, terms=3):
    return sum(jnp.dot(m_bf16, p, preferred_element_type=F32) for p in _split3(x)[:terms])


def _normmod(x, g, sc, sh):
    y = x * lax.rsqrt(jnp.mean(x * x, axis=-1, keepdims=True) + NORM_EPS)
    return y * g * (1.0 + sc) + sh


def _ada_body(c_ref, w_ref, b_ref, o_ref):
    a = c_ref[...]
    a = a * jax.nn.sigmoid(a)
    o_ref[0] = jnp.dot(a, w_ref[0], precision=lax.Precision.HIGHEST,
                       preferred_element_type=F32) + b_ref[0]


def _ada(cc, ada_w, ada_b):
    depth, d, n = ada_w.shape
    nb = n // 4
    return pl.pallas_call(
        _ada_body,
        grid=(depth, n // nb),
        in_specs=[pl.BlockSpec((SUBLANES, d), lambda l, j: (0, 0)),
                  pl.BlockSpec((1, d, nb), lambda l, j: (l, 0, j)),
                  pl.BlockSpec((1, 1, nb), lambda l, j: (l, 0, j))],
        out_specs=pl.BlockSpec((1, SUBLANES, nb), lambda l, j: (l, 0, j)),
        out_shape=jax.ShapeDtypeStruct((depth, SUBLANES, n), F32),
        compiler_params=_params(("arbitrary", "arbitrary"), VMEM_LIMIT_BYTES),
        name="ada_mod",
    )(cc, ada_w, ada_b.reshape(depth, 1, n))


def _row_spec(d, tm=ROW_TILE):
    return pl.BlockSpec((1, tm, d), lambda b, t: (b, t, 0))


def _mod_spec(d, nct):
    return pl.BlockSpec((1, 1, 6, d), lambda b, t: (b, jnp.where(t >= nct, 1, 0), 0, 0))


def _full_spec(shape):
    nd = len(shape)
    return pl.BlockSpec(shape, lambda b, t: (0,) * nd)


def _halo_specs(d, tm, n_rows):
    per = tm // SUBLANES
    last = n_rows // SUBLANES - 1
    prev = pl.BlockSpec((1, SUBLANES, d), lambda b, t: (b, jnp.maximum(t * per - 1, 0), 0))
    nxt = pl.BlockSpec((1, SUBLANES, d), lambda b, t: (b, jnp.minimum((t + 1) * per, last), 0))
    return prev, nxt


def _ffn_body(x_ref, xp_ref, xn_ref, g_ref, m_ref, win_ref, cw_ref, cb_ref, wout_ref, fg_ref, o_ref, act_ref,
              *, nct, nt, t0, hidden, final):
    t = pl.program_id(1) + t0
    x = x_ref[0]
    tm = x.shape[0]
    m = m_ref[0, 0]
    xa = jnp.concatenate([xp_ref[0], x, xn_ref[0]], axis=0)
    h32 = _normmod(xa, g_ref[...], m[4:5], m[3:4])
    ha = h32.astype(BF16)
    hm = h32[SUBLANES:SUBLANES + tm].astype(BF16)
    first = jnp.logical_or(t == 0, t == nct)
    last = jnp.logical_or(t == nct - 1, t == nt - 1)
    rows = lax.broadcasted_iota(jnp.int32, (tm + 2 * SUBLANES, 1), 0)
    dead = jnp.logical_or(rows == jnp.where(first, SUBLANES - 1, -1),
                          rows == jnp.where(last, tm + SUBLANES, -1))
    keep = jnp.where(dead, 0.0, 1.0)
    for j in range(hidden // FFN_COLS):
        lo, hi = j * FFN_COLS, (j + 1) * FFN_COLS
        val = jnp.dot(hm, win_ref[:, lo:hi], preferred_element_type=F32)
        gate = jnp.dot(ha, win_ref[:, hidden + lo:hidden + hi], preferred_element_type=F32) * keep
        g_prev = pltpu.roll(gate, 1, 0)[SUBLANES:SUBLANES + tm]
        g_next = pltpu.roll(gate, tm + 2 * SUBLANES - 1, 0)[SUBLANES:SUBLANES + tm]
        g_mid = gate[SUBLANES:SUBLANES + tm]
        cw = cw_ref[:, lo:hi]
        conv = cw[0:1] * g_prev + cw[1:2] * g_mid + cw[2:3] * g_next + cb_ref[:, lo:hi]
        act_ref[:, lo:hi] = (jax.nn.gelu(conv) * val).astype(BF16)
    out = jnp.dot(act_ref[...], wout_ref[...], preferred_element_type=F32)
    y = x + m[5:6] * out
    if final:
        y = y * lax.rsqrt(jnp.mean(y * y, axis=-1, keepdims=True) + NORM_EPS) * fg_ref[...]
    o_ref[0] = y


def _ffn_call(xs, g, modsel, w_in, conv_w, conv_b, w_out, nct, final_g=None):
    b, t, d = xs.shape
    hidden = w_out.shape[0]
    nt = t // ROW_TILE
    final = final_g is not None
    t0 = nct if final else 0
    per = ROW_TILE // SUBLANES
    last = t // SUBLANES - 1
    body = functools.partial(_ffn_body, nct=nct, nt=nt, t0=t0, hidden=hidden, final=final)
    return pl.pallas_call(
        body,
        grid=(b, nt - t0),
        in_specs=[pl.BlockSpec((1, ROW_TILE, d), lambda bb, tt: (bb, tt + t0, 0)),
                  pl.BlockSpec((1, SUBLANES, d), lambda bb, tt: (bb, jnp.maximum((tt + t0) * per - 1, 0), 0)),
                  pl.BlockSpec((1, SUBLANES, d), lambda bb, tt: (bb, jnp.minimum((tt + t0 + 1) * per, last), 0)),
                  _full_spec((1, d)),
                  pl.BlockSpec((1, 1, 6, d), lambda bb, tt: (bb, jnp.where(tt + t0 >= nct, 1, 0), 0, 0)),
                  _full_spec((d, 2 * hidden)), _full_spec((3, hidden)), _full_spec((1, hidden)),
                  _full_spec((hidden, d)), _full_spec((1, d))],
        out_specs=_row_spec(d),
        out_shape=jax.ShapeDtypeStruct((b, t - t0 * ROW_TILE, d), F32),
        scratch_shapes=[pltpu.VMEM((ROW_TILE, hidden), BF16)],
        compiler_params=_params(("parallel", "parallel"), VMEM_LIMIT_BYTES),
        name="conv_ffn",
    )(xs, xs, xs, g.reshape(1, d), modsel, w_in.astype(BF16), conv_w, conv_b.reshape(1, hidden),
      w_out.astype(BF16), (final_g if final else g).reshape(1, d))


def _seg_sum(x, ones_bd):
    return jnp.dot(x.astype(BF16), ones_bd, preferred_element_type=F32)


def _rwkv_prep_body(x_ref, xp_ref, xn_ref, ng_ref, m_ref, mix_ref, wrkv_ref, w1_ref, a1_ref, g1_ref, w2_ref, a2_ref,
                    g2_ref, vec_ref, ones_ref,
                    r_ref, v_ref, kk_ref, g_ref, bonus_ref, lw_ref, kd_ref, ka_ref, *, nct, nt):
    t = pl.program_id(1)
    m = m_ref[0, 0]
    h = _normmod(x_ref[0], ng_ref[...], m[1:2], m[0:1])
    tm = h.shape[0]
    first = jnp.logical_or(t == 0, t == nct)
    last = jnp.logical_or(t == nct - 1, t == nt - 1)
    p_row = _normmod(xp_ref[0], ng_ref[...], m[1:2], m[0:1])[SUBLANES - 1:SUBLANES] * jnp.where(first, 0.0, 1.0)
    n_row = _normmod(xn_ref[0], ng_ref[...], m[1:2], m[0:1])[0:1] * jnp.where(last, 0.0, 1.0)
    rows = lax.broadcasted_iota(jnp.int32, (tm, 1), 0)
    h_prev = jnp.where(rows == 0, p_row, pltpu.roll(h, 1, 0))
    h_next = jnp.where(rows == tm - 1, n_row, pltpu.roll(h, tm - 1, 0))
    xx = 0.5 * (h_prev + h_next) - h
    mix = mix_ref[...]
    vec = vec_ref[...]
    ones_bd = ones_ref[...]

    def shifted(j):
        return h + xx * mix[j:j + 1]

    r = _mm(shifted(0), wrkv_ref[0])
    k = _mm(shifted(1), wrkv_ref[1])
    v = _mm(shifted(2), wrkv_ref[2])
    w_in = jnp.tanh(_mm(shifted(3), w1_ref[...]))
    a_in = _mm(shifted(4), a1_ref[...])
    g = _mm(jax.nn.sigmoid(_mm(shifted(5), g1_ref[...])), g2_ref[...])

    kk = k * vec[4:5]
    kk = kk / jnp.maximum(jnp.sqrt(_seg_sum(kk * kk, ones_bd)), 1e-12)
    kd_sum = None
    for d in range(2):
        w_pre = vec[d:d + 1] + _mm(w_in, w2_ref[d])
        lw_ref[d, 0] = -jax.nn.sigmoid(w_pre) * math.exp(-0.5)
        a = jax.nn.sigmoid(vec[2 + d:3 + d] + _mm(a_in, a2_ref[d]))
        kd = k * (1.0 + (a - 1.0) * vec[5:6])
        kd_ref[d, 0] = kd.astype(kd_ref.dtype)
        ka_ref[d, 0] = (kk * a).astype(ka_ref.dtype)
        kd_sum = kd if kd_sum is None else kd_sum + kd
    r_ref[0] = r.astype(r_ref.dtype)
    v_ref[0] = v.astype(v_ref.dtype)
    kk_ref[0] = kk.astype(kk_ref.dtype)
    g_ref[0] = g.astype(g_ref.dtype)
    bonus_ref[0] = (_seg_sum(r * kd_sum * vec[6:7], ones_bd) * v).astype(bonus_ref.dtype)


def _rwkv_prep(xs, ng, modsel, mix, w_rkv, w0, w1, w2, a0, a1, a2, g1, g2, k_k, k_a, r_k, ones_bd, nct):
    b, t, d = xs.shape
    nt = t // ROW_TILE
    lora = w1.shape[-1]
    prev, nxt = _halo_specs(d, ROW_TILE, t)
    zeros = jnp.zeros((lora, d), F32)
    w2p = jnp.stack([jnp.concatenate([w2[0], zeros], 0), jnp.concatenate([zeros, w2[1]], 0)]).astype(BF16)
    a2p = jnp.stack([jnp.concatenate([a2[0], zeros], 0), jnp.concatenate([zeros, a2[1]], 0)]).astype(BF16)
    w1c = jnp.concatenate([w1[0], w1[1]], axis=1).astype(BF16)
    a1c = jnp.concatenate([a1[0], a1[1]], axis=1).astype(BF16)
    vec = jnp.stack([w0[0], w0[1], a0[0], a0[1], k_k, k_a, r_k.reshape(d), jnp.zeros((d,), F32)])
    shared = jax.ShapeDtypeStruct((b, t, d), BF16)
    directional = jax.ShapeDtypeStruct((2, b, t, d), BF16)
    log_decay = jax.ShapeDtypeStruct((2, b, t, d), F32)
    dir_spec = pl.BlockSpec((2, 1, ROW_TILE, d), lambda bb, tt: (0, bb, tt, 0))
    body = functools.partial(_rwkv_prep_body, nct=nct, nt=nt)
    return pl.pallas_call(
        body,
        grid=(b, nt),
        in_specs=[_row_spec(d), prev, nxt, _full_spec((1, d)), _mod_spec(d, nct), _full_spec((6, d)),
                  _full_spec((3, d, d)), _full_spec(w1c.shape), _full_spec(a1c.shape), _full_spec(g1.shape),
                  _full_spec(w2p.shape), _full_spec(a2p.shape), _full_spec(g2.shape),
                  _full_spec((SUBLANES, d)), _full_spec((d, d))],
        out_specs=[_row_spec(d)] * 5 + [dir_spec] * 3,
        out_shape=[shared] * 5 + [log_decay, directional, directional],
        compiler_params=_params(("parallel", "parallel"), VMEM_LIMIT_BYTES),
        name="rwkv_prep",
    )(xs, xs, xs, ng.reshape(1, d), modsel, mix, w_rkv.astype(BF16), w1c, a1c, g1.astype(BF16), w2p, a2p, g2.astype(BF16), vec,
      ones_bd)


def _rwkv_scan_body(rf_ref, vf_ref, kkf_ref, lwf_ref, kdf_ref, kaf_ref, rb_ref, vb_ref, kkb_ref, lwb_ref, kdb_ref,
                    kab_ref, yf_ref, yb_ref, s_ref, *, chunk, group):
    @pl.when(pl.program_id(0) == 0)
    def _():
        s_ref[...] = jnp.zeros_like(s_ref)

    c, gc, width = chunk, group * chunk, group * RW_HEAD
    pr = lax.broadcasted_iota(jnp.int32, (2 * c, 2 * gc), 0)
    pc = lax.broadcasted_iota(jnp.int32, (2 * c, 2 * gc), 1)
    wr = lax.broadcasted_iota(jnp.int32, (c, gc), 0)
    wc = lax.broadcasted_iota(jnp.int32, (c, gc), 1)
    eye_w = jnp.where(wr == wc % c, 1.0, 0.0)
    tr = lax.broadcasted_iota(jnp.int32, (c, c), 0)
    tc = lax.broadcasted_iota(jnp.int32, (c, c), 1)
    own = (lax.broadcasted_iota(jnp.int32, (gc, width), 0) // c
           == lax.broadcasted_iota(jnp.int32, (gc, width), 1) // RW_HEAD)
    same_head = (lax.broadcasted_iota(jnp.int32, (width, width), 0) // RW_HEAD
                 == lax.broadcasted_iota(jnp.int32, (width, width), 1) // RW_HEAD)

    def stack(x):
        return jnp.where(own, jnp.concatenate([x] * group, axis=0), 0.0).astype(BF16)

    def direction(di, r_ref, v_ref, kk_ref, lw_ref, kd_ref, ka_ref, y_ref):
        sgn = 1 - 2 * di
        rows = range(r_ref.shape[0])
        chains = [(bi, hg) for bi in rows for hg in range(r_ref.shape[-1] // width)]
        lns = [slice(hg * width, (hg + 1) * width) for _, hg in chains]
        gs = range(len(chains))
        st = {}

        def prepare():
            tri = jnp.where((tr - tc) * sgn >= 0, 1.0, 0.0).astype(BF16)
            lw_all = [lw_ref[0, bi] for bi in rows]
            cum_all = [_exact_left(tri, lw_all[bi], terms=2) for bi in rows]
            tot_all = [jnp.sum(lw_all[bi], axis=0, keepdims=True) for bi in rows]
            st["left"], st["right"], st["carry_in"], st["v_nat"], st["vm"], st["decay"] = [], [], [], [], [], []
            for (bi, _), ln in zip(chains, lns):
                lw, cum, tot = lw_all[bi][:, ln], cum_all[bi][:, ln], tot_all[bi][:, ln]
                e_neg = jnp.exp(-cum)
                e_out = jnp.exp(tot - cum)
                kk, ka, kd = (ref.astype(F32) for ref in (kk_ref[bi, :, ln], ka_ref[0, bi, :, ln], kd_ref[0, bi, :, ln]))
                st["left"].append(jnp.concatenate([kk * jnp.exp(cum - lw),
                                                   r_ref[bi, :, ln].astype(F32) * jnp.exp(cum)], axis=0).astype(BF16))
                st["right"].append(jnp.concatenate([stack(ka * e_neg), stack(kd * e_neg)], axis=0))
                st["carry_in"].append(jnp.concatenate([ka * e_out, kd * e_out], axis=0).astype(BF16))
                st["v_nat"].append(v_ref[bi, :, ln].astype(BF16))
                st["vm"].append(stack(v_ref[bi, :, ln].astype(F32)))
                st["decay"].append(jnp.exp(tot))

        def pair_products():
            visible = ((pr % c) - (pc % c)) * sgn > jnp.where(pr < c, 0, -1)
            st["pair"] = [jnp.where(visible, _mm_nt(st["left"][g], st["right"][g]), 0.0) for g in gs]
            st["inv"] = [eye_w - st["pair"][g][0:c, 0:gc] for g in gs]
            st["xp"] = [_mm(st["pair"][g][0:c, 0:gc], stack(st["pair"][g][0:c, 0:gc])) for g in gs]

        def double():
            both = [_mm(jnp.concatenate([st["inv"][g], st["xp"][g]], axis=0), stack(st["xp"][g])) for g in gs]
            st["inv"] = [st["inv"][g] + both[g][0:c] for g in gs]
            st["xp"] = [both[g][c:2 * c] for g in gs]

        def last_double():
            st["inv"] = [st["inv"][g] + _mm(st["inv"][g], stack(st["xp"][g])) for g in gs]

        def residual():
            st["resid"] = [(eye_w - st["inv"][g]) - _mm(st["pair"][g][0:c, 0:gc], stack(st["inv"][g])) for g in gs]

        def newton():
            st["inv"] = [st["inv"][g] + _mm(st["inv"][g], stack(st["resid"][g])) for g in gs]

        def read_state():
            st["s"] = [s_ref[di, chains[g][0], chains[g][1]] for g in gs]
            st["from_state"] = [_mm_nt(st["left"][g], st["s"][g]) for g in gs]
            st["from_v"] = [_mm(st["pair"][g][:, gc:2 * gc], st["vm"][g]) for g in gs]

        def solve():
            st["u"] = [-_mm(st["inv"][g], stack(st["from_state"][g][0:c] + st["from_v"][g][0:c])) for g in gs]

        def emit():
            for g in gs:
                bi, hg = chains[g]
                y_ref[bi, :, lns[g]] = (st["from_state"][g][c:2 * c] + st["from_v"][g][c:2 * c]
                                        + _mm(st["pair"][g][c:2 * c, 0:gc], stack(st["u"][g]))).astype(y_ref.dtype)
                grown = _mm_tn(jnp.concatenate([st["u"][g].astype(BF16), st["v_nat"][g]], axis=0),
                               st["carry_in"][g])
                s_ref[di, bi, hg] = st["s"][g] * st["decay"][g] + jnp.where(same_head, grown, 0.0)

        return ([prepare, pair_products] + [double] * (int(math.log2(c)) - 3)
                + [last_double, residual, newton, read_state, solve, emit])

    forward = direction(0, rf_ref, vf_ref, kkf_ref, lwf_ref, kdf_ref, kaf_ref, yf_ref)
    backward = direction(1, rb_ref, vb_ref, kkb_ref, lwb_ref, kdb_ref, kab_ref, yb_ref)
    for stage_f, stage_b in zip(forward, backward):
        stage_f()
        stage_b()


def _rwkv_scan(r, v, kk, lw, kd, ka, ctx_len):
    b, t, d = r.shape
    c = RW_CHUNK
    width = RW_GROUP * RW_HEAD
    nc, nck = t // c, ctx_len // c

    def back(i):
        return jnp.where(i < nck, nck - 1 - i, nc - 1 - (i - nck))

    fwd = pl.BlockSpec((b, c, d), lambda i: (0, i, 0))
    bwd = pl.BlockSpec((b, c, d), lambda i: (0, back(i), 0))
    fwd_dir = pl.BlockSpec((1, b, c, d), lambda i: (0, 0, i, 0))
    bwd_dir = pl.BlockSpec((1, b, c, d), lambda i: (1, 0, back(i), 0))
    body = functools.partial(_rwkv_scan_body, chunk=c, group=RW_GROUP)
    return pl.pallas_call(
        body,
        grid=(nc,),
        in_specs=[fwd, fwd, fwd, fwd_dir, fwd_dir, fwd_dir, bwd, bwd, bwd, bwd_dir, bwd_dir, bwd_dir],
        out_specs=[fwd, bwd],
        out_shape=[jax.ShapeDtypeStruct((b, t, d), BF16)] * 2,
        scratch_shapes=[pltpu.VMEM((2, b, d // width, width, width), F32)],
        compiler_params=_params(("arbitrary",), VMEM_LIMIT_BYTES),
        name="rwkv_scan",
    )(r, v, kk, lw, kd, ka, r, v, kk, lw, kd, ka)


def _rwkv_post_body(yf_ref, yb_ref, bonus_ref, g_ref, x_ref, m_ref, lng_ref, lnb_ref, ones_ref, wout_ref, o_ref):
    ones_bd = ones_ref[...]
    y = yf_ref[0].astype(F32) + yb_ref[0].astype(F32)
    mu = _seg_sum(y, ones_bd) * (1.0 / RW_HEAD)
    yc = y - mu
    var = _seg_sum(yc * yc, ones_bd) * (1.0 / RW_HEAD)
    yn = yc * lax.rsqrt(var + RW_LNX_EPS) * lng_ref[...] + lnb_ref[...]
    out = _mm((yn + bonus_ref[0].astype(F32)) * g_ref[0].astype(F32), wout_ref[...])
    o_ref[0] = x_ref[0] + m_ref[0, 0][2:3] * out


def _rwkv_post(yf, yb, bonus, g, xs, modsel, lnx_g, lnx_b, ones_bd, w_out, nct):
    b, t, d = xs.shape
    return pl.pallas_call(
        _rwkv_post_body,
        grid=(b, t // ROW_TILE),
        in_specs=[_row_spec(d)] * 5 + [_mod_spec(d, nct), _full_spec((1, d)), _full_spec((1, d)),
                                      _full_spec((d, d)), _full_spec((d, d))],
        out_specs=_row_spec(d),
        out_shape=jax.ShapeDtypeStruct((b, t, d), F32),
        compiler_params=_params(("parallel", "parallel"), VMEM_LIMIT_BYTES),
        name="rwkv_post",
    )(yf, yb, bonus, g, xs, modsel, lnx_g.reshape(1, d), lnx_b.reshape(1, d), ones_bd, w_out.astype(BF16))


def _rwkv_layer(xs, n1g, modsel, p, ones_bd, nct, ctx_len):
    r, v, kk, g, bonus, lw, kd, ka = _rwkv_prep(
        xs, n1g, modsel, p["mix"], p["w_rkv"], p["w0"], p["w1"], p["w2"], p["a0"], p["a1"], p["a2"], p["g1"], p["g2"],
        p["k_k"], p["k_a"], p["r_k"], ones_bd, nct)
    yf, yb = _rwkv_scan(r, v, kk, lw, kd, ka, ctx_len)
    return _rwkv_post(yf, yb, bonus, g, xs, modsel, p["lnx_g"], p["lnx_b"], ones_bd, p["w_out"], nct)


def _mlstm_in_body(x_ref, g_ref, m_ref, w_ref, b_ref, q_ref, k_ref, v_ref, o_ref, gf_ref, gb_ref,
                   *, heads, dk, dv):
    m = m_ref[0, 0]
    h = _normmod(x_ref[0], g_ref[...], m[1:2], m[0:1])
    z = _mm(h, w_ref[...]) + b_ref[...]
    nqk = heads * dk
    nv = heads * dv
    q_ref[0] = z[:, 0:nqk].astype(q_ref.dtype)
    k_ref[0] = (z[:, nqk:2 * nqk] * (dk ** -0.5)).astype(k_ref.dtype)
    v_ref[0] = z[:, 2 * nqk:2 * nqk + nv].astype(v_ref.dtype)
    d_model = o_ref.shape[-1]
    o_ref[0] = jax.nn.sigmoid(z[:, 2 * nqk + nv:2 * nqk + nv + d_model]).astype(o_ref.dtype)
    gates = z[:, 2 * nqk + nv + d_model:]
    lane = lax.broadcasted_iota(jnp.int32, (1, LANES), 1)
    is_forget = jnp.logical_and(lane >= heads, lane < 2 * heads)
    for ref, off in ((gf_ref, 0), (gb_ref, LANES)):
        gd = gates[:, off:off + LANES]
        ref[0] = jnp.where(is_forget, jax.nn.log_sigmoid(gd), gd)


def _mlstm_in(xs, g, modsel, w_in, b_in, nct):
    b, t, d = xs.shape
    heads, dk, dv = ML_HEADS, d // (2 * ML_HEADS), d // ML_HEADS
    main = 2 * heads * dk + heads * dv + d
    pad = jnp.zeros((d, LANES - 2 * heads), F32)
    w_cat = jnp.concatenate([w_in[:, :main], w_in[:, main:main + 2 * heads], pad,
                             w_in[:, main + 2 * heads:], pad], axis=1)
    padb = jnp.zeros((LANES - 2 * heads,), F32)
    b_cat = jnp.concatenate([b_in[:main], b_in[main:main + 2 * heads], padb, b_in[main + 2 * heads:], padb])
    n = w_cat.shape[1]
    body = functools.partial(_mlstm_in_body, heads=heads, dk=dk, dv=dv)
    shapes = [(heads * dk, BF16), (heads * dk, BF16), (heads * dv, BF16), (d, BF16), (LANES, F32), (LANES, F32)]
    return pl.pallas_call(
        body,
        grid=(b, t // ROW_TILE),
        in_specs=[_row_spec(d), _full_spec((1, d)), _mod_spec(d, nct), _full_spec((d, n)), _full_spec((1, n))],
        out_specs=[_row_spec(s) for s, _ in shapes],
        out_shape=[jax.ShapeDtypeStruct((b, t, s), dt) for s, dt in shapes],
        compiler_params=_params(("parallel", "parallel"), VMEM_LIMIT_BYTES),
        name="mlstm_in",
    )(xs, g.reshape(1, d), modsel, w_cat.astype(BF16), b_cat.reshape(1, n))


def _mlstm_chunk_body(qf_ref, kf_ref, vf_ref, gcf_ref, grf_ref, qb_ref, kb_ref, vb_ref, gcb_ref, grb_ref,
                      hf_ref, hb_ref, c_ref, n_ref, m_ref, *, heads, dk, dv):
    @pl.when(pl.program_id(1) == 0)
    def _():
        c_ref[...] = jnp.zeros_like(c_ref)
        n_ref[...] = jnp.zeros_like(n_ref)
        m_ref[...] = jnp.zeros_like(m_ref)

    t = qf_ref.shape[1]
    rr = lax.broadcasted_iota(jnp.int32, (t, t), 0)
    cc = lax.broadcasted_iota(jnp.int32, (t, t), 1)
    chains, seen = [], []
    li_col, li_row, b_col, b_row, b_end, q32, q, k, v = [], [], [], [], [], [], [], [], []
    for di, (q_ref, k_ref, v_ref, gc_ref, gr_ref, h_ref) in enumerate(
            ((qf_ref, kf_ref, vf_ref, gcf_ref, grf_ref, hf_ref), (qb_ref, kb_ref, vb_ref, gcb_ref, grb_ref, hb_ref))):
        sgn = 1 - 2 * di
        vis = (rr - cc) * sgn >= 0
        tri = jnp.where(vis, 1.0, 0.0).astype(BF16)
        gcol = gc_ref[0, 0]
        grow = gr_ref[0, 0]
        bcol = _exact_left(tri, gcol)
        brow = sum(_mm_nt(p, tri) for p in _split3(grow))
        for h in range(heads):
            chains.append((di, h, h_ref))
            seen.append(vis)
            li_col.append(gcol[:, h:h + 1])
            li_row.append(grow[h:h + 1, :])
            b_col.append(bcol[:, heads + h:heads + h + 1])
            b_row.append(brow[heads + h:heads + h + 1, :])
            b_end.append(jnp.sum(gcol[:, heads + h:heads + h + 1], axis=0, keepdims=True))
            q32.append(q_ref[0, :, h * dk:(h + 1) * dk].astype(F32))
            q.append(q_ref[0, :, h * dk:(h + 1) * dk].astype(BF16))
            k.append(k_ref[0, :, h * dk:(h + 1) * dk].astype(F32))
            v.append(v_ref[0, :, h * dv:(h + 1) * dv].astype(BF16))
    cs = range(len(chains))
    c0 = [c_ref[chains[i][0], chains[i][1]] for i in cs]
    n0 = [n_ref[chains[i][0], chains[i][1]] for i in cs]
    m0 = [m_ref[chains[i][0], chains[i][1]][:, 0:1] for i in cs]
    g_end = [b_end[i] - b_col[i] + li_col[i] for i in cs]
    m_loc = [jnp.max(g_end[i], axis=0, keepdims=True) for i in cs]
    kw = [k[i] * jnp.exp(g_end[i] - m_loc[i]) for i in cs]
    c_loc = [_mm_tn(kw[i], v[i]) for i in cs]
    n_loc = [jnp.sum(kw[i], axis=0, keepdims=True) for i in cs]
    qk = [_mm_nt(q[i], k[i]) for i in cs]
    qc = [_mm(q[i], c0[i]) for i in cs]
    d_log = [jnp.where(seen[i], b_col[i] - b_row[i] + li_row[i], -jnp.inf) for i in cs]
    inter = [b_col[i] + m0[i] for i in cs]
    m_t = [jnp.maximum(jnp.max(d_log[i], axis=-1, keepdims=True), inter[i]) for i in cs]
    p = [jnp.exp(d_log[i] - m_t[i]) * qk[i] for i in cs]
    e_inter = [jnp.exp(inter[i] - m_t[i]) for i in cs]
    num = [_mm(p[i], v[i]) + e_inter[i] * qc[i] for i in cs]
    for i in cs:
        di, h, h_ref = chains[i]
        den = (jnp.sum(p[i], axis=-1, keepdims=True)
               + e_inter[i] * jnp.sum(q32[i] * n0[i], axis=-1, keepdims=True))
        h_ref[0, :, h * dv:(h + 1) * dv] = (num[i] / jnp.maximum(jnp.abs(den), jnp.exp(-m_t[i]))).astype(h_ref.dtype)
        m_new = jnp.maximum(b_end[i] + m0[i], m_loc[i])
        a = jnp.exp(b_end[i] + m0[i] - m_new)
        e = jnp.exp(m_loc[i] - m_new)
        c_ref[di, h] = a * c0[i] + e * c_loc[i]
        n_ref[di, h] = a * n0[i] + e * n_loc[i]
        m_ref[di, h] = jnp.broadcast_to(m_new, (1, LANES))


def _mlstm_chunks(q, k, v, gcol, grow, ctx_len):
    b, t, _ = q.shape
    heads = ML_HEADS
    dk, dv = q.shape[-1] // heads, v.shape[-1] // heads
    c = ML_CHUNK
    nc, nck = t // c, ctx_len // c

    def back(i):
        return jnp.where(i < nck, nck - 1 - i, nc - 1 - (i - nck))

    def specs(chunk_of, di):
        seq = lambda width: pl.BlockSpec((1, c, width), lambda bb, i: (bb, chunk_of(i), 0))
        return [seq(heads * dk), seq(heads * dk), seq(heads * dv),
                pl.BlockSpec((1, 1, c, LANES), lambda bb, i: (di, bb, chunk_of(i), 0)),
                pl.BlockSpec((1, 1, SUBLANES, c), lambda bb, i: (di, bb, 0, chunk_of(i)))]

    body = functools.partial(_mlstm_chunk_body, heads=heads, dk=dk, dv=dv)
    return pl.pallas_call(
        body,
        grid=(b, nc),
        in_specs=specs(lambda i: i, 0) + specs(back, 1),
        out_specs=[pl.BlockSpec((1, c, heads * dv), lambda bb, i: (bb, i, 0)),
                   pl.BlockSpec((1, c, heads * dv), lambda bb, i: (bb, back(i), 0))],
        out_shape=[jax.ShapeDtypeStruct((b, t, heads * dv), BF16)] * 2,
        scratch_shapes=[pltpu.VMEM((2, heads, dk, dv), F32), pltpu.VMEM((2, heads, 1, dk), F32),
                        pltpu.VMEM((2, heads, 1, LANES), F32)],
        compiler_params=_params(("parallel", "arbitrary")),
        name="mlstm_chunks",
    )(q, k, v, gcol, grow, q, k, v, gcol, grow)


def _mlstm_post_body(hf_ref, hb_ref, o_ref, x_ref, m_ref, ng_ref, wout_ref, out_ref, *, heads):
    hs = hf_ref[0].astype(F32) + hb_ref[0].astype(F32)
    dv = hs.shape[-1] // heads
    parts = []
    for h in range(heads):
        hh = hs[:, h * dv:(h + 1) * dv]
        parts.append(hh * lax.rsqrt(jnp.mean(hh * hh, axis=-1, keepdims=True) + NORM_EPS))
    hn = jnp.concatenate(parts, axis=-1)
    out = _mm(hn * ng_ref[...] * o_ref[0].astype(F32), wout_ref[...])
    out_ref[0] = x_ref[0] + m_ref[0, 0][2:3] * out


def _mlstm_post(hf, hb, o, xs, modsel, norm_g, w_out, nct):
    b, t, d = xs.shape
    body = functools.partial(_mlstm_post_body, heads=ML_HEADS)
    return pl.pallas_call(
        body,
        grid=(b, t // ROW_TILE),
        in_specs=[_row_spec(d)] * 4 + [_mod_spec(d, nct), _full_spec((1, d)), _full_spec((d, d))],
        out_specs=_row_spec(d),
        out_shape=jax.ShapeDtypeStruct((b, t, d), F32),
        compiler_params=_params(("parallel", "parallel"), VMEM_LIMIT_BYTES),
        name="mlstm_post",
    )(hf, hb, o, xs, modsel, norm_g.reshape(1, d), w_out.astype(BF16))


def _mlstm_layer(xs, n1g, modsel, p, nct, ctx_len):
    q, k, v, o, gf, gb = _mlstm_in(xs, n1g, modsel, p["w_in"], p["b_in"], nct)
    gcol = jnp.stack([gf, gb])
    grow = jnp.swapaxes(gcol[..., :SUBLANES], -1, -2)
    hf, hb = _mlstm_chunks(q, k, v, gcol, grow, ctx_len)
    return _mlstm_post(hf, hb, o, xs, modsel, p["norm_g"], p["w_out"], nct)


def _attn_in_body(x_ref, g_ref, m_ref, w_ref, cos_ref, sin_ref, qt_ref, k_ref, vt_ref, *, d_model):
    m = m_ref[0, 0]
    h = _normmod(x_ref[0], g_ref[...], m[1:2], m[0:1])
    z = _mm(h, w_ref[...])
    reps = d_model // LANES
    cos = jnp.concatenate([cos_ref[...]] * reps, axis=1)
    sin = jnp.concatenate([sin_ref[...]] * reps, axis=1)
    lane = lax.broadcasted_iota(jnp.int32, (1, d_model), 1)
    low = (lane % DA_HEAD) < (DA_HEAD // 2)
    half = DA_HEAD // 2

    def rope(u):
        partner = jnp.where(low, pltpu.roll(u, d_model - half, 1), pltpu.roll(u, half, 1))
        return u * cos + partner * sin

    q = rope(z[:, 0:d_model]) * (DA_HEAD ** -0.5 * math.log2(math.e))
    qt_ref[0] = q.T.astype(BF16)
    k_ref[0] = rope(z[:, d_model:2 * d_model]).astype(BF16)
    vt = z[:, 2 * d_model:].T.astype(BF16)
    dh2 = 2 * DA_HEAD
    ones = jnp.ones((ATT_ONES, vt.shape[1]), BF16)
    for hh in range(d_model // dh2):
        vt_ref[0, 0, hh] = jnp.concatenate([vt[hh * dh2:(hh + 1) * dh2], ones], axis=0)


def _attn_in(xs, g, modsel, w_qkv, cos, sin, nct):
    b, t, d = xs.shape
    nt = t // ROW_TILE
    heads, vrows = d // (2 * DA_HEAD), 2 * DA_HEAD + ATT_ONES
    body = functools.partial(_attn_in_body, d_model=d)
    tab = pl.BlockSpec((ROW_TILE, LANES), lambda bb, tt: (tt, 0))
    return pl.pallas_call(
        body,
        grid=(b, nt),
        in_specs=[_row_spec(d), _full_spec((1, d)), _mod_spec(d, nct), _full_spec((d, 3 * d)), tab, tab],
        out_specs=[pl.BlockSpec((1, d, ROW_TILE), lambda bb, tt: (bb, 0, tt)), _row_spec(d),
                   pl.BlockSpec((1, 1, heads, vrows, ROW_TILE), lambda bb, tt: (bb, tt, 0, 0, 0))],
        out_shape=[jax.ShapeDtypeStruct((b, d, t), BF16), jax.ShapeDtypeStruct((b, t, d), BF16),
                   jax.ShapeDtypeStruct((b, nt, heads, vrows, ROW_TILE), BF16)],
        compiler_params=_params(("parallel", "parallel"), VMEM_LIMIT_BYTES),
        name="attn_in",
    )(xs, g.reshape(1, d), modsel, w_qkv.astype(BF16), cos, sin)


def _attn_body(lam_ref, qt_ref, k_ref, vt_ref, o_ref, sa_ref, sb_ref, sc_ref, *, nct, lambda_init, unroll, n_blocks):
    qi = pl.program_id(2)
    dh2 = 2 * DA_HEAD
    heads = vt_ref.shape[2]
    tq = qt_ref.shape[-1]
    tk = vt_ref.shape[-1]
    zero = jnp.zeros((DA_HEAD, tq), qt_ref.dtype)
    qs = []
    for h in range(heads):
        qt = qt_ref[0, h * dh2:(h + 1) * dh2, :]
        qs.append(jnp.concatenate([jnp.concatenate([qt[0:DA_HEAD], zero], axis=1),
                                   jnp.concatenate([zero, qt[DA_HEAD:dh2]], axis=1)], axis=0))
    lam = lam_ref[...]
    lam_full = (jnp.exp(jnp.sum(lam[0:1] * lam[1:2], axis=-1, keepdims=True))
                - jnp.exp(jnp.sum(lam[2:3] * lam[3:4], axis=-1, keepdims=True)) + lambda_init)

    def scores(h, j):
        return jnp.dot(k_ref[0, j * tk:(j + 1) * tk, h * dh2:(h + 1) * dh2], qs[h],
                       preferred_element_type=F32)

    def weigh(h, j, sj, m_new):
        return jnp.dot(vt_ref[0, j, h], jnp.exp2(sj - m_new).astype(BF16), preferred_element_type=F32)

    def col_max(s):
        return jnp.max(s, axis=0, keepdims=True)

    def finish(h, acc):
        o = acc[0:dh2] / acc[dh2:dh2 + 1]
        o_ref[0, :, h * dh2:(h + 1) * dh2] = (o[:, 0:tq] - lam_full * o[:, tq:2 * tq]).T.astype(o_ref.dtype)

    @pl.when(qi < nct)
    def _():
        for h in range(heads):
            s_ctx = [scores(h, j) for j in range(nct)]
            top = functools.reduce(jnp.maximum, [col_max(sj) for sj in s_ctx])
            finish(h, sum(weigh(h, j, s_ctx[j], top) for j in range(nct)))

    @pl.when(qi >= nct)
    def _():
        bufs = (sa_ref, sb_ref, sc_ref)
        per_head = [list(range(nct + unroll))] + [list(range(nct + unroll * b, nct + unroll * (b + 1)))
                                                  for b in range(1, n_blocks)]
        blocks = [(h, chunks, i == 0, i == n_blocks - 1) for h in range(heads) for i, chunks in enumerate(per_head)]
        n = len(blocks)

        def fill(g, u):
            bufs[g % 3][u] = scores(blocks[g][0], blocks[g][1][u])

        for u in range(len(blocks[0][1])):
            fill(0, u)
        top = functools.reduce(jnp.maximum, [col_max(bufs[0][u]) for u in range(len(blocks[0][1]))])
        if n > 1:
            for u in range(len(blocks[1][1])):
                fill(1, u)
        m_run, acc = None, None
        for g in range(n):
            h, chunks, first, last = blocks[g]
            if first:
                m_new, acc = top, None
            else:
                m_new = jnp.maximum(m_run, top)
                acc = acc * jnp.exp2(m_run - m_new)
            next_top = None
            for u in range(max(len(chunks), len(blocks[g + 1][1]) if g + 1 < n else 0,
                               len(blocks[g + 2][1]) if g + 2 < n else 0)):
                if g + 2 < n and u < len(blocks[g + 2][1]):
                    fill(g + 2, u)
                if u < len(chunks):
                    part = weigh(h, chunks[u], bufs[g % 3][u], m_new)
                    acc = part if acc is None else acc + part
                if g + 1 < n and u < len(blocks[g + 1][1]):
                    top_u = col_max(bufs[(g + 1) % 3][u])
                    next_top = top_u if next_top is None else jnp.maximum(next_top, top_u)
            if last:
                finish(h, acc)
            m_run, top = m_new, next_top


def _attn(qt, k, vt, lam, lambda_init, nct):
    b, t, d = k.shape
    dh2 = 2 * DA_HEAD
    nt, vrows = vt.shape[1], vt.shape[3]
    n_lat = nt - nct
    unroll = max(u for u in range(1, ATT_UNROLL + 1) if n_lat % u == 0)
    hp = ATT_HEADS
    body = functools.partial(_attn_body, nct=nct, lambda_init=lambda_init, unroll=unroll,
                             n_blocks=n_lat // unroll)
    score_buf = pltpu.VMEM((nct + unroll, ROW_TILE, 2 * ROW_TILE), F32)
    return pl.pallas_call(
        body,
        grid=(b, d // (hp * dh2), t // ROW_TILE),
        in_specs=[pl.BlockSpec(lam.shape, lambda bb, hh, qq: (0, 0)),
                  pl.BlockSpec((1, hp * dh2, ROW_TILE), lambda bb, hh, qq: (bb, hh, qq)),
                  pl.BlockSpec((1, t, hp * dh2), lambda bb, hh, qq: (bb, 0, hh)),
                  pl.BlockSpec((1, nt, hp, vrows, ROW_TILE), lambda bb, hh, qq: (bb, 0, hh, 0, 0))],
        out_specs=pl.BlockSpec((1, ROW_TILE, hp * dh2), lambda bb, hh, qq: (bb, qq, hh)),
        out_shape=jax.ShapeDtypeStruct((b, t, d), BF16),
        scratch_shapes=[score_buf, score_buf, score_buf],
        compiler_params=_params(("parallel", "parallel", "parallel"), VMEM_LIMIT_BYTES),
        name="diff_attn",
    )(lam, qt, k, vt)


def _attn_post_body(a_ref, x_ref, m_ref, ng_ref, wout_ref, o_ref, *, heads, scale):
    a = a_ref[0].astype(F32)
    dh2 = a.shape[-1] // heads
    parts = []
    for h in range(heads):
        ah = a[:, h * dh2:(h + 1) * dh2]
        parts.append(ah * lax.rsqrt(jnp.mean(ah * ah, axis=-1, keepdims=True) + DA_EPS))
    an = jnp.concatenate(parts, axis=-1) * ng_ref[...] * scale
    o_ref[0] = x_ref[0] + m_ref[0, 0][2:3] * _mm(an, wout_ref[...])


def _attn_post(a, xs, modsel, norm_g, w_out, lambda_init, nct):
    b, t, d = xs.shape
    body = functools.partial(_attn_post_body, heads=DA_HEADS, scale=1.0 - lambda_init)
    return pl.pallas_call(
        body,
        grid=(b, t // ROW_TILE),
        in_specs=[_row_spec(d), _row_spec(d), _mod_spec(d, nct), _full_spec((1, d)), _full_spec((d, d))],
        out_specs=_row_spec(d),
        out_shape=jax.ShapeDtypeStruct((b, t, d), F32),
        compiler_params=_params(("parallel", "parallel"), VMEM_LIMIT_BYTES),
        name="attn_post",
    )(a, xs, modsel, jnp.tile(norm_g, DA_HEADS).reshape(1, d), w_out.astype(BF16))


def _rope_tables(seq, ctx_len):
    rows = seq // GRID_W
    row = jnp.repeat(jnp.arange(rows), GRID_W).astype(F32)
    col = jnp.tile(jnp.arange(GRID_W), rows).astype(F32)
    nf = DA_HEAD // 4
    inv = jnp.power(ROPE_BASE, -jnp.arange(nf, dtype=F32) / nf)
    ang = jnp.concatenate([row[:, None] * inv, col[:, None] * inv], axis=-1)
    cos, sin = jnp.cos(ang), jnp.sin(ang)
    cos = jnp.concatenate([jnp.ones((ctx_len, DA_HEAD // 2), F32), cos], axis=0)
    sin = jnp.concatenate([jnp.zeros((ctx_len, DA_HEAD // 2), F32), sin], axis=0)
    reps = LANES // DA_HEAD
    return (jnp.tile(jnp.concatenate([cos, cos], axis=-1), (1, reps)),
            jnp.tile(jnp.concatenate([-sin, sin], axis=-1), (1, reps)))


def _attn_layer(xs, n1g, modsel, p, lambda_init, nct, ctx_len):
    cos, sin = _rope_tables(xs.shape[1] - ctx_len, ctx_len)
    qt, k, vt = _attn_in(xs, n1g, modsel, p["w_qkv"], cos, sin, nct)
    a = _attn(qt, k, vt, p["lam"].astype(F32), lambda_init, nct)
    return _attn_post(a, xs, modsel, p["norm_g"], p["w_out"], lambda_init, nct)


def kernel(x, c, ctx, c_ctx, ada_w, ada_b, norm1_g, norm2_g, ffn_w_in, ffn_conv_w, ffn_conv_b, ffn_w_out,
           ra_mix, ra_w_rkv, ra_w0, ra_w1, ra_w2, ra_a0, ra_a1, ra_a2, ra_g1, ra_g2, ra_k_k, ra_k_a, ra_r_k,
           ra_lnx_g, ra_lnx_b, ra_w_out, ml_w_in, ml_b_in, ml_norm_g, ml_w_out, da_w_qkv, da_lambda,
           da_norm_g, da_w_out, final_g):
    batch, seq, d = x.shape
    ctx_len = ctx.shape[1]
    assert ctx_len % ROW_TILE == 0 and seq % ROW_TILE == 0 and batch + 1 <= SUBLANES
    nct = ctx_len // ROW_TILE
    depth = ada_w.shape[0]

    xs = jnp.concatenate([ctx, x], axis=1)
    cc = jnp.concatenate([c_ctx[None], c, jnp.zeros((SUBLANES - 1 - batch, d), F32)], axis=0)
    mod = _ada(cc, ada_w, ada_b).reshape(depth, SUBLANES, 6, d)
    modsel = jnp.stack([jnp.broadcast_to(mod[:, 0:1], (depth, batch, 6, d)), mod[:, 1:1 + batch]], axis=2)

    head_id = jnp.arange(d) // RW_HEAD
    ones_bd = (head_id[:, None] == head_id[None, :]).astype(BF16)

    for i in range(depth):
        kind, j = i % N_MIXERS, i // N_MIXERS
        if kind == 0:
            p = dict(mix=ra_mix[j], w_rkv=ra_w_rkv[j], w0=ra_w0[j], w1=ra_w1[j], w2=ra_w2[j], a0=ra_a0[j],
                     a1=ra_a1[j], a2=ra_a2[j], g1=ra_g1[j], g2=ra_g2[j], k_k=ra_k_k[j], k_a=ra_k_a[j],
                     r_k=ra_r_k[j], lnx_g=ra_lnx_g[j], lnx_b=ra_lnx_b[j], w_out=ra_w_out[j])
            xs = _rwkv_layer(xs, norm1_g[i], modsel[i], p, ones_bd, nct, ctx_len)
        elif kind == 1:
            p = dict(w_in=ml_w_in[j], b_in=ml_b_in[j], norm_g=ml_norm_g[j], w_out=ml_w_out[j])
            xs = _mlstm_layer(xs, norm1_g[i], modsel[i], p, nct, ctx_len)
        else:
            lambda_init = 0.8 - 0.6 * math.exp(-0.3 * i)
            p = dict(w_qkv=da_w_qkv[j], lam=da_lambda[j], norm_g=da_norm_g[j], w_out=da_w_out[j])
            xs = _attn_layer(xs, norm1_g[i], modsel[i], p, lambda_init, nct, ctx_len)
        xs = _ffn_call(xs, norm2_g[i], modsel[i], ffn_w_in[i], ffn_conv_w[i], ffn_conv_b[i], ffn_w_out[i], nct,
                       final_g=final_g if i == depth - 1 else None)
    return xs
```

```python
import functools
import math

import jax
import jax.numpy as jnp
from jax import lax
from jax.experimental import pallas as pl
from jax.experimental.pallas import tpu as pltpu

F32 = jnp.float32
BF16 = jnp.bfloat16

DEPTH = 4
N_MIXERS = 3
GRID_W = 64
NORM_EPS = 1e-6
RW_HEAD = 64
RW_LNX_EPS = 6.4e-4
ML_HEADS = 4
ML_CHUNK = 128
DA_HEADS = 8
DA_HEAD = 64
DA_EPS = 1e-5
ROPE_BASE = 10000.0

LANES = 128
SUBLANES = 8
VMEM_LIMIT_BYTES = 56 * 1024 * 1024

ROW_TILE = 256
RW_CHUNK = 64
RW_GROUP = 4
FFN_COLS = 256
ATT_UNROLL = 4
ATT_ONES = 16
ATT_HEADS = 4


def _params(sem, vmem=None):
    return pltpu.CompilerParams(dimension_semantics=sem, vmem_limit_bytes=vmem)


def _mm(a, b):
    return jnp.dot(a.astype(BF16), b.astype(BF16), preferred_element_type=F32)


def _mm_nt(a, b):
    return lax.dot_general(a.astype(BF16), b.astype(BF16), (((1,), (1,)), ((), ())),
                           preferred_element_type=F32)


def _mm_tn(a, b):
    return lax.dot_general(a.astype(BF16), b.astype(BF16), (((0,), (0,)), ((), ())),
                           preferred_element_type=F32)


def _split3(x):
    hi = x.astype(BF16)
    r1 = x - hi.astype(F32)
    mid = r1.astype(BF16)
    lo = (r1 - mid.astype(F32)).astype(BF16)
    return hi, mid, lo


def _exact_left(m_bf16, x, terms=3):
    return sum(jnp.dot(m_bf16, p, preferred_element_type=F32) for p in _split3(x)[:terms])


def _normmod(x, g, sc, sh):
    y = x * lax.rsqrt(jnp.mean(x * x, axis=-1, keepdims=True) + NORM_EPS)
    return y * g * (1.0 + sc) + sh


def _ada_body(c_ref, w_ref, b_ref, o_ref):
    a = c_ref[...]
    a = a * jax.nn.sigmoid(a)
    o_ref[0] = jnp.dot(a, w_ref[0], precision=lax.Precision.HIGHEST,
                       preferred_element_type=F32) + b_ref[0]


def _ada(cc, ada_w, ada_b):
    depth, d, n = ada_w.shape
    nb = n // 4
    return pl.pallas_call(
        _ada_body,
        grid=(depth, n // nb),
        in_specs=[pl.BlockSpec((SUBLANES, d), lambda l, j: (0, 0)),
                  pl.BlockSpec((1, d, nb), lambda l, j: (l, 0, j)),
                  pl.BlockSpec((1, 1, nb), lambda l, j: (l, 0, j))],
        out_specs=pl.BlockSpec((1, SUBLANES, nb), lambda l, j: (l, 0, j)),
        out_shape=jax.ShapeDtypeStruct((depth, SUBLANES, n), F32),
        compiler_params=_params(("arbitrary", "arbitrary"), VMEM_LIMIT_BYTES),
        name="ada_mod",
    )(cc, ada_w, ada_b.reshape(depth, 1, n))


def _row_spec(d, tm=ROW_TILE):
    return pl.BlockSpec((1, tm, d), lambda b, t: (b, t, 0))


def _mod_spec(d, nct):
    return pl.BlockSpec((1, 1, 6, d), lambda b, t: (b, jnp.where(t >= nct, 1, 0), 0, 0))


def _full_spec(shape):
    nd = len(shape)
    return pl.BlockSpec(shape, lambda b, t: (0,) * nd)


def _halo_specs(d, tm, n_rows):
    per = tm // SUBLANES
    last = n_rows // SUBLANES - 1
    prev = pl.BlockSpec((1, SUBLANES, d), lambda b, t: (b, jnp.maximum(t * per - 1, 0), 0))
    nxt = pl.BlockSpec((1, SUBLANES, d), lambda b, t: (b, jnp.minimum((t + 1) * per, last), 0))
    return prev, nxt


def _ffn_body(x_ref, xp_ref, xn_ref, g_ref, m_ref, win_ref, cw_ref, cb_ref, wout_ref, fg_ref, o_ref, act_ref,
              *, nct, nt, t0, hidden, final):
    t = pl.program_id(1) + t0
    x = x_ref[0]
    tm = x.shape[0]
    m = m_ref[0, 0]
    xa = jnp.concatenate([xp_ref[0], x, xn_ref[0]], axis=0)
    h32 = _normmod(xa, g_ref[...], m[4:5], m[3:4])
    ha = h32.astype(BF16)
    hm = h32[SUBLANES:SUBLANES + tm].astype(BF16)
    first = jnp.logical_or(t == 0, t == nct)
    last = jnp.logical_or(t == nct - 1, t == nt - 1)
    rows = lax.broadcasted_iota(jnp.int32, (tm + 2 * SUBLANES, 1), 0)
    dead = jnp.logical_or(rows == jnp.where(first, SUBLANES - 1, -1),
                          rows == jnp.where(last, tm + SUBLANES, -1))
    keep = jnp.where(dead, 0.0, 1.0)
    for j in range(hidden // FFN_COLS):
        lo, hi = j * FFN_COLS, (j + 1) * FFN_COLS
        val = jnp.dot(hm, win_ref[:, lo:hi], preferred_element_type=F32)
        gate = jnp.dot(ha, win_ref[:, hidden + lo:hidden + hi], preferred_element_type=F32) * keep
        g_prev = pltpu.roll(gate, 1, 0)[SUBLANES:SUBLANES + tm]
        g_next = pltpu.roll(gate, tm + 2 * SUBLANES - 1, 0)[SUBLANES:SUBLANES + tm]
        g_mid = gate[SUBLANES:SUBLANES + tm]
        cw = cw_ref[:, lo:hi]
        conv = cw[0:1] * g_prev + cw[1:2] * g_mid + cw[2:3] * g_next + cb_ref[:, lo:hi]
        act_ref[:, lo:hi] = (jax.nn.gelu(conv) * val).astype(BF16)
    out = jnp.dot(act_ref[...], wout_ref[...], preferred_element_type=F32)
    y = x + m[5:6] * out
    if final:
        y = y * lax.rsqrt(jnp.mean(y * y, axis=-1, keepdims=True) + NORM_EPS) * fg_ref[...]
    o_ref[0] = y


def _ffn_call(xs, g, modsel, w_in, conv_w, conv_b, w_out, nct, final_g=None):
    b, t, d = xs.shape
    hidden = w_out.shape[0]
    nt = t // ROW_TILE
    final = final_g is not None
    t0 = nct if final else 0
    per = ROW_TILE // SUBLANES
    last = t // SUBLANES - 1
    body = functools.partial(_ffn_body, nct=nct, nt=nt, t0=t0, hidden=hidden, final=final)
    return pl.pallas_call(
        body,
        grid=(b, nt - t0),
        in_specs=[pl.BlockSpec((1, ROW_TILE, d), lambda bb, tt: (bb, tt + t0, 0)),
                  pl.BlockSpec((1, SUBLANES, d), lambda bb, tt: (bb, jnp.maximum((tt + t0) * per - 1, 0), 0)),
                  pl.BlockSpec((1, SUBLANES, d), lambda bb, tt: (bb, jnp.minimum((tt + t0 + 1) * per, last), 0)),
                  _full_spec((1, d)),
                  pl.BlockSpec((1, 1, 6, d), lambda bb, tt: (bb, jnp.where(tt + t0 >= nct, 1, 0), 0, 0)),
                  _full_spec((d, 2 * hidden)), _full_spec((3, hidden)), _full_spec((1, hidden)),
                  _full_spec((hidden, d)), _full_spec((1, d))],
        out_specs=_row_spec(d),
        out_shape=jax.ShapeDtypeStruct((b, t - t0 * ROW_TILE, d), F32),
        scratch_shapes=[pltpu.VMEM((ROW_TILE, hidden), BF16)],
        compiler_params=_params(("parallel", "parallel"), VMEM_LIMIT_BYTES),
        name="conv_ffn",
    )(xs, xs, xs, g.reshape(1, d), modsel, w_in.astype(BF16), conv_w, conv_b.reshape(1, hidden),
      w_out.astype(BF16), (final_g if final else g).reshape(1, d))


def _seg_sum(x, ones_bd):
    return jnp.dot(x.astype(BF16), ones_bd, preferred_element_type=F32)


def _rwkv_prep_body(x_ref, xp_ref, xn_ref, ng_ref, m_ref, mix_ref, wrkv_ref, w1_ref, a1_ref, g1_ref, w2_ref, a2_ref,
                    g2_ref, vec_ref, ones_ref,
                    r_ref, v_ref, kk_ref, g_ref, bonus_ref, lw_ref, kd_ref, ka_ref, *, nct, nt):
    t = pl.program_id(1)
    m = m_ref[0, 0]
    h = _normmod(x_ref[0], ng_ref[...], m[1:2], m[0:1])
    tm = h.shape[0]
    first = jnp.logical_or(t == 0, t == nct)
    last = jnp.logical_or(t == nct - 1, t == nt - 1)
    p_row = _normmod(xp_ref[0], ng_ref[...], m[1:2], m[0:1])[SUBLANES - 1:SUBLANES] * jnp.where(first, 0.0, 1.0)
    n_row = _normmod(xn_ref[0], ng_ref[...], m[1:2], m[0:1])[0:1] * jnp.where(last, 0.0, 1.0)
    rows = lax.broadcasted_iota(jnp.int32, (tm, 1), 0)
    h_prev = jnp.where(rows == 0, p_row, pltpu.roll(h, 1, 0))
    h_next = jnp.where(rows == tm - 1, n_row, pltpu.roll(h, tm - 1, 0))
    xx = 0.5 * (h_prev + h_next) - h
    mix = mix_ref[...]
    vec = vec_ref[...]
    ones_bd = ones_ref[...]

    def shifted(j):
        return h + xx * mix[j:j + 1]

    r = _mm(shifted(0), wrkv_ref[0])
    k = _mm(shifted(1), wrkv_ref[1])
    v = _mm(shifted(2), wrkv_ref[2])
    w_in = jnp.tanh(_mm(shifted(3), w1_ref[...]))
    a_in = _mm(shifted(4), a1_ref[...])
    g = _mm(jax.nn.sigmoid(_mm(shifted(5), g1_ref[...])), g2_ref[...])

    kk = k * vec[4:5]
    kk = kk / jnp.maximum(jnp.sqrt(_seg_sum(kk * kk, ones_bd)), 1e-12)
    kd_sum = None
    for d in range(2):
        w_pre = vec[d:d + 1] + _mm(w_in, w2_ref[d])
        lw_ref[d, 0] = -jax.nn.sigmoid(w_pre) * math.exp(-0.5)
        a = jax.nn.sigmoid(vec[2 + d:3 + d] + _mm(a_in, a2_ref[d]))
        kd = k * (1.0 + (a - 1.0) * vec[5:6])
        kd_ref[d, 0] = kd.astype(kd_ref.dtype)
        ka_ref[d, 0] = (kk * a).astype(ka_ref.dtype)
        kd_sum = kd if kd_sum is None else kd_sum + kd
    r_ref[0] = r.astype(r_ref.dtype)
    v_ref[0] = v.astype(v_ref.dtype)
    kk_ref[0] = kk.astype(kk_ref.dtype)
    g_ref[0] = g.astype(g_ref.dtype)
    bonus_ref[0] = (_seg_sum(r * kd_sum * vec[6:7], ones_bd) * v).astype(bonus_ref.dtype)


def _rwkv_prep(xs, ng, modsel, mix, w_rkv, w0, w1, w2, a0, a1, a2, g1, g2, k_k, k_a, r_k, ones_bd, nct):
    b, t, d = xs.shape
    nt = t // ROW_TILE
    lora = w1.shape[-1]
    prev, nxt = _halo_specs(d, ROW_TILE, t)
    zeros = jnp.zeros((lora, d), F32)
    w2p = jnp.stack([jnp.concatenate([w2[0], zeros], 0), jnp.concatenate([zeros, w2[1]], 0)]).astype(BF16)
    a2p = jnp.stack([jnp.concatenate([a2[0], zeros], 0), jnp.concatenate([zeros, a2[1]], 0)]).astype(BF16)
    w1c = jnp.concatenate([w1[0], w1[1]], axis=1).astype(BF16)
    a1c = jnp.concatenate([a1[0], a1[1]], axis=1).astype(BF16)
    vec = jnp.stack([w0[0], w0[1], a0[0], a0[1], k_k, k_a, r_k.reshape(d), jnp.zeros((d,), F32)])
    shared = jax.ShapeDtypeStruct((b, t, d), BF16)
    directional = jax.ShapeDtypeStruct((2, b, t, d), BF16)
    log_decay = jax.ShapeDtypeStruct((2, b, t, d), F32)
    dir_spec = pl.BlockSpec((2, 1, ROW_TILE, d), lambda bb, tt: (0, bb, tt, 0))
    body = functools.partial(_rwkv_prep_body, nct=nct, nt=nt)
    return pl.pallas_call(
        body,
        grid=(b, nt),
        in_specs=[_row_spec(d), prev, nxt, _full_spec((1, d)), _mod_spec(d, nct), _full_spec((6, d)),
                  _full_spec((3, d, d)), _full_spec(w1c.shape), _full_spec(a1c.shape), _full_spec(g1.shape),
                  _full_spec(w2p.shape), _full_spec(a2p.shape), _full_spec(g2.shape),
                  _full_spec((SUBLANES, d)), _full_spec((d, d))],
        out_specs=[_row_spec(d)] * 5 + [dir_spec] * 3,
        out_shape=[shared] * 5 + [log_decay, directional, directional],
        compiler_params=_params(("parallel", "parallel"), VMEM_LIMIT_BYTES),
        name="rwkv_prep",
    )(xs, xs, xs, ng.reshape(1, d), modsel, mix, w_rkv.astype(BF16), w1c, a1c, g1.astype(BF16), w2p, a2p, g2.astype(BF16), vec,
      ones_bd)


def _rwkv_scan_body(rf_ref, vf_ref, kkf_ref, lwf_ref, kdf_ref, kaf_ref, rb_ref, vb_ref, kkb_ref, lwb_ref, kdb_ref,
                    kab_ref, yf_ref, yb_ref, s_ref, *, chunk, group):
    @pl.when(pl.program_id(0) == 0)
    def _():
        s_ref[...] = jnp.zeros_like(s_ref)

    c, gc, width = chunk, group * chunk, group * RW_HEAD
    pr = lax.broadcasted_iota(jnp.int32, (2 * c, 2 * gc), 0)
    pc = lax.broadcasted_iota(jnp.int32, (2 * c, 2 * gc), 1)
    wr = lax.broadcasted_iota(jnp.int32, (c, gc), 0)
    wc = lax.broadcasted_iota(jnp.int32, (c, gc), 1)
    eye_w = jnp.where(wr == wc % c, 1.0, 0.0)
    tr = lax.broadcasted_iota(jnp.int32, (c, c), 0)
    tc = lax.broadcasted_iota(jnp.int32, (c, c), 1)
    own = (lax.broadcasted_iota(jnp.int32, (gc, width), 0) // c
           == lax.broadcasted_iota(jnp.int32, (gc, width), 1) // RW_HEAD)
    same_head = (lax.broadcasted_iota(jnp.int32, (width, width), 0) // RW_HEAD
                 == lax.broadcasted_iota(jnp.int32, (width, width), 1) // RW_HEAD)

    def stack(x):
        return jnp.where(own, jnp.concatenate([x] * group, axis=0), 0.0).astype(BF16)

    def direction(di, r_ref, v_ref, kk_ref, lw_ref, kd_ref, ka_ref, y_ref):
        sgn = 1 - 2 * di
        rows = range(r_ref.shape[0])
        chains = [(bi, hg) for bi in rows for hg in range(r_ref.shape[-1] // width)]
        lns = [slice(hg * width, (hg + 1) * width) for _, hg in chains]
        gs = range(len(chains))
        st = {}

        def prepare():
            tri = jnp.where((tr - tc) * sgn >= 0, 1.0, 0.0).astype(BF16)
            lw_all = [lw_ref[0, bi] for bi in rows]
            cum_all = [_exact_left(tri, lw_all[bi], terms=2) for bi in rows]
            tot_all = [jnp.sum(lw_all[bi], axis=0, keepdims=True) for bi in rows]
            st["left"], st["right"], st["carry_in"], st["v_nat"], st["vm"], st["decay"] = [], [], [], [], [], []
            for (bi, _), ln in zip(chains, lns):
                lw, cum, tot = lw_all[bi][:, ln], cum_all[bi][:, ln], tot_all[bi][:, ln]
                e_neg = jnp.exp(-cum)
                e_out = jnp.exp(tot - cum)
                kk, ka, kd = (ref.astype(F32) for ref in (kk_ref[bi, :, ln], ka_ref[0, bi, :, ln], kd_ref[0, bi, :, ln]))
                st["left"].append(jnp.concatenate([kk * jnp.exp(cum - lw),
                                                   r_ref[bi, :, ln].astype(F32) * jnp.exp(cum)], axis=0).astype(BF16))
                st["right"].append(jnp.concatenate([stack(ka * e_neg), stack(kd * e_neg)], axis=0))
                st["carry_in"].append(jnp.concatenate([ka * e_out, kd * e_out], axis=0).astype(BF16))
                st["v_nat"].append(v_ref[bi, :, ln].astype(BF16))
                st["vm"].append(stack(v_ref[bi, :, ln].astype(F32)))
                st["decay"].append(jnp.exp(tot))

        def pair_products():
            visible = ((pr % c) - (pc % c)) * sgn > jnp.where(pr < c, 0, -1)
            st["pair"] = [jnp.where(visible, _mm_nt(st["left"][g], st["right"][g]), 0.0) for g in gs]
            st["inv"] = [eye_w - st["pair"][g][0:c, 0:gc] for g in gs]
            st["xp"] = [_mm(st["pair"][g][0:c, 0:gc], stack(st["pair"][g][0:c, 0:gc])) for g in gs]

        def double():
            both = [_mm(jnp.concatenate([st["inv"][g], st["xp"][g]], axis=0), stack(st["xp"][g])) for g in gs]
            st["inv"] = [st["inv"][g] + both[g][0:c] for g in gs]
            st["xp"] = [both[g][c:2 * c] for g in gs]

        def last_double():
            st["inv"] = [st["inv"][g] + _mm(st["inv"][g], stack(st["xp"][g])) for g in gs]

        def residual():
            st["resid"] = [(eye_w - st["inv"][g]) - _mm(st["pair"][g][0:c, 0:gc], stack(st["inv"][g])) for g in gs]

        def newton():
            st["inv"] = [st["inv"][g] + _mm(st["inv"][g], stack(st["resid"][g])) for g in gs]

        def read_state():
            st["s"] = [s_ref[di, chains[g][0], chains[g][1]] for g in gs]
            st["from_state"] = [_mm_nt(st["left"][g], st["s"][g]) for g in gs]
            st["from_v"] = [_mm(st["pair"][g][:, gc:2 * gc], st["vm"][g]) for g in gs]

        def solve():
            st["u"] = [-_mm(st["inv"][g], stack(st["from_state"][g][0:c] + st["from_v"][g][0:c])) for g in gs]

        def emit():
            for g in gs:
                bi, hg = chains[g]
                y_ref[bi, :, lns[g]] = (st["from_state"][g][c:2 * c] + st["from_v"][g][c:2 * c]
                                        + _mm(st["pair"][g][c:2 * c, 0:gc], stack(st["u"][g]))).astype(y_ref.dtype)
                grown = _mm_tn(jnp.concatenate([st["u"][g].astype(BF16), st["v_nat"][g]], axis=0),
                               st["carry_in"][g])
                s_ref[di, bi, hg] = st["s"][g] * st["decay"][g] + jnp.where(same_head, grown, 0.0)

        return ([prepare, pair_products] + [double] * (int(math.log2(c)) - 3)
                + [last_double, residual, newton, read_state, solve, emit])

    forward = direction(0, rf_ref, vf_ref, kkf_ref, lwf_ref, kdf_ref, kaf_ref, yf_ref)
    backward = direction(1, rb_ref, vb_ref, kkb_ref, lwb_ref, kdb_ref, kab_ref, yb_ref)
    for stage_f, stage_b in zip(forward, backward):
        stage_f()
        stage_b()


def _rwkv_scan(r, v, kk, lw, kd, ka, ctx_len):
    b, t, d = r.shape
    c = RW_CHUNK
    width = RW_GROUP * RW_HEAD
    nc, nck = t // c, ctx_len // c

    def back(i):
        return jnp.where(i < nck, nck - 1 - i, nc - 1 - (i - nck))

    fwd = pl.BlockSpec((b, c, d), lambda i: (0, i, 0))
    bwd = pl.BlockSpec((b, c, d), lambda i: (0, back(i), 0))
    fwd_dir = pl.BlockSpec((1, b, c, d), lambda i: (0, 0, i, 0))
    bwd_dir = pl.BlockSpec((1, b, c, d), lambda i: (1, 0, back(i), 0))
    body = functools.partial(_rwkv_scan_body, chunk=c, group=RW_GROUP)
    return pl.pallas_call(
        body,
        grid=(nc,),
        in_specs=[fwd, fwd, fwd, fwd_dir, fwd_dir, fwd_dir, bwd, bwd, bwd, bwd_dir, bwd_dir, bwd_dir],
        out_specs=[fwd, bwd],
        out_shape=[jax.ShapeDtypeStruct((b, t, d), BF16)] * 2,
        scratch_shapes=[pltpu.VMEM((2, b, d // width, width, width), F32)],
        compiler_params=_params(("arbitrary",), VMEM_LIMIT_BYTES),
        name="rwkv_scan",
    )(r, v, kk, lw, kd, ka, r, v, kk, lw, kd, ka)


def _rwkv_post_body(yf_ref, yb_ref, bonus_ref, g_ref, x_ref, m_ref, lng_ref, lnb_ref, ones_ref, wout_ref, o_ref):
    ones_bd = ones_ref[...]
    y = yf_ref[0].astype(F32) + yb_ref[0].astype(F32)
    mu = _seg_sum(y, ones_bd) * (1.0 / RW_HEAD)
    yc = y - mu
    var = _seg_sum(yc * yc, ones_bd) * (1.0 / RW_HEAD)
    yn = yc * lax.rsqrt(var + RW_LNX_EPS) * lng_ref[...] + lnb_ref[...]
    out = _mm((yn + bonus_ref[0].astype(F32)) * g_ref[0].astype(F32), wout_ref[...])
    o_ref[0] = x_ref[0] + m_ref[0, 0][2:3] * out


def _rwkv_post(yf, yb, bonus, g, xs, modsel, lnx_g, lnx_b, ones_bd, w_out, nct):
    b, t, d = xs.shape
    return pl.pallas_call(
        _rwkv_post_body,
        grid=(b, t // ROW_TILE),
        in_specs=[_row_spec(d)] * 5 + [_mod_spec(d, nct), _full_spec((1, d)), _full_spec((1, d)),
                                      _full_spec((d, d)), _full_spec((d, d))],
        out_specs=_row_spec(d),
        out_shape=jax.ShapeDtypeStruct((b, t, d), F32),
        compiler_params=_params(("parallel", "parallel"), VMEM_LIMIT_BYTES),
        name="rwkv_post",
    )(yf, yb, bonus, g, xs, modsel, lnx_g.reshape(1, d), lnx_b.reshape(1, d), ones_bd, w_out.astype(BF16))


def _rwkv_layer(xs, n1g, modsel, p, ones_bd, nct, ctx_len):
    r, v, kk, g, bonus, lw, kd, ka = _rwkv_prep(
        xs, n1g, modsel, p["mix"], p["w_rkv"], p["w0"], p["w1"], p["w2"], p["a0"], p["a1"], p["a2"], p["g1"], p["g2"],
        p["k_k"], p["k_a"], p["r_k"], ones_bd, nct)
    yf, yb = _rwkv_scan(r, v, kk, lw, kd, ka, ctx_len)
    return _rwkv_post(yf, yb, bonus, g, xs, modsel, p["lnx_g"], p["lnx_b"], ones_bd, p["w_out"], nct)


def _mlstm_in_body(x_ref, g_ref, m_ref, w_ref, b_ref, q_ref, k_ref, v_ref, o_ref, gf_ref, gb_ref,
                   *, heads, dk, dv):
    m = m_ref[0, 0]
    h = _normmod(x_ref[0], g_ref[...], m[1:2], m[0:1])
    z = _mm(h, w_ref[...]) + b_ref[...]
    nqk = heads * dk
    nv = heads * dv
    q_ref[0] = z[:, 0:nqk].astype(q_ref.dtype)
    k_ref[0] = (z[:, nqk:2 * nqk] * (dk ** -0.5)).astype(k_ref.dtype)
    v_ref[0] = z[:, 2 * nqk:2 * nqk + nv].astype(v_ref.dtype)
    d_model = o_ref.shape[-1]
    o_ref[0] = jax.nn.sigmoid(z[:, 2 * nqk + nv:2 * nqk + nv + d_model]).astype(o_ref.dtype)
    gates = z[:, 2 * nqk + nv + d_model:]
    lane = lax.broadcasted_iota(jnp.int32, (1, LANES), 1)
    is_forget = jnp.logical_and(lane >= heads, lane < 2 * heads)
    for ref, off in ((gf_ref, 0), (gb_ref, LANES)):
        gd = gates[:, off:off + LANES]
        ref[0] = jnp.where(is_forget, jax.nn.log_sigmoid(gd), gd)


def _mlstm_in(xs, g, modsel, w_in, b_in, nct):
    b, t, d = xs.shape
    heads, dk, dv = ML_HEADS, d // (2 * ML_HEADS), d // ML_HEADS
    main = 2 * heads * dk + heads * dv + d
    pad = jnp.zeros((d, LANES - 2 * heads), F32)
    w_cat = jnp.concatenate([w_in[:, :main], w_in[:, main:main + 2 * heads], pad,
                             w_in[:, main + 2 * heads:], pad], axis=1)
    padb = jnp.zeros((LANES - 2 * heads,), F32)
    b_cat = jnp.concatenate([b_in[:main], b_in[main:main + 2 * heads], padb, b_in[main + 2 * heads:], padb])
    n = w_cat.shape[1]
    body = functools.partial(_mlstm_in_body, heads=heads, dk=dk, dv=dv)
    shapes = [(heads * dk, BF16), (heads * dk, BF16), (heads * dv, BF16), (d, BF16), (LANES, F32), (LANES, F32)]
    return pl.pallas_call(
        body,
        grid=(b, t // ROW_TILE),
        in_specs=[_row_spec(d), _full_spec((1, d)), _mod_spec(d, nct), _full_spec((d, n)), _full_spec((1, n))],
        out_specs=[_row_spec(s) for s, _ in shapes],
        out_shape=[jax.ShapeDtypeStruct((b, t, s), dt) for s, dt in shapes],
        compiler_params=_params(("parallel", "parallel"), VMEM_LIMIT_BYTES),
        name="mlstm_in",
    )(xs, g.reshape(1, d), modsel, w_cat.astype(BF16), b_cat.reshape(1, n))


def _mlstm_chunk_body(qf_ref, kf_ref, vf_ref, gcf_ref, grf_ref, qb_ref, kb_ref, vb_ref, gcb_ref, grb_ref,
                      hf_ref, hb_ref, c_ref, n_ref, m_ref, *, heads, dk, dv):
    @pl.when(pl.program_id(0) == 0)
    def _():
        c_ref[...] = jnp.zeros_like(c_ref)
        n_ref[...] = jnp.zeros_like(n_ref)
        m_ref[...] = jnp.zeros_like(m_ref)

    t = qf_ref.shape[1]
    rr = lax.broadcasted_iota(jnp.int32, (t, t), 0)
    cc = lax.broadcasted_iota(jnp.int32, (t, t), 1)
    chains, seen = [], []
    li_col, li_row, b_col, b_row, b_end, q32, q, k, v = [], [], [], [], [], [], [], [], []
    for di, (q_ref, k_ref, v_ref, gc_ref, gr_ref, h_ref) in enumerate(
            ((qf_ref, kf_ref, vf_ref, gcf_ref, grf_ref, hf_ref), (qb_ref, kb_ref, vb_ref, gcb_ref, grb_ref, hb_ref))):
        sgn = 1 - 2 * di
        vis = (rr - cc) * sgn >= 0
        tri = jnp.where(vis, 1.0, 0.0).astype(BF16)
        for bi in range(q_ref.shape[0]):
            gcol = gc_ref[0, bi]
            grow = gr_ref[0, bi]
            bcol = _exact_left(tri, gcol)
            brow = sum(_mm_nt(p, tri) for p in _split3(grow))
            for h in range(heads):
                chains.append((di, bi * heads + h, h_ref, bi, h))
                seen.append(vis)
                li_col.append(gcol[:, h:h + 1])
                li_row.append(grow[h:h + 1, :])
                b_col.append(bcol[:, heads + h:heads + h + 1])
                b_row.append(brow[heads + h:heads + h + 1, :])
                b_end.append(jnp.sum(gcol[:, heads + h:heads + h + 1], axis=0, keepdims=True))
                q32.append(q_ref[bi, :, h * dk:(h + 1) * dk].astype(F32))
                q.append(q_ref[bi, :, h * dk:(h + 1) * dk].astype(BF16))
                k.append(k_ref[bi, :, h * dk:(h + 1) * dk].astype(F32))
                v.append(v_ref[bi, :, h * dv:(h + 1) * dv].astype(BF16))
    cs = range(len(chains))
    c0 = [c_ref[chains[i][0], chains[i][1]] for i in cs]
    n0 = [n_ref[chains[i][0], chains[i][1]] for i in cs]
    m0 = [m_ref[chains[i][0], chains[i][1]][:, 0:1] for i in cs]
    g_end = [b_end[i] - b_col[i] + li_col[i] for i in cs]
    m_loc = [jnp.max(g_end[i], axis=0, keepdims=True) for i in cs]
    kw = [k[i] * jnp.exp(g_end[i] - m_loc[i]) for i in cs]
    c_loc = [_mm_tn(kw[i], v[i]) for i in cs]
    n_loc = [jnp.sum(kw[i], axis=0, keepdims=True) for i in cs]
    qk = [_mm_nt(q[i], k[i]) for i in cs]
    qc = [_mm(q[i], c0[i]) for i in cs]
    d_log = [jnp.where(seen[i], b_col[i] - b_row[i] + li_row[i], -jnp.inf) for i in cs]
    inter = [b_col[i] + m0[i] for i in cs]
    m_t = [jnp.maximum(jnp.max(d_log[i], axis=-1, keepdims=True), inter[i]) for i in cs]
    p = [jnp.exp(d_log[i] - m_t[i]) * qk[i] for i in cs]
    e_inter = [jnp.exp(inter[i] - m_t[i]) for i in cs]
    num = [_mm(p[i], v[i]) + e_inter[i] * qc[i] for i in cs]
    for i in cs:
        di, slot, h_ref, bi, h = chains[i]
        den = (jnp.sum(p[i], axis=-1, keepdims=True)
               + e_inter[i] * jnp.sum(q32[i] * n0[i], axis=-1, keepdims=True))
        h_ref[bi, :, h * dv:(h + 1) * dv] = (num[i] / jnp.maximum(jnp.abs(den), jnp.exp(-m_t[i]))).astype(h_ref.dtype)
        m_new = jnp.maximum(b_end[i] + m0[i], m_loc[i])
        a = jnp.exp(b_end[i] + m0[i] - m_new)
        e = jnp.exp(m_loc[i] - m_new)
        c_ref[di, slot] = a * c0[i] + e * c_loc[i]
        n_ref[di, slot] = a * n0[i] + e * n_loc[i]
        m_ref[di, slot] = jnp.broadcast_to(m_new, (1, LANES))


def _mlstm_chunks(q, k, v, gcol, grow, ctx_len):
    b, t, _ = q.shape
    heads = ML_HEADS
    dk, dv = q.shape[-1] // heads, v.shape[-1] // heads
    c = ML_CHUNK
    nc, nck = t // c, ctx_len // c

    def back(i):
        return jnp.where(i < nck, nck - 1 - i, nc - 1 - (i - nck))

    def specs(chunk_of, di):
        seq = lambda width: pl.BlockSpec((b, c, width), lambda i: (0, chunk_of(i), 0))
        return [seq(heads * dk), seq(heads * dk), seq(heads * dv),
                pl.BlockSpec((1, b, c, LANES), lambda i: (di, 0, chunk_of(i), 0)),
                pl.BlockSpec((1, b, SUBLANES, c), lambda i: (di, 0, 0, chunk_of(i)))]

    body = functools.partial(_mlstm_chunk_body, heads=heads, dk=dk, dv=dv)
    return pl.pallas_call(
        body,
        grid=(nc,),
        in_specs=specs(lambda i: i, 0) + specs(back, 1),
        out_specs=[pl.BlockSpec((b, c, heads * dv), lambda i: (0, i, 0)),
                   pl.BlockSpec((b, c, heads * dv), lambda i: (0, back(i), 0))],
        out_shape=[jax.ShapeDtypeStruct((b, t, heads * dv), BF16)] * 2,
        scratch_shapes=[pltpu.VMEM((2, b * heads, dk, dv), F32), pltpu.VMEM((2, b * heads, 1, dk), F32),
                        pltpu.VMEM((2, b * heads, 1, LANES), F32)],
        compiler_params=_params(("arbitrary",), VMEM_LIMIT_BYTES),
        name="mlstm_chunks",
    )(q, k, v, gcol, grow, q, k, v, gcol, grow)


def _mlstm_post_body(hf_ref, hb_ref, o_ref, x_ref, m_ref, ng_ref, wout_ref, out_ref, *, heads):
    hs = hf_ref[0].astype(F32) + hb_ref[0].astype(F32)
    dv = hs.shape[-1] // heads
    parts = []
    for h in range(heads):
        hh = hs[:, h * dv:(h + 1) * dv]
        parts.append(hh * lax.rsqrt(jnp.mean(hh * hh, axis=-1, keepdims=True) + NORM_EPS))
    hn = jnp.concatenate(parts, axis=-1)
    out = _mm(hn * ng_ref[...] * o_ref[0].astype(F32), wout_ref[...])
    out_ref[0] = x_ref[0] + m_ref[0, 0][2:3] * out


def _mlstm_post(hf, hb, o, xs, modsel, norm_g, w_out, nct):
    b, t, d = xs.shape
    body = functools.partial(_mlstm_post_body, heads=ML_HEADS)
    return pl.pallas_call(
        body,
        grid=(b, t // ROW_TILE),
        in_specs=[_row_spec(d)] * 4 + [_mod_spec(d, nct), _full_spec((1, d)), _full_spec((d, d))],
        out_specs=_row_spec(d),
        out_shape=jax.ShapeDtypeStruct((b, t, d), F32),
        compiler_params=_params(("parallel", "parallel"), VMEM_LIMIT_BYTES),
        name="mlstm_post",
    )(hf, hb, o, xs, modsel, norm_g.reshape(1, d), w_out.astype(BF16))


def _mlstm_layer(xs, n1g, modsel, p, nct, ctx_len):
    q, k, v, o, gf, gb = _mlstm_in(xs, n1g, modsel, p["w_in"], p["b_in"], nct)
    gcol = jnp.stack([gf, gb])
    grow = jnp.swapaxes(gcol[..., :SUBLANES], -1, -2)
    hf, hb = _mlstm_chunks(q, k, v, gcol, grow, ctx_len)
    return _mlstm_post(hf, hb, o, xs, modsel, p["norm_g"], p["w_out"], nct)


def _attn_in_body(x_ref, g_ref, m_ref, w_ref, cos_ref, sin_ref, qt_ref, k_ref, vt_ref, *, d_model):
    m = m_ref[0, 0]
    h = _normmod(x_ref[0], g_ref[...], m[1:2], m[0:1])
    z = _mm(h, w_ref[...])
    reps = d_model // LANES
    cos = jnp.concatenate([cos_ref[...]] * reps, axis=1)
    sin = jnp.concatenate([sin_ref[...]] * reps, axis=1)
    lane = lax.broadcasted_iota(jnp.int32, (1, d_model), 1)
    low = (lane % DA_HEAD) < (DA_HEAD // 2)
    half = DA_HEAD // 2

    def rope(u):
        partner = jnp.where(low, pltpu.roll(u, d_model - half, 1), pltpu.roll(u, half, 1))
        return u * cos + partner * sin

    q = rope(z[:, 0:d_model]) * (DA_HEAD ** -0.5 * math.log2(math.e))
    qt_ref[0] = q.T.astype(BF16)
    k_ref[0] = rope(z[:, d_model:2 * d_model]).astype(BF16)
    vt = z[:, 2 * d_model:].T.astype(BF16)
    dh2 = 2 * DA_HEAD
    ones = jnp.ones((ATT_ONES, vt.shape[1]), BF16)
    for hh in range(d_model // dh2):
        vt_ref[0, 0, hh] = jnp.concatenate([vt[hh * dh2:(hh + 1) * dh2], ones], axis=0)


def _attn_in(xs, g, modsel, w_qkv, cos, sin, nct):
    b, t, d = xs.shape
    nt = t // ROW_TILE
    heads, vrows = d // (2 * DA_HEAD), 2 * DA_HEAD + ATT_ONES
    body = functools.partial(_attn_in_body, d_model=d)
    tab = pl.BlockSpec((ROW_TILE, LANES), lambda bb, tt: (tt, 0))
    return pl.pallas_call(
        body,
        grid=(b, nt),
        in_specs=[_row_spec(d), _full_spec((1, d)), _mod_spec(d, nct), _full_spec((d, 3 * d)), tab, tab],
        out_specs=[pl.BlockSpec((1, d, ROW_TILE), lambda bb, tt: (bb, 0, tt)), _row_spec(d),
                   pl.BlockSpec((1, 1, heads, vrows, ROW_TILE), lambda bb, tt: (bb, tt, 0, 0, 0))],
        out_shape=[jax.ShapeDtypeStruct((b, d, t), BF16), jax.ShapeDtypeStruct((b, t, d), BF16),
                   jax.ShapeDtypeStruct((b, nt, heads, vrows, ROW_TILE), BF16)],
        compiler_params=_params(("parallel", "parallel"), VMEM_LIMIT_BYTES),
        name="attn_in",
    )(xs, g.reshape(1, d), modsel, w_qkv.astype(BF16), cos, sin)


def _attn_body(lam_ref, qt_ref, k_ref, vt_ref, o_ref, sa_ref, sb_ref, sc_ref, *, nct, lambda_init, unroll, n_blocks):
    qi = pl.program_id(2)
    dh2 = 2 * DA_HEAD
    heads = vt_ref.shape[2]
    tq = qt_ref.shape[-1]
    tk = vt_ref.shape[-1]
    zero = jnp.zeros((DA_HEAD, tq), qt_ref.dtype)
    qs = []
    for h in range(heads):
        qt = qt_ref[0, h * dh2:(h + 1) * dh2, :]
        qs.append(jnp.concatenate([jnp.concatenate([qt[0:DA_HEAD], zero], axis=1),
                                   jnp.concatenate([zero, qt[DA_HEAD:dh2]], axis=1)], axis=0))
    lam = lam_ref[...]
    lam_full = (jnp.exp(jnp.sum(lam[0:1] * lam[1:2], axis=-1, keepdims=True))
                - jnp.exp(jnp.sum(lam[2:3] * lam[3:4], axis=-1, keepdims=True)) + lambda_init)

    def scores(h, j):
        return jnp.dot(k_ref[0, j * tk:(j + 1) * tk, h * dh2:(h + 1) * dh2], qs[h],
                       preferred_element_type=F32)

    def weigh(h, j, sj, m_new):
        return jnp.dot(vt_ref[0, j, h], jnp.exp2(sj - m_new).astype(BF16), preferred_element_type=F32)

    def col_max(s):
        return jnp.max(s, axis=0, keepdims=True)

    def finish(h, acc):
        o = acc[0:dh2] / acc[dh2:dh2 + 1]
        o_ref[0, :, h * dh2:(h + 1) * dh2] = (o[:, 0:tq] - lam_full * o[:, tq:2 * tq]).T.astype(o_ref.dtype)

    @pl.when(qi < nct)
    def _():
        for h in range(heads):
            s_ctx = [scores(h, j) for j in range(nct)]
            top = functools.reduce(jnp.maximum, [col_max(sj) for sj in s_ctx])
            finish(h, sum(weigh(h, j, s_ctx[j], top) for j in range(nct)))

    @pl.when(qi >= nct)
    def _():
        bufs = (sa_ref, sb_ref, sc_ref)
        per_head = [list(range(nct + unroll))] + [list(range(nct + unroll * b, nct + unroll * (b + 1)))
                                                  for b in range(1, n_blocks)]
        blocks = [(h, chunks, i == 0, i == n_blocks - 1) for h in range(heads) for i, chunks in enumerate(per_head)]
        n = len(blocks)

        def fill(g, u):
            bufs[g % 3][u] = scores(blocks[g][0], blocks[g][1][u])

        for u in range(len(blocks[0][1])):
            fill(0, u)
        top = functools.reduce(jnp.maximum, [col_max(bufs[0][u]) for u in range(len(blocks[0][1]))])
        if n > 1:
            for u in range(len(blocks[1][1])):
                fill(1, u)
        m_run, acc = None, None
        for g in range(n):
            h, chunks, first, last = blocks[g]
            if first:
                m_new, acc = top, None
            else:
                m_new = jnp.maximum(m_run, top)
                acc = acc * jnp.exp2(m_run - m_new)
            next_top = None
            for u in range(max(len(chunks), len(blocks[g + 1][1]) if g + 1 < n else 0,
                               len(blocks[g + 2][1]) if g + 2 < n else 0)):
                if g + 2 < n and u < len(blocks[g + 2][1]):
                    fill(g + 2, u)
                if u < len(chunks):
                    part = weigh(h, chunks[u], bufs[g % 3][u], m_new)
                    acc = part if acc is None else acc + part
                if g + 1 < n and u < len(blocks[g + 1][1]):
                    top_u = col_max(bufs[(g + 1) % 3][u])
                    next_top = top_u if next_top is None else jnp.maximum(next_top, top_u)
            if last:
                finish(h, acc)
            m_run, top = m_new, next_top


def _attn(qt, k, vt, lam, lambda_init, nct):
    b, t, d = k.shape
    dh2 = 2 * DA_HEAD
    nt, vrows = vt.shape[1], vt.shape[3]
    n_lat = nt - nct
    unroll = max(u for u in range(1, ATT_UNROLL + 1) if n_lat % u == 0)
    hp = ATT_HEADS
    body = functools.partial(_attn_body, nct=nct, lambda_init=lambda_init, unroll=unroll,
                             n_blocks=n_lat // unroll)
    score_buf = pltpu.VMEM((nct + unroll, ROW_TILE, 2 * ROW_TILE), F32)
    return pl.pallas_call(
        body,
        grid=(b, d // (hp * dh2), t // ROW_TILE),
        in_specs=[pl.BlockSpec(lam.shape, lambda bb, hh, qq: (0, 0)),
                  pl.BlockSpec((1, hp * dh2, ROW_TILE), lambda bb, hh, qq: (bb, hh, qq)),
                  pl.BlockSpec((1, t, hp * dh2), lambda bb, hh, qq: (bb, 0, hh)),
                  pl.BlockSpec((1, nt, hp, vrows, ROW_TILE), lambda bb, hh, qq: (bb, 0, hh, 0, 0))],
        out_specs=pl.BlockSpec((1, ROW_TILE, hp * dh2), lambda bb, hh, qq: (bb, qq, hh)),
        out_shape=jax.ShapeDtypeStruct((b, t, d), BF16),
        scratch_shapes=[score_buf, score_buf, score_buf],
        compiler_params=_params(("parallel", "parallel", "parallel"), VMEM_LIMIT_BYTES),
        name="diff_attn",
    )(lam, qt, k, vt)


def _attn_post_body(a_ref, x_ref, m_ref, ng_ref, wout_ref, o_ref, *, heads, scale):
    a = a_ref[0].astype(F32)
    dh2 = a.shape[-1] // heads
    parts = []
    for h in range(heads):
        ah = a[:, h * dh2:(h + 1) * dh2]
        parts.append(ah * lax.rsqrt(jnp.mean(ah * ah, axis=-1, keepdims=True) + DA_EPS))
    an = jnp.concatenate(parts, axis=-1) * ng_ref[...] * scale
    o_ref[0] = x_ref[0] + m_ref[0, 0][2:3] * _mm(an, wout_ref[...])


def _attn_post(a, xs, modsel, norm_g, w_out, lambda_init, nct):
    b, t, d = xs.shape
    body = functools.partial(_attn_post_body, heads=DA_HEADS, scale=1.0 - lambda_init)
    return pl.pallas_call(
        body,
        grid=(b, t // ROW_TILE),
        in_specs=[_row_spec(d), _row_spec(d), _mod_spec(d, nct), _full_spec((1, d)), _full_spec((d, d))],
        out_specs=_row_spec(d),
        out_shape=jax.ShapeDtypeStruct((b, t, d), F32),
        compiler_params=_params(("parallel", "parallel"), VMEM_LIMIT_BYTES),
        name="attn_post",
    )(a, xs, modsel, jnp.tile(norm_g, DA_HEADS).reshape(1, d), w_out.astype(BF16))


def _rope_tables(seq, ctx_len):
    rows = seq // GRID_W
    row = jnp.repeat(jnp.arange(rows), GRID_W).astype(F32)
    col = jnp.tile(jnp.arange(GRID_W), rows).astype(F32)
    nf = DA_HEAD // 4
    inv = jnp.power(ROPE_BASE, -jnp.arange(nf, dtype=F32) / nf)
    ang = jnp.concatenate([row[:, None] * inv, col[:, None] * inv], axis=-1)
    cos, sin = jnp.cos(ang), jnp.sin(ang)
    cos = jnp.concatenate([jnp.ones((ctx_len, DA_HEAD // 2), F32), cos], axis=0)
    sin = jnp.concatenate([jnp.zeros((ctx_len, DA_HEAD // 2), F32), sin], axis=0)
    reps = LANES // DA_HEAD
    return (jnp.tile(jnp.concatenate([cos, cos], axis=-1), (1, reps)),
            jnp.tile(jnp.concatenate([-sin, sin], axis=-1), (1, reps)))


def _attn_layer(xs, n1g, modsel, p, lambda_init, nct, ctx_len):
    cos, sin = _rope_tables(xs.shape[1] - ctx_len, ctx_len)
    qt, k, vt = _attn_in(xs, n1g, modsel, p["w_qkv"], cos, sin, nct)
    a = _attn(qt, k, vt, p["lam"].astype(F32), lambda_init, nct)
    return _attn_post(a, xs, modsel, p["norm_g"], p["w_out"], lambda_init, nct)


def kernel(x, c, ctx, c_ctx, ada_w, ada_b, norm1_g, norm2_g, ffn_w_in, ffn_conv_w, ffn_conv_b, ffn_w_out,
           ra_mix, ra_w_rkv, ra_w0, ra_w1, ra_w2, ra_a0, ra_a1, ra_a2, ra_g1, ra_g2, ra_k_k, ra_k_a, ra_r_k,
           ra_lnx_g, ra_lnx_b, ra_w_out, ml_w_in, ml_b_in, ml_norm_g, ml_w_out, da_w_qkv, da_lambda,
           da_norm_g, da_w_out, final_g):
    batch, seq, d = x.shape
    ctx_len = ctx.shape[1]
    assert ctx_len % ROW_TILE == 0 and seq % ROW_TILE == 0 and batch + 1 <= SUBLANES
    nct = ctx_len // ROW_TILE
    depth = ada_w.shape[0]

    xs = jnp.concatenate([ctx, x], axis=1)
    cc = jnp.concatenate([c_ctx[None], c, jnp.zeros((SUBLANES - 1 - batch, d), F32)], axis=0)
    mod = _ada(cc, ada_w, ada_b).reshape(depth, SUBLANES, 6, d)
    modsel = jnp.stack([jnp.broadcast_to(mod[:, 0:1], (depth, batch, 6, d)), mod[:, 1:1 + batch]], axis=2)

    head_id = jnp.arange(d) // RW_HEAD
    ones_bd = (head_id[:, None] == head_id[None, :]).astype(BF16)

    for i in range(depth):
        kind, j = i % N_MIXERS, i // N_MIXERS
        if kind == 0:
            p = dict(mix=ra_mix[j], w_rkv=ra_w_rkv[j], w0=ra_w0[j], w1=ra_w1[j], w2=ra_w2[j], a0=ra_a0[j],
                     a1=ra_a1[j], a2=ra_a2[j], g1=ra_g1[j], g2=ra_g2[j], k_k=ra_k_k[j], k_a=ra_k_a[j],
                     r_k=ra_r_k[j], lnx_g=ra_lnx_g[j], lnx_b=ra_lnx_b[j], w_out=ra_w_out[j])
            xs = _rwkv_layer(xs, norm1_g[i], modsel[i], p, ones_bd, nct, ctx_len)
        elif kind == 1:
            p = dict(w_in=ml_w_in[j], b_in=ml_b_in[j], norm_g=ml_norm_g[j], w_out=ml_w_out[j])
            xs = _mlstm_layer(xs, norm1_g[i], modsel[i], p, nct, ctx_len)
        else:
            lambda_init = 0.8 - 0.6 * math.exp(-0.3 * i)
            p = dict(w_qkv=da_w_qkv[j], lam=da_lambda[j], norm_g=da_norm_g[j], w_out=da_w_out[j])
            xs = _attn_layer(xs, norm1_g[i], modsel[i], p, lambda_init, nct, ctx_len)
        xs = _ffn_call(xs, norm2_g[i], modsel[i], ffn_w_in[i], ffn_conv_w[i], ffn_conv_b[i], ffn_w_out[i], nct,
                       final_g=final_g if i == depth - 1 else None)
    return xs
```

```python
import functools
import math

import jax
import jax.numpy as jnp
from jax import lax
from jax.experimental import pallas as pl
from jax.experimental.pallas import tpu as pltpu

F32 = jnp.float32
BF16 = jnp.bfloat16

DEPTH = 4
N_MIXERS = 3
GRID_W = 64
NORM_EPS = 1e-6
RW_HEAD = 64
RW_LNX_EPS = 6.4e-4
ML_HEADS = 4
ML_CHUNK = 128
DA_HEADS = 8
DA_HEAD = 64
DA_EPS = 1e-5
ROPE_BASE = 10000.0

LANES = 128
SUBLANES = 8
VMEM_LIMIT_BYTES = 56 * 1024 * 1024

ROW_TILE = 256
RW_CHUNK = 64
RW_GROUP = 4
FFN_COLS = 256
ATT_UNROLL = 4
ATT_ONES = 16
ATT_HEADS = 4


def _params(sem, vmem=None):
    return pltpu.CompilerParams(dimension_semantics=sem, vmem_limit_bytes=vmem)


def _mm(a, b):
    return jnp.dot(a.astype(BF16), b.astype(BF16), preferred_element_type=F32)


def _mm_nt(a, b):
    return lax.dot_general(a.astype(BF16), b.astype(BF16), (((1,), (1,)), ((), ())),
                           preferred_element_type=F32)


def _mm_tn(a, b):
    return lax.dot_general(a.astype(BF16), b.astype(BF16), (((0,), (0,)), ((), ())),
                           preferred_element_type=F32)


def _split3(x):
    hi = x.astype(BF16)
    r1 = x - hi.astype(F32)
    mid = r1.astype(BF16)
    lo = (r1 - mid.astype(F32)).astype(BF16)
    return hi, mid, lo


def _exact_left(m_bf16, x, terms=3):
    return sum(jnp.dot(m_bf16, p, preferred_element_type=F32) for p in _split3(x)[:terms])


def _normmod(x, g, sc, sh):
    y = x * lax.rsqrt(jnp.mean(x * x, axis=-1, keepdims=True) + NORM_EPS)
    return y * g * (1.0 + sc) + sh


def _ada_body(c_ref, w_ref, b_ref, o_ref):
    a = c_ref[...]
    a = a * jax.nn.sigmoid(a)
    o_ref[0] = jnp.dot(a, w_ref[0], precision=lax.Precision.HIGHEST,
                       preferred_element_type=F32) + b_ref[0]


def _ada(cc, ada_w, ada_b):
    depth, d, n = ada_w.shape
    nb = n // 4
    return pl.pallas_call(
        _ada_body,
        grid=(depth, n // nb),
        in_specs=[pl.BlockSpec((SUBLANES, d), lambda l, j: (0, 0)),
                  pl.BlockSpec((1, d, nb), lambda l, j: (l, 0, j)),
                  pl.BlockSpec((1, 1, nb), lambda l, j: (l, 0, j))],
        out_specs=pl.BlockSpec((1, SUBLANES, nb), lambda l, j: (l, 0, j)),
        out_shape=jax.ShapeDtypeStruct((depth, SUBLANES, n), F32),
        compiler_params=_params(("arbitrary", "arbitrary"), VMEM_LIMIT_BYTES),
        name="ada_mod",
    )(cc, ada_w, ada_b.reshape(depth, 1, n))


def _row_spec(d, tm=ROW_TILE):
    return pl.BlockSpec((1, tm, d), lambda b, t: (b, t, 0))


def _mod_spec(d, nct):
    return pl.BlockSpec((1, 1, 6, d), lambda b, t: (b, jnp.where(t >= nct, 1, 0), 0, 0))


def _full_spec(shape):
    nd = len(shape)
    return pl.BlockSpec(shape, lambda b, t: (0,) * nd)


def _halo_specs(d, tm, n_rows):
    per = tm // SUBLANES
    last = n_rows // SUBLANES - 1
    prev = pl.BlockSpec((1, SUBLANES, d), lambda b, t: (b, jnp.maximum(t * per - 1, 0), 0))
    nxt = pl.BlockSpec((1, SUBLANES, d), lambda b, t: (b, jnp.minimum((t + 1) * per, last), 0))
    return prev, nxt


def _ffn_body(x_ref, xp_ref, xn_ref, g_ref, m_ref, win_ref, cw_ref, cb_ref, wout_ref, fg_ref, o_ref, act_ref,
              *, nct, nt, t0, hidden, final):
    t = pl.program_id(1) + t0
    x = x_ref[0]
    tm = x.shape[0]
    m = m_ref[0, 0]
    xa = jnp.concatenate([xp_ref[0], x, xn_ref[0]], axis=0)
    h32 = _normmod(xa, g_ref[...], m[4:5], m[3:4])
    ha = h32.astype(BF16)
    hm = h32[SUBLANES:SUBLANES + tm].astype(BF16)
    first = jnp.logical_or(t == 0, t == nct)
    last = jnp.logical_or(t == nct - 1, t == nt - 1)
    rows = lax.broadcasted_iota(jnp.int32, (tm + 2 * SUBLANES, 1), 0)
    dead = jnp.logical_or(rows == jnp.where(first, SUBLANES - 1, -1),
                          rows == jnp.where(last, tm + SUBLANES, -1))
    keep = jnp.where(dead, 0.0, 1.0)
    for j in range(hidden // FFN_COLS):
        lo, hi = j * FFN_COLS, (j + 1) * FFN_COLS
        val = jnp.dot(hm, win_ref[:, lo:hi], preferred_element_type=F32)
        gate = jnp.dot(ha, win_ref[:, hidden + lo:hidden + hi], preferred_element_type=F32) * keep
        g_prev = pltpu.roll(gate, 1, 0)[SUBLANES:SUBLANES + tm]
        g_next = pltpu.roll(gate, tm + 2 * SUBLANES - 1, 0)[SUBLANES:SUBLANES + tm]
        g_mid = gate[SUBLANES:SUBLANES + tm]
        cw = cw_ref[:, lo:hi]
        conv = cw[0:1] * g_prev + cw[1:2] * g_mid + cw[2:3] * g_next + cb_ref[:, lo:hi]
        act_ref[:, lo:hi] = (jax.nn.gelu(conv) * val).astype(BF16)
    out = jnp.dot(act_ref[...], wout_ref[...], preferred_element_type=F32)
    y = x + m[5:6] * out
    if final:
        y = y * lax.rsqrt(jnp.mean(y * y, axis=-1, keepdims=True) + NORM_EPS) * fg_ref[...]
    o_ref[0] = y


def _ffn_call(xs, g, modsel, w_in, conv_w, conv_b, w_out, nct, final_g=None):
    b, t, d = xs.shape
    hidden = w_out.shape[0]
    nt = t // ROW_TILE
    final = final_g is not None
    t0 = nct if final else 0
    per = ROW_TILE // SUBLANES
    last = t // SUBLANES - 1
    body = functools.partial(_ffn_body, nct=nct, nt=nt, t0=t0, hidden=hidden, final=final)
    return pl.pallas_call(
        body,
        grid=(b, nt - t0),
        in_specs=[pl.BlockSpec((1, ROW_TILE, d), lambda bb, tt: (bb, tt + t0, 0)),
                  pl.BlockSpec((1, SUBLANES, d), lambda bb, tt: (bb, jnp.maximum((tt + t0) * per - 1, 0), 0)),
                  pl.BlockSpec((1, SUBLANES, d), lambda bb, tt: (bb, jnp.minimum((tt + t0 + 1) * per, last), 0)),
                  _full_spec((1, d)),
                  pl.BlockSpec((1, 1, 6, d), lambda bb, tt: (bb, jnp.where(tt + t0 >= nct, 1, 0), 0, 0)),
                  _full_spec((d, 2 * hidden)), _full_spec((3, hidden)), _full_spec((1, hidden)),
                  _full_spec((hidden, d)), _full_spec((1, d))],
        out_specs=_row_spec(d),
        out_shape=jax.ShapeDtypeStruct((b, t - t0 * ROW_TILE, d), F32),
        scratch_shapes=[pltpu.VMEM((ROW_TILE, hidden), BF16)],
        compiler_params=_params(("parallel", "parallel"), VMEM_LIMIT_BYTES),
        name="conv_ffn",
    )(xs, xs, xs, g.reshape(1, d), modsel, w_in.astype(BF16), conv_w, conv_b.reshape(1, hidden),
      w_out.astype(BF16), (final_g if final else g).reshape(1, d))


def _seg_sum(x, ones_bd):
    return jnp.dot(x.astype(BF16), ones_bd, preferred_element_type=F32)


def _rwkv_prep_body(x_ref, xp_ref, xn_ref, ng_ref, m_ref, mix_ref, wrkv_ref, w1_ref, a1_ref, g1_ref, w2_ref, a2_ref,
                    g2_ref, vec_ref, ones_ref,
                    r_ref, v_ref, kk_ref, g_ref, bonus_ref, lw_ref, kd_ref, ka_ref, *, nct, nt):
    t = pl.program_id(1)
    m = m_ref[0, 0]
    h = _normmod(x_ref[0], ng_ref[...], m[1:2], m[0:1])
    tm = h.shape[0]
    first = jnp.logical_or(t == 0, t == nct)
    last = jnp.logical_or(t == nct - 1, t == nt - 1)
    p_row = _normmod(xp_ref[0], ng_ref[...], m[1:2], m[0:1])[SUBLANES - 1:SUBLANES] * jnp.where(first, 0.0, 1.0)
    n_row = _normmod(xn_ref[0], ng_ref[...], m[1:2], m[0:1])[0:1] * jnp.where(last, 0.0, 1.0)
    rows = lax.broadcasted_iota(jnp.int32, (tm, 1), 0)
    h_prev = jnp.where(rows == 0, p_row, pltpu.roll(h, 1, 0))
    h_next = jnp.where(rows == tm - 1, n_row, pltpu.roll(h, tm - 1, 0))
    xx = 0.5 * (h_prev + h_next) - h
    mix = mix_ref[...]
    vec = vec_ref[...]
    ones_bd = ones_ref[...]

    def shifted(j):
        return h + xx * mix[j:j + 1]

    r = _mm(shifted(0), wrkv_ref[0])
    k = _mm(shifted(1), wrkv_ref[1])
    v = _mm(shifted(2), wrkv_ref[2])
    w_in = jnp.tanh(_mm(shifted(3), w1_ref[...]))
    a_in = _mm(shifted(4), a1_ref[...])
    g = _mm(jax.nn.sigmoid(_mm(shifted(5), g1_ref[...])), g2_ref[...])

    kk = k * vec[4:5]
    kk = kk / jnp.maximum(jnp.sqrt(_seg_sum(kk * kk, ones_bd)), 1e-12)
    kd_sum = None
    for d in range(2):
        w_pre = vec[d:d + 1] + _mm(w_in, w2_ref[d])
        lw_ref[d, 0] = -jax.nn.sigmoid(w_pre) * math.exp(-0.5)
        a = jax.nn.sigmoid(vec[2 + d:3 + d] + _mm(a_in, a2_ref[d]))
        kd = k * (1.0 + (a - 1.0) * vec[5:6])
        kd_ref[d, 0] = kd.astype(kd_ref.dtype)
        ka_ref[d, 0] = (kk * a).astype(ka_ref.dtype)
        kd_sum = kd if kd_sum is None else kd_sum + kd
    r_ref[0] = r.astype(r_ref.dtype)
    v_ref[0] = v.astype(v_ref.dtype)
    kk_ref[0] = kk.astype(kk_ref.dtype)
    g_ref[0] = g.astype(g_ref.dtype)
    bonus_ref[0] = (_seg_sum(r * kd_sum * vec[6:7], ones_bd) * v).astype(bonus_ref.dtype)


def _rwkv_prep(xs, ng, modsel, mix, w_rkv, w0, w1, w2, a0, a1, a2, g1, g2, k_k, k_a, r_k, ones_bd, nct):
    b, t, d = xs.shape
    nt = t // ROW_TILE
    lora = w1.shape[-1]
    prev, nxt = _halo_specs(d, ROW_TILE, t)
    zeros = jnp.zeros((lora, d), F32)
    w2p = jnp.stack([jnp.concatenate([w2[0], zeros], 0), jnp.concatenate([zeros, w2[1]], 0)]).astype(BF16)
    a2p = jnp.stack([jnp.concatenate([a2[0], zeros], 0), jnp.concatenate([zeros, a2[1]], 0)]).astype(BF16)
    w1c = jnp.concatenate([w1[0], w1[1]], axis=1).astype(BF16)
    a1c = jnp.concatenate([a1[0], a1[1]], axis=1).astype(BF16)
    vec = jnp.stack([w0[0], w0[1], a0[0], a0[1], k_k, k_a, r_k.reshape(d), jnp.zeros((d,), F32)])
    shared = jax.ShapeDtypeStruct((b, t, d), BF16)
    directional = jax.ShapeDtypeStruct((2, b, t, d), BF16)
    log_decay = jax.ShapeDtypeStruct((2, b, t, d), F32)
    dir_spec = pl.BlockSpec((2, 1, ROW_TILE, d), lambda bb, tt: (0, bb, tt, 0))
    body = functools.partial(_rwkv_prep_body, nct=nct, nt=nt)
    return pl.pallas_call(
        body,
        grid=(b, nt),
        in_specs=[_row_spec(d), prev, nxt, _full_spec((1, d)), _mod_spec(d, nct), _full_spec((6, d)),
                  _full_spec((3, d, d)), _full_spec(w1c.shape), _full_spec(a1c.shape), _full_spec(g1.shape),
                  _full_spec(w2p.shape), _full_spec(a2p.shape), _full_spec(g2.shape),
                  _full_spec((SUBLANES, d)), _full_spec((d, d))],
        out_specs=[_row_spec(d)] * 5 + [dir_spec] * 3,
        out_shape=[shared] * 5 + [log_decay, directional, directional],
        compiler_params=_params(("parallel", "parallel"), VMEM_LIMIT_BYTES),
        name="rwkv_prep",
    )(xs, xs, xs, ng.reshape(1, d), modsel, mix, w_rkv.astype(BF16), w1c, a1c, g1.astype(BF16), w2p, a2p, g2.astype(BF16), vec,
      ones_bd)


def _rwkv_scan_body(rf_ref, vf_ref, kkf_ref, lwf_ref, kdf_ref, kaf_ref, rb_ref, vb_ref, kkb_ref, lwb_ref, kdb_ref,
                    kab_ref, yf_ref, yb_ref, s_ref, *, chunk, group):
    @pl.when(pl.program_id(0) == 0)
    def _():
        s_ref[...] = jnp.zeros_like(s_ref)

    c, gc, width = chunk, group * chunk, group * RW_HEAD
    pr = lax.broadcasted_iota(jnp.int32, (2 * c, 2 * gc), 0)
    pc = lax.broadcasted_iota(jnp.int32, (2 * c, 2 * gc), 1)
    wr = lax.broadcasted_iota(jnp.int32, (c, gc), 0)
    wc = lax.broadcasted_iota(jnp.int32, (c, gc), 1)
    eye_w = jnp.where(wr == wc % c, 1.0, 0.0)
    tr = lax.broadcasted_iota(jnp.int32, (c, c), 0)
    tc = lax.broadcasted_iota(jnp.int32, (c, c), 1)
    own = (lax.broadcasted_iota(jnp.int32, (gc, width), 0) // c
           == lax.broadcasted_iota(jnp.int32, (gc, width), 1) // RW_HEAD)
    same_head = (lax.broadcasted_iota(jnp.int32, (width, width), 0) // RW_HEAD
                 == lax.broadcasted_iota(jnp.int32, (width, width), 1) // RW_HEAD)

    def stack(x):
        return jnp.where(own, jnp.concatenate([x] * group, axis=0), 0.0).astype(BF16)

    def direction(di, r_ref, v_ref, kk_ref, lw_ref, kd_ref, ka_ref, y_ref):
        sgn = 1 - 2 * di
        rows = range(r_ref.shape[0])
        chains = [(bi, hg) for bi in rows for hg in range(r_ref.shape[-1] // width)]
        lns = [slice(hg * width, (hg + 1) * width) for _, hg in chains]
        gs = range(len(chains))
        st = {}

        def prepare():
            tri = jnp.where((tr - tc) * sgn >= 0, 1.0, 0.0).astype(BF16)
            lw_all = [lw_ref[0, bi] for bi in rows]
            cum_all = [_exact_left(tri, lw_all[bi], terms=2) for bi in rows]
            tot_all = [jnp.sum(lw_all[bi], axis=0, keepdims=True) for bi in rows]
            st["left"], st["right"], st["carry_in"], st["v_nat"], st["vm"], st["decay"] = [], [], [], [], [], []
            for (bi, _), ln in zip(chains, lns):
                lw, cum, tot = lw_all[bi][:, ln], cum_all[bi][:, ln], tot_all[bi][:, ln]
                e_neg = jnp.exp(-cum)
                e_out = jnp.exp(tot - cum)
                kk, ka, kd = (ref.astype(F32) for ref in (kk_ref[bi, :, ln], ka_ref[0, bi, :, ln], kd_ref[0, bi, :, ln]))
                st["left"].append(jnp.concatenate([kk * jnp.exp(cum - lw),
                                                   r_ref[bi, :, ln].astype(F32) * jnp.exp(cum)], axis=0).astype(BF16))
                st["right"].append(jnp.concatenate([stack(ka * e_neg), stack(kd * e_neg)], axis=0))
                st["carry_in"].append(jnp.concatenate([ka * e_out, kd * e_out], axis=0).astype(BF16))
                st["v_nat"].append(v_ref[bi, :, ln].astype(BF16))
                st["vm"].append(stack(v_ref[bi, :, ln].astype(F32)))
                st["decay"].append(jnp.exp(tot))

        def pair_products():
            visible = ((pr % c) - (pc % c)) * sgn > jnp.where(pr < c, 0, -1)
            st["pair"] = [jnp.where(visible, _mm_nt(st["left"][g], st["right"][g]), 0.0) for g in gs]
            st["inv"] = [eye_w - st["pair"][g][0:c, 0:gc] for g in gs]
            st["xp"] = [_mm(st["pair"][g][0:c, 0:gc], stack(st["pair"][g][0:c, 0:gc])) for g in gs]

        def double():
            both = [_mm(jnp.concatenate([st["inv"][g], st["xp"][g]], axis=0), stack(st["xp"][g])) for g in gs]
            st["inv"] = [st["inv"][g] + both[g][0:c] for g in gs]
            st["xp"] = [both[g][c:2 * c] for g in gs]

        def last_double():
            st["inv"] = [st["inv"][g] + _mm(st["inv"][g], stack(st["xp"][g])) for g in gs]

        def residual():
            st["resid"] = [(eye_w - st["inv"][g]) - _mm(st["pair"][g][0:c, 0:gc], stack(st["inv"][g])) for g in gs]

        def newton():
            st["inv"] = [st["inv"][g] + _mm(st["inv"][g], stack(st["resid"][g])) for g in gs]

        def read_state():
            st["s"] = [s_ref[di, chains[g][0], chains[g][1]] for g in gs]
            st["from_state"] = [_mm_nt(st["left"][g], st["s"][g]) for g in gs]
            st["from_v"] = [_mm(st["pair"][g][:, gc:2 * gc], st["vm"][g]) for g in gs]

        def solve():
            st["u"] = [-_mm(st["inv"][g], stack(st["from_state"][g][0:c] + st["from_v"][g][0:c])) for g in gs]

        def emit():
            for g in gs:
                bi, hg = chains[g]
                y_ref[bi, :, lns[g]] = (st["from_state"][g][c:2 * c] + st["from_v"][g][c:2 * c]
                                        + _mm(st["pair"][g][c:2 * c, 0:gc], stack(st["u"][g]))).astype(y_ref.dtype)
                grown = _mm_tn(jnp.concatenate([st["u"][g].astype(BF16), st["v_nat"][g]], axis=0),
                               st["carry_in"][g])
                s_ref[di, bi, hg] = st["s"][g] * st["decay"][g] + jnp.where(same_head, grown, 0.0)

        return ([prepare, pair_products] + [double] * (int(math.log2(c)) - 3)
                + [last_double, residual, newton, read_state, solve, emit])

    forward = direction(0, rf_ref, vf_ref, kkf_ref, lwf_ref, kdf_ref, kaf_ref, yf_ref)
    backward = direction(1, rb_ref, vb_ref, kkb_ref, lwb_ref, kdb_ref, kab_ref, yb_ref)
    for stage_f, stage_b in zip(forward, backward):
        stage_f()
        stage_b()


def _rwkv_scan(r, v, kk, lw, kd, ka, ctx_len):
    b, t, d = r.shape
    c = RW_CHUNK
    width = RW_GROUP * RW_HEAD
    nc, nck = t // c, ctx_len // c

    def back(i):
        return jnp.where(i < nck, nck - 1 - i, nc - 1 - (i - nck))

    fwd = pl.BlockSpec((b, c, d), lambda i: (0, i, 0))
    bwd = pl.BlockSpec((b, c, d), lambda i: (0, back(i), 0))
    fwd_dir = pl.BlockSpec((1, b, c, d), lambda i: (0, 0, i, 0))
    bwd_dir = pl.BlockSpec((1, b, c, d), lambda i: (1, 0, back(i), 0))
    body = functools.partial(_rwkv_scan_body, chunk=c, group=RW_GROUP)
    return pl.pallas_call(
        body,
        grid=(nc,),
        in_specs=[fwd, fwd, fwd, fwd_dir, fwd_dir, fwd_dir, bwd, bwd, bwd, bwd_dir, bwd_dir, bwd_dir],
        out_specs=[fwd, bwd],
        out_shape=[jax.ShapeDtypeStruct((b, t, d), BF16)] * 2,
        scratch_shapes=[pltpu.VMEM((2, b, d // width, width, width), F32)],
        compiler_params=_params(("arbitrary",), VMEM_LIMIT_BYTES),
        name="rwkv_scan",
    )(r, v, kk, lw, kd, ka, r, v, kk, lw, kd, ka)


def _rwkv_post_body(yf_ref, yb_ref, bonus_ref, g_ref, x_ref, m_ref, lng_ref, lnb_ref, ones_ref, wout_ref, o_ref):
    ones_bd = ones_ref[...]
    y = yf_ref[0].astype(F32) + yb_ref[0].astype(F32)
    mu = _seg_sum(y, ones_bd) * (1.0 / RW_HEAD)
    yc = y - mu
    var = _seg_sum(yc * yc, ones_bd) * (1.0 / RW_HEAD)
    yn = yc * lax.rsqrt(var + RW_LNX_EPS) * lng_ref[...] + lnb_ref[...]
    out = _mm((yn + bonus_ref[0].astype(F32)) * g_ref[0].astype(F32), wout_ref[...])
    o_ref[0] = x_ref[0] + m_ref[0, 0][2:3] * out


def _rwkv_post(yf, yb, bonus, g, xs, modsel, lnx_g, lnx_b, ones_bd, w_out, nct):
    b, t, d = xs.shape
    return pl.pallas_call(
        _rwkv_post_body,
        grid=(b, t // ROW_TILE),
        in_specs=[_row_spec(d)] * 5 + [_mod_spec(d, nct), _full_spec((1, d)), _full_spec((1, d)),
                                      _full_spec((d, d)), _full_spec((d, d))],
        out_specs=_row_spec(d),
        out_shape=jax.ShapeDtypeStruct((b, t, d), F32),
        compiler_params=_params(("parallel", "parallel"), VMEM_LIMIT_BYTES),
        name="rwkv_post",
    )(yf, yb, bonus, g, xs, modsel, lnx_g.reshape(1, d), lnx_b.reshape(1, d), ones_bd, w_out.astype(BF16))


def _rwkv_layer(xs, n1g, modsel, p, ones_bd, nct, ctx_len):
    r, v, kk, g, bonus, lw, kd, ka = _rwkv_prep(
        xs, n1g, modsel, p["mix"], p["w_rkv"], p["w0"], p["w1"], p["w2"], p["a0"], p["a1"], p["a2"], p["g1"], p["g2"],
        p["k_k"], p["k_a"], p["r_k"], ones_bd, nct)
    yf, yb = _rwkv_scan(r, v, kk, lw, kd, ka, ctx_len)
    return _rwkv_post(yf, yb, bonus, g, xs, modsel, p["lnx_g"], p["lnx_b"], ones_bd, p["w_out"], nct)


def _mlstm_in_body(x_ref, g_ref, m_ref, w_ref, b_ref, q_ref, k_ref, v_ref, o_ref, gf_ref, gb_ref,
                   *, heads, dk, dv):
    m = m_ref[0, 0]
    h = _normmod(x_ref[0], g_ref[...], m[1:2], m[0:1])
    z = _mm(h, w_ref[...]) + b_ref[...]
    nqk = heads * dk
    nv = heads * dv
    q_ref[0] = z[:, 0:nqk].astype(q_ref.dtype)
    k_ref[0] = (z[:, nqk:2 * nqk] * (dk ** -0.5)).astype(k_ref.dtype)
    v_ref[0] = z[:, 2 * nqk:2 * nqk + nv].astype(v_ref.dtype)
    d_model = o_ref.shape[-1]
    o_ref[0] = jax.nn.sigmoid(z[:, 2 * nqk + nv:2 * nqk + nv + d_model]).astype(o_ref.dtype)
    gates = z[:, 2 * nqk + nv + d_model:]
    lane = lax.broadcasted_iota(jnp.int32, (1, LANES), 1)
    is_forget = jnp.logical_and(lane >= heads, lane < 2 * heads)
    for ref, off in ((gf_ref, 0), (gb_ref, LANES)):
        gd = gates[:, off:off + LANES]
        ref[0] = jnp.where(is_forget, jax.nn.log_sigmoid(gd), gd)


def _mlstm_in(xs, g, modsel, w_in, b_in, nct):
    b, t, d = xs.shape
    heads, dk, dv = ML_HEADS, d // (2 * ML_HEADS), d // ML_HEADS
    main = 2 * heads * dk + heads * dv + d
    pad = jnp.zeros((d, LANES - 2 * heads), F32)
    w_cat = jnp.concatenate([w_in[:, :main], w_in[:, main:main + 2 * heads], pad,
                             w_in[:, main + 2 * heads:], pad], axis=1)
    padb = jnp.zeros((LANES - 2 * heads,), F32)
    b_cat = jnp.concatenate([b_in[:main], b_in[main:main + 2 * heads], padb, b_in[main + 2 * heads:], padb])
    n = w_cat.shape[1]
    body = functools.partial(_mlstm_in_body, heads=heads, dk=dk, dv=dv)
    shapes = [(heads * dk, BF16), (heads * dk, BF16), (heads * dv, BF16), (d, BF16), (LANES, F32), (LANES, F32)]
    return pl.pallas_call(
        body,
        grid=(b, t // ROW_TILE),
        in_specs=[_row_spec(d), _full_spec((1, d)), _mod_spec(d, nct), _full_spec((d, n)), _full_spec((1, n))],
        out_specs=[_row_spec(s) for s, _ in shapes],
        out_shape=[jax.ShapeDtypeStruct((b, t, s), dt) for s, dt in shapes],
        compiler_params=_params(("parallel", "parallel"), VMEM_LIMIT_BYTES),
        name="mlstm_in",
    )(xs, g.reshape(1, d), modsel, w_cat.astype(BF16), b_cat.reshape(1, n))


def _mlstm_chunk_body(qf_ref, kf_ref, vf_ref, gcf_ref, grf_ref, qb_ref, kb_ref, vb_ref, gcb_ref, grb_ref,
                      hf_ref, hb_ref, c_ref, n_ref, m_ref, *, heads, dk, dv):
    @pl.when(pl.program_id(1) == 0)
    def _():
        c_ref[...] = jnp.zeros_like(c_ref)
        n_ref[...] = jnp.zeros_like(n_ref)
        m_ref[...] = jnp.zeros_like(m_ref)

    t = qf_ref.shape[1]
    rr = lax.broadcasted_iota(jnp.int32, (t, t), 0)
    cc = lax.broadcasted_iota(jnp.int32, (t, t), 1)
    chains, seen = [], []
    li_col, li_row, b_col, b_row, b_end, q32, q, k, v = [], [], [], [], [], [], [], [], []
    for di, (q_ref, k_ref, v_ref, gc_ref, gr_ref, h_ref) in enumerate(
            ((qf_ref, kf_ref, vf_ref, gcf_ref, grf_ref, hf_ref), (qb_ref, kb_ref, vb_ref, gcb_ref, grb_ref, hb_ref))):
        sgn = 1 - 2 * di
        vis = (rr - cc) * sgn >= 0
        tri = jnp.where(vis, 1.0, 0.0).astype(BF16)
        gcol = gc_ref[0, 0]
        grow = gr_ref[0, 0]
        bcol = _exact_left(tri, gcol)
        brow = sum(_mm_nt(p, tri) for p in _split3(grow))
        for h in range(heads):
            chains.append((di, h, h_ref))
            seen.append(vis)
            li_col.append(gcol[:, h:h + 1])
            li_row.append(grow[h:h + 1, :])
            b_col.append(bcol[:, heads + h:heads + h + 1])
            b_row.append(brow[heads + h:heads + h + 1, :])
            b_end.append(jnp.sum(gcol[:, heads + h:heads + h + 1], axis=0, keepdims=True))
            q32.append(q_ref[0, :, h * dk:(h + 1) * dk].astype(F32))
            q.append(q_ref[0, :, h * dk:(h + 1) * dk].astype(BF16))
            k.append(k_ref[0, :, h * dk:(h + 1) * dk].astype(F32))
            v.append(v_ref[0, :, h * dv:(h + 1) * dv].astype(BF16))
    cs = range(len(chains))
    c0 = [c_ref[chains[i][0], chains[i][1]] for i in cs]
    n0 = [n_ref[chains[i][0], chains[i][1]] for i in cs]
    m0 = [m_ref[chains[i][0], chains[i][1]][:, 0:1] for i in cs]
    g_end = [b_end[i] - b_col[i] + li_col[i] for i in cs]
    m_loc = [jnp.max(g_end[i], axis=0, keepdims=True) for i in cs]
    kw = [k[i] * jnp.exp(g_end[i] - m_loc[i]) for i in cs]
    c_loc = [_mm_tn(kw[i], v[i]) for i in cs]
    n_loc = [jnp.sum(kw[i], axis=0, keepdims=True) for i in cs]
    qk = [_mm_nt(q[i], k[i]) for i in cs]
    qc = [_mm(q[i], c0[i]) for i in cs]
    d_log = [jnp.where(seen[i], b_col[i] - b_row[i] + li_row[i], -jnp.inf) for i in cs]
    inter = [b_col[i] + m0[i] for i in cs]
    m_t = [jnp.maximum(jnp.max(d_log[i], axis=-1, keepdims=True), inter[i]) for i in cs]
    p = [jnp.exp(d_log[i] - m_t[i]) * qk[i] for i in cs]
    e_inter = [jnp.exp(inter[i] - m_t[i]) for i in cs]
    num = [_mm(p[i], v[i]) + e_inter[i] * qc[i] for i in cs]
    for i in cs:
        di, h, h_ref = chains[i]
        den = (jnp.sum(p[i], axis=-1, keepdims=True)
               + e_inter[i] * jnp.sum(q32[i] * n0[i], axis=-1, keepdims=True))
        h_ref[0, :, h * dv:(h + 1) * dv] = (num[i] / jnp.maximum(jnp.abs(den), jnp.exp(-m_t[i]))).astype(h_ref.dtype)
        m_new = jnp.maximum(b_end[i] + m0[i], m_loc[i])
        a = jnp.exp(b_end[i] + m0[i] - m_new)
        e = jnp.exp(m_loc[i] - m_new)
        c_ref[di, h] = a * c0[i] + e * c_loc[i]
        n_ref[di, h] = a * n0[i] + e * n_loc[i]
        m_ref[di, h] = jnp.broadcast_to(m_new, (1, LANES))


def _mlstm_chunks(q, k, v, gcol, grow, ctx_len):
    b, t, _ = q.shape
    heads = ML_HEADS
    dk, dv = q.shape[-1] // heads, v.shape[-1] // heads
    c = ML_CHUNK
    nc, nck = t // c, ctx_len // c

    def back(i):
        return jnp.where(i < nck, nck - 1 - i, nc - 1 - (i - nck))

    def specs(chunk_of, di):
        seq = lambda width: pl.BlockSpec((1, c, width), lambda bb, i: (bb, chunk_of(i), 0))
        return [seq(heads * dk), seq(heads * dk), seq(heads * dv),
                pl.BlockSpec((1, 1, c, LANES), lambda bb, i: (di, bb, chunk_of(i), 0)),
                pl.BlockSpec((1, 1, SUBLANES, c), lambda bb, i: (di, bb, 0, chunk_of(i)))]

    body = functools.partial(_mlstm_chunk_body, heads=heads, dk=dk, dv=dv)
    return pl.pallas_call(
        body,
        grid=(b, nc),
        in_specs=specs(lambda i: i, 0) + specs(back, 1),
        out_specs=[pl.BlockSpec((1, c, heads * dv), lambda bb, i: (bb, i, 0)),
                   pl.BlockSpec((1, c, heads * dv), lambda bb, i: (bb, back(i), 0))],
        out_shape=[jax.ShapeDtypeStruct((b, t, heads * dv), BF16)] * 2,
        scratch_shapes=[pltpu.VMEM((2, heads, dk, dv), F32), pltpu.VMEM((2, heads, 1, dk), F32),
                        pltpu.VMEM((2, heads, 1, LANES), F32)],
        compiler_params=_params(("parallel", "arbitrary")),
        name="mlstm_chunks",
    )(q, k, v, gcol, grow, q, k, v, gcol, grow)


def _mlstm_post_body(hf_ref, hb_ref, o_ref, x_ref, m_ref, ng_ref, wout_ref, out_ref, *, heads):
    hs = hf_ref[0].astype(F32) + hb_ref[0].astype(F32)
    dv = hs.shape[-1] // heads
    parts = []
    for h in range(heads):
        hh = hs[:, h * dv:(h + 1) * dv]
        parts.append(hh * lax.rsqrt(jnp.mean(hh * hh, axis=-1, keepdims=True) + NORM_EPS))
    hn = jnp.concatenate(parts, axis=-1)
    out = _mm(hn * ng_ref[...] * o_ref[0].astype(F32), wout_ref[...])
    out_ref[0] = x_ref[0] + m_ref[0, 0][2:3] * out


def _mlstm_post(hf, hb, o, xs, modsel, norm_g, w_out, nct):
    b, t, d = xs.shape
    body = functools.partial(_mlstm_post_body, heads=ML_HEADS)
    return pl.pallas_call(
        body,
        grid=(b, t // ROW_TILE),
        in_specs=[_row_spec(d)] * 4 + [_mod_spec(d, nct), _full_spec((1, d)), _full_spec((d, d))],
        out_specs=_row_spec(d),
        out_shape=jax.ShapeDtypeStruct((b, t, d), F32),
        compiler_params=_params(("parallel", "parallel"), VMEM_LIMIT_BYTES),
        name="mlstm_post",
    )(hf, hb, o, xs, modsel, norm_g.reshape(1, d), w_out.astype(BF16))


def _mlstm_layer(xs, n1g, modsel, p, nct, ctx_len):
    q, k, v, o, gf, gb = _mlstm_in(xs, n1g, modsel, p["w_in"], p["b_in"], nct)
    gcol = jnp.stack([gf, gb])
    grow = jnp.swapaxes(gcol[..., :SUBLANES], -1, -2)
    hf, hb = _mlstm_chunks(q, k, v, gcol, grow, ctx_len)
    return _mlstm_post(hf, hb, o, xs, modsel, p["norm_g"], p["w_out"], nct)


def _attn_in_body(x_ref, g_ref, m_ref, w_ref, cos_ref, sin_ref, qt_ref, k_ref, vt_ref, *, d_model):
    m = m_ref[0, 0]
    h = _normmod(x_ref[0], g_ref[...], m[1:2], m[0:1])
    z = _mm(h, w_ref[...])
    reps = d_model // LANES
    cos = jnp.concatenate([cos_ref[...]] * reps, axis=1)
    sin = jnp.concatenate([sin_ref[...]] * reps, axis=1)
    lane = lax.broadcasted_iota(jnp.int32, (1, d_model), 1)
    low = (lane % DA_HEAD) < (DA_HEAD // 2)
    half = DA_HEAD // 2

    def rope(u):
        partner = jnp.where(low, pltpu.roll(u, d_model - half, 1), pltpu.roll(u, half, 1))
        return u * cos + partner * sin

    q = rope(z[:, 0:d_model]) * (DA_HEAD ** -0.5 * math.log2(math.e))
    qt_ref[0] = q.T.astype(BF16)
    k_ref[0] = rope(z[:, d_model:2 * d_model]).astype(BF16)
    vt = z[:, 2 * d_model:].T.astype(BF16)
    dh2 = 2 * DA_HEAD
    ones = jnp.ones((ATT_ONES, vt.shape[1]), BF16)
    for hh in range(d_model // dh2):
        vt_ref[0, 0, hh] = jnp.concatenate([vt[hh * dh2:(hh + 1) * dh2], ones], axis=0)


def _attn_in(xs, g, modsel, w_qkv, cos, sin, nct):
    b, t, d = xs.shape
    nt = t // ROW_TILE
    heads, vrows = d // (2 * DA_HEAD), 2 * DA_HEAD + ATT_ONES
    body = functools.partial(_attn_in_body, d_model=d)
    tab = pl.BlockSpec((ROW_TILE, LANES), lambda bb, tt: (tt, 0))
    return pl.pallas_call(
        body,
        grid=(b, nt),
        in_specs=[_row_spec(d), _full_spec((1, d)), _mod_spec(d, nct), _full_spec((d, 3 * d)), tab, tab],
        out_specs=[pl.BlockSpec((1, d, ROW_TILE), lambda bb, tt: (bb, 0, tt)), _row_spec(d),
                   pl.BlockSpec((1, 1, heads, vrows, ROW_TILE), lambda bb, tt: (bb, tt, 0, 0, 0))],
        out_shape=[jax.ShapeDtypeStruct((b, d, t), BF16), jax.ShapeDtypeStruct((b, t, d), BF16),
                   jax.ShapeDtypeStruct((b, nt, heads, vrows, ROW_TILE), BF16)],
        compiler_params=_params(("parallel", "parallel"), VMEM_LIMIT_BYTES),
        name="attn_in",
    )(xs, g.reshape(1, d), modsel, w_qkv.astype(BF16), cos, sin)


def _attn_body(lam_ref, qt_ref, k_ref, vt_ref, o_ref, sa_ref, sb_ref, sc_ref, *, nct, lambda_init, unroll, n_blocks):
    qi = pl.program_id(2)
    dh2 = 2 * DA_HEAD
    heads = vt_ref.shape[2]
    tq = qt_ref.shape[-1]
    tk = vt_ref.shape[-1]
    zero = jnp.zeros((DA_HEAD, tq), qt_ref.dtype)
    qs = []
    for h in range(heads):
        qt = qt_ref[0, h * dh2:(h + 1) * dh2, :]
        qs.append(jnp.concatenate([jnp.concatenate([qt[0:DA_HEAD], zero], axis=1),
                                   jnp.concatenate([zero, qt[DA_HEAD:dh2]], axis=1)], axis=0))
    lam = lam_ref[...]
    lam_full = (jnp.exp(jnp.sum(lam[0:1] * lam[1:2], axis=-1, keepdims=True))
                - jnp.exp(jnp.sum(lam[2:3] * lam[3:4], axis=-1, keepdims=True)) + lambda_init)

    def scores(h, j):
        return jnp.dot(k_ref[0, j * tk:(j + 1) * tk, h * dh2:(h + 1) * dh2], qs[h],
                       preferred_element_type=F32)

    def weigh(h, j, sj, m_new):
        halves = [jnp.dot(vt_ref[0, j, h], jnp.exp2(sj[:, lo:lo + tq] - m_new[:, lo:lo + tq]).astype(BF16),
                          preferred_element_type=F32) for lo in (0, tq)]
        return jnp.concatenate(halves, axis=1)

    def col_max(s):
        return jnp.max(s, axis=0, keepdims=True)

    def finish(h, acc):
        o = acc[0:dh2] / acc[dh2:dh2 + 1]
        o_ref[0, :, h * dh2:(h + 1) * dh2] = (o[:, 0:tq] - lam_full * o[:, tq:2 * tq]).T.astype(o_ref.dtype)

    @pl.when(qi < nct)
    def _():
        for h in range(heads):
            s_ctx = [scores(h, j) for j in range(nct)]
            top = functools.reduce(jnp.maximum, [col_max(sj) for sj in s_ctx])
            finish(h, sum(weigh(h, j, s_ctx[j], top) for j in range(nct)))

    @pl.when(qi >= nct)
    def _():
        bufs = (sa_ref, sb_ref, sc_ref)
        per_head = [list(range(nct + unroll))] + [list(range(nct + unroll * b, nct + unroll * (b + 1)))
                                                  for b in range(1, n_blocks)]
        blocks = [(h, chunks, i == 0, i == n_blocks - 1) for h in range(heads) for i, chunks in enumerate(per_head)]
        n = len(blocks)

        def fill(g, u):
            bufs[g % 3][u] = scores(blocks[g][0], blocks[g][1][u])

        for u in range(len(blocks[0][1])):
            fill(0, u)
        top = functools.reduce(jnp.maximum, [col_max(bufs[0][u]) for u in range(len(blocks[0][1]))])
        if n > 1:
            for u in range(len(blocks[1][1])):
                fill(1, u)
        m_run, acc = None, None
        for g in range(n):
            h, chunks, first, last = blocks[g]
            if first:
                m_new, acc = top, None
            else:
                m_new = jnp.maximum(m_run, top)
                acc = acc * jnp.exp2(m_run - m_new)
            next_top = None
            for u in range(max(len(chunks), len(blocks[g + 1][1]) if g + 1 < n else 0,
                               len(blocks[g + 2][1]) if g + 2 < n else 0)):
                if g + 2 < n and u < len(blocks[g + 2][1]):
                    fill(g + 2, u)
                if u < len(chunks):
                    part = weigh(h, chunks[u], bufs[g % 3][u], m_new)
                    acc = part if acc is None else acc + part
                if g + 1 < n and u < len(blocks[g + 1][1]):
                    top_u = col_max(bufs[(g + 1) % 3][u])
                    next_top = top_u if next_top is None else jnp.maximum(next_top, top_u)
            if last:
                finish(h, acc)
            m_run, top = m_new, next_top


def _attn(qt, k, vt, lam, lambda_init, nct):
    b, t, d = k.shape
    dh2 = 2 * DA_HEAD
    nt, vrows = vt.shape[1], vt.shape[3]
    n_lat = nt - nct
    unroll = max(u for u in range(1, ATT_UNROLL + 1) if n_lat % u == 0)
    hp = ATT_HEADS
    body = functools.partial(_attn_body, nct=nct, lambda_init=lambda_init, unroll=unroll,
                             n_blocks=n_lat // unroll)
    score_buf = pltpu.VMEM((nct + unroll, ROW_TILE, 2 * ROW_TILE), F32)
    return pl.pallas_call(
        body,
        grid=(b, d // (hp * dh2), t // ROW_TILE),
        in_specs=[pl.BlockSpec(lam.shape, lambda bb, hh, qq: (0, 0)),
                  pl.BlockSpec((1, hp * dh2, ROW_TILE), lambda bb, hh, qq: (bb, hh, qq)),
                  pl.BlockSpec((1, t, hp * dh2), lambda bb, hh, qq: (bb, 0, hh)),
                  pl.BlockSpec((1, nt, hp, vrows, ROW_TILE), lambda bb, hh, qq: (bb, 0, hh, 0, 0))],
        out_specs=pl.BlockSpec((1, ROW_TILE, hp * dh2), lambda bb, hh, qq: (bb, qq, hh)),
        out_shape=jax.ShapeDtypeStruct((b, t, d), BF16),
        scratch_shapes=[score_buf, score_buf, score_buf],
        compiler_params=_params(("parallel", "parallel", "parallel"), VMEM_LIMIT_BYTES),
        name="diff_attn",
    )(lam, qt, k, vt)


def _attn_post_body(a_ref, x_ref, m_ref, ng_ref, wout_ref, o_ref, *, heads, scale):
    a = a_ref[0].astype(F32)
    dh2 = a.shape[-1] // heads
    parts = []
    for h in range(heads):
        ah = a[:, h * dh2:(h + 1) * dh2]
        parts.append(ah * lax.rsqrt(jnp.mean(ah * ah, axis=-1, keepdims=True) + DA_EPS))
    an = jnp.concatenate(parts, axis=-1) * ng_ref[...] * scale
    o_ref[0] = x_ref[0] + m_ref[0, 0][2:3] * _mm(an, wout_ref[...])


def _attn_post(a, xs, modsel, norm_g, w_out, lambda_init, nct):
    b, t, d = xs.shape
    body = functools.partial(_attn_post_body, heads=DA_HEADS, scale=1.0 - lambda_init)
    return pl.pallas_call(
        body,
        grid=(b, t // ROW_TILE),
        in_specs=[_row_spec(d), _row_spec(d), _mod_spec(d, nct), _full_spec((1, d)), _full_spec((d, d))],
        out_specs=_row_spec(d),
        out_shape=jax.ShapeDtypeStruct((b, t, d), F32),
        compiler_params=_params(("parallel", "parallel"), VMEM_LIMIT_BYTES),
        name="attn_post",
    )(a, xs, modsel, jnp.tile(norm_g, DA_HEADS).reshape(1, d), w_out.astype(BF16))


def _rope_tables(seq, ctx_len):
    rows = seq // GRID_W
    row = jnp.repeat(jnp.arange(rows), GRID_W).astype(F32)
    col = jnp.tile(jnp.arange(GRID_W), rows).astype(F32)
    nf = DA_HEAD // 4
    inv = jnp.power(ROPE_BASE, -jnp.arange(nf, dtype=F32) / nf)
    ang = jnp.concatenate([row[:, None] * inv, col[:, None] * inv], axis=-1)
    cos, sin = jnp.cos(ang), jnp.sin(ang)
    cos = jnp.concatenate([jnp.ones((ctx_len, DA_HEAD // 2), F32), cos], axis=0)
    sin = jnp.concatenate([jnp.zeros((ctx_len, DA_HEAD // 2), F32), sin], axis=0)
    reps = LANES // DA_HEAD
    return (jnp.tile(jnp.concatenate([cos, cos], axis=-1), (1, reps)),
            jnp.tile(jnp.concatenate([-sin, sin], axis=-1), (1, reps)))


def _attn_layer(xs, n1g, modsel, p, lambda_init, nct, ctx_len):
    cos, sin = _rope_tables(xs.shape[1] - ctx_len, ctx_len)
    qt, k, vt = _attn_in(xs, n1g, modsel, p["w_qkv"], cos, sin, nct)
    a = _attn(qt, k, vt, p["lam"].astype(F32), lambda_init, nct)
    return _attn_post(a, xs, modsel, p["norm_g"], p["w_out"], lambda_init, nct)


def kernel(x, c, ctx, c_ctx, ada_w, ada_b, norm1_g, norm2_g, ffn_w_in, ffn_conv_w, ffn_conv_b, ffn_w_out,
           ra_mix, ra_w_rkv, ra_w0, ra_w1, ra_w2, ra_a0, ra_a1, ra_a2, ra_g1, ra_g2, ra_k_k, ra_k_a, ra_r_k,
           ra_lnx_g, ra_lnx_b, ra_w_out, ml_w_in, ml_b_in, ml_norm_g, ml_w_out, da_w_qkv, da_lambda,
           da_norm_g, da_w_out, final_g):
    batch, seq, d = x.shape
    ctx_len = ctx.shape[1]
    assert ctx_len % ROW_TILE == 0 and seq % ROW_TILE == 0 and batch + 1 <= SUBLANES
    nct = ctx_len // ROW_TILE
    depth = ada_w.shape[0]

    xs = jnp.concatenate([ctx, x], axis=1)
    cc = jnp.concatenate([c_ctx[None], c, jnp.zeros((SUBLANES - 1 - batch, d), F32)], axis=0)
    mod = _ada(cc, ada_w, ada_b).reshape(depth, SUBLANES, 6, d)
    modsel = jnp.stack([jnp.broadcast_to(mod[:, 0:1], (depth, batch, 6, d)), mod[:, 1:1 + batch]], axis=2)

    head_id = jnp.arange(d) // RW_HEAD
    ones_bd = (head_id[:, None] == head_id[None, :]).astype(BF16)

    for i in range(depth):
        kind, j = i % N_MIXERS, i // N_MIXERS
        if kind == 0:
            p = dict(mix=ra_mix[j], w_rkv=ra_w_rkv[j], w0=ra_w0[j], w1=ra_w1[j], w2=ra_w2[j], a0=ra_a0[j],
                     a1=ra_a1[j], a2=ra_a2[j], g1=ra_g1[j], g2=ra_g2[j], k_k=ra_k_k[j], k_a=ra_k_a[j],
                     r_k=ra_r_k[j], lnx_g=ra_lnx_g[j], lnx_b=ra_lnx_b[j], w_out=ra_w_out[j])
            xs = _rwkv_layer(xs, norm1_g[i], modsel[i], p, ones_bd, nct, ctx_len)
        elif kind == 1:
            p = dict(w_in=ml_w_in[j], b_in=ml_b_in[j], norm_g=ml_norm_g[j], w_out=ml_w_out[j])
            xs = _mlstm_layer(xs, norm1_g[i], modsel[i], p, nct, ctx_len)
        else:
            lambda_init = 0.8 - 0.6 * math.exp(-0.3 * i)
            p = dict(w_qkv=da_w_qkv[j], lam=da_lambda[j], norm_g=da_norm_g[j], w_out=da_w_out[j])
            xs = _attn_layer(xs, norm1_g[i], modsel[i], p, lambda_init, nct, ctx_len)
        xs = _ffn_call(xs, norm2_g[i], modsel[i], ffn_w_in[i], ffn_conv_w[i], ffn_conv_b[i], ffn_w_out[i], nct,
                       final_g=final_g if i == depth - 1 else None)
    return xs
```
